```python
import math
import jax, jax.numpy as jnp
from jax import lax
import numpy as np

D_MODEL = 1024
BATCH = 4
SEQ = 4096
DEPTH = 4
DEC_BATCH = 32
DEC_SEQ = 1
PAST_LEN = 8192
PAGE_SIZE = 128

MIX_WIDTH = D_MODEL
CONV_DIM = MIX_WIDTH // 4
SSM_DIM = MIX_WIDTH // 4
ATT_DIM = MIX_WIDTH - CONV_DIM - SSM_DIM
HEAD_DIM = 64
N_ATT_HEADS = ATT_DIM // HEAD_DIM
CONV_K = 3
SSM_GROUP = 16
SSM_GROUPS = SSM_DIM // SSM_GROUP
SSM_STATE = 64
D_FF = -(-8 * D_MODEL // (3 * 256)) * 256
PLE_DIM = 256
Q_BLOCK = 128
EPS = 1e-6
FORGET_BIAS = 3.0
IN_COLS = 3 * CONV_DIM + SSM_DIM + 3 * ATT_DIM + N_ATT_HEADS

kernel_name = "hymba_conv_s5_fox_decoder_step"


def rmsnorm(x, g):
    xf = x.astype(jnp.float32)
    y = xf * lax.rsqrt(jnp.mean(xf * xf, axis=-1, keepdims=True) + EPS)
    return (y * g.astype(jnp.float32)).astype(x.dtype)


def short_conv_mixer(b_gate, c_gate, v_in, buf, w):
    u = c_gate * v_in
    t = u.shape[1]
    ext = jnp.concatenate([buf.astype(u.dtype), u], axis=1)
    y = sum(w[j] * ext[:, j:j + t] for j in range(CONV_K))
    return b_gate * y, ext[:, -(CONV_K - 1):]


def _lin_combine(e1, e2):
    a1, b1 = e1
    a2, b2 = e2
    return a1 * a2, a2 * b1 + b2


def ssm_mixer(u, h0, a_re, a_im, log_dt, b_re, b_im, c_re, c_im, d, w_glu):
    f32 = jnp.float32
    bsz, t, _ = u.shape
    uf = u.astype(f32).reshape(bsz, t, SSM_GROUPS, SSM_GROUP)
    lam = lax.complex(a_re.astype(f32), a_im.astype(f32))
    dt = jnp.exp(log_dt.astype(f32))[:, None]
    lam_bar = jnp.exp(lam * dt)
    b_bar = ((lam_bar - 1.0) / lam)[:, :, None] * lax.complex(b_re.astype(f32), b_im.astype(f32))
    bu = jnp.einsum('gpc,btgc->btgp', b_bar, uf.astype(jnp.complex64))
    a_seq = jnp.broadcast_to(lam_bar, bu.shape)
    a_cum, h = lax.associative_scan(_lin_combine, (a_seq, bu), axis=1)
    h = h + a_cum * h0[:, None]
    c = lax.complex(c_re.astype(f32), c_im.astype(f32))
    y = jnp.real(jnp.einsum('gcp,btgp->btgc', c, h)) + d.astype(f32).reshape(SSM_GROUPS, SSM_GROUP) * uf
    y = jax.nn.gelu(y.reshape(bsz, t, SSM_DIM))
    out = y * jax.nn.sigmoid(y @ w_glu.astype(f32))
    return out.astype(u.dtype), h[:, -1]


def fox_block(q, cq, qpos, k, v, ck, kpos):
    s = jnp.einsum('bqhd,bkhd->bhqk', q, k).astype(jnp.float32) * (HEAD_DIM ** -0.5)
    s = s + jnp.transpose(cq, (0, 2, 1))[:, :, :, None] - jnp.transpose(ck, (0, 2, 1))[:, :, None, :]
    mask = kpos[None, :] <= qpos[:, None]
    s = jnp.where(mask, s, -jnp.inf)
    p = jax.nn.softmax(s, axis=-1)
    return jnp.einsum('bhqk,bkhd->bqhd', p.astype(v.dtype), v)


def fox_prompt(q, k, v, logf):
    bsz, t = q.shape[:2]
    c = jnp.cumsum(logf, axis=1)
    nb = t // Q_BLOCK
    kpos = jnp.arange(t)
    qb = q.reshape(bsz, nb, Q_BLOCK, N_ATT_HEADS, HEAD_DIM).transpose(1, 0, 2, 3, 4)
    cb = c.reshape(bsz, nb, Q_BLOCK, N_ATT_HEADS).transpose(1, 0, 2, 3)
    starts = jnp.arange(nb) * Q_BLOCK

    def one(args):
        q_i, c_i, s0 = args
        return fox_block(q_i, c_i, s0 + jnp.arange(Q_BLOCK), k, v, c, kpos)

    o = lax.map(one, (qb, cb, starts))
    return o.transpose(1, 0, 2, 3, 4).reshape(bsz, t, N_ATT_HEADS, HEAD_DIM)


def fox_sample(q, k_new, v_new, logf_new, k_past, v_past, logf_past):
    past = k_past.shape[1]
    t = q.shape[1]
    k = jnp.concatenate([k_past.astype(k_new.dtype), k_new], axis=1)
    v = jnp.concatenate([v_past.astype(v_new.dtype), v_new], axis=1)
    c = jnp.cumsum(jnp.concatenate([logf_past.astype(jnp.float32), logf_new], axis=1), axis=1)
    kpos = jnp.arange(past + t)
    qpos = past + jnp.arange(t)
    return fox_block(q, c[:, past:], qpos, k, v, c, kpos)


def trunk_layer(x, p, lw, conv_buf, ssm_re, ssm_im, kv_past):
    bsz, t, _ = x.shape
    h = rmsnorm(x, lw['norm_mix_pre'])
    z = h @ lw['w_in']
    cuts = (CONV_DIM, 2 * CONV_DIM, 3 * CONV_DIM, 3 * CONV_DIM + SSM_DIM,
            3 * CONV_DIM + SSM_DIM + ATT_DIM, 3 * CONV_DIM + SSM_DIM + 2 * ATT_DIM,
            3 * CONV_DIM + SSM_DIM + 3 * ATT_DIM)
    cb, cc, cv, su, q, k, v, fl = jnp.split(z, cuts, axis=-1)
    conv_out, conv_new = short_conv_mixer(cb, cc, cv, conv_buf, lw['conv_w'])
    h0 = lax.complex(ssm_re.astype(jnp.float32), ssm_im.astype(jnp.float32))
    ssm_out, h_last = ssm_mixer(su, h0, lw['ssm_a_re'], lw['ssm_a_im'], lw['ssm_log_dt'],
                                lw['ssm_b_re'], lw['ssm_b_im'], lw['ssm_c_re'], lw['ssm_c_im'],
                                lw['ssm_d'], lw['w_ssm_glu'])
    q = q.reshape(bsz, t, N_ATT_HEADS, HEAD_DIM)
    k = k.reshape(bsz, t, N_ATT_HEADS, HEAD_DIM)
    v = v.reshape(bsz, t, N_ATT_HEADS, HEAD_DIM)
    logf = jax.nn.log_sigmoid((fl + lw['b_forget']).astype(jnp.float32))
    if kv_past is None:
        att = fox_prompt(q, k, v, logf)
    else:
        att = fox_sample(q, k, v, logf, kv_past[0], kv_past[1], kv_past[2])
    mix = jnp.concatenate([conv_out, ssm_out, att.reshape(bsz, t, ATT_DIM)], axis=-1) @ lw['w_out']
    x = x + rmsnorm(mix, lw['norm_mix_post'])
    h2 = rmsnorm(x, lw['norm_ffn_pre'])
    ffn = (jax.nn.silu(h2 @ lw['w_ffn_gate']) * (h2 @ lw['w_ffn_up'])) @ lw['w_ffn_down']
    x = x + rmsnorm(ffn, lw['norm_ffn_post'])
    x = x + jax.nn.sigmoid(x @ lw['w_ple_gate']) * (p @ lw['w_ple_proj'])
    return x, (k, v, logf, conv_new, jnp.real(h_last), jnp.imag(h_last))


def setup_inputs(seed: int = 0) -> dict:
    key = jax.random.key(seed)
    ks = iter(jax.random.split(key, 48))
    f32 = jnp.float32
    nrm = lambda shape, scale=1.0: jax.random.normal(next(ks), shape, f32) * scale
    n_pages = PAST_LEN // PAGE_SIZE
    n_used = DEC_BATCH * n_pages
    n_pool = n_used + max(1, n_used // 4)
    page_table = jax.random.permutation(next(ks), n_pool)[:n_used].reshape(DEC_BATCH, n_pages).astype(jnp.int32)
    log_dt = jax.random.uniform(next(ks), (DEPTH, SSM_GROUPS), f32, math.log(0.001), math.log(0.1))
    a_im = jnp.pi * jnp.arange(SSM_STATE, dtype=f32)[None, None, :] + nrm((DEPTH, SSM_GROUPS, SSM_STATE), 0.01)
    return {
        "x_prompt": nrm((BATCH, SEQ, D_MODEL)),
        "x_sample": nrm((DEC_BATCH, DEC_SEQ, D_MODEL)),
        "cache_k": nrm((DEPTH, n_pool, PAGE_SIZE, N_ATT_HEADS, HEAD_DIM)),
        "cache_v": nrm((DEPTH, n_pool, PAGE_SIZE, N_ATT_HEADS, HEAD_DIM)),
        "cache_logf": jax.nn.log_sigmoid(FORGET_BIAS + nrm((DEPTH, n_pool, PAGE_SIZE, N_ATT_HEADS))),
        "state_conv": nrm((DEPTH, DEC_BATCH, CONV_K - 1, CONV_DIM)),
        "state_ssm_re": nrm((DEPTH, DEC_BATCH, SSM_GROUPS, SSM_STATE), 0.3),
        "state_ssm_im": nrm((DEPTH, DEC_BATCH, SSM_GROUPS, SSM_STATE), 0.3),
        "page_table": page_table,
        "p_prompt": nrm((DEPTH, BATCH, SEQ, PLE_DIM)),
        "p_sample": nrm((DEPTH, DEC_BATCH, DEC_SEQ, PLE_DIM)),
        "norm_mix_pre": 1.0 + nrm((DEPTH, D_MODEL), 0.02),
        "norm_mix_post": 1.0 + nrm((DEPTH, D_MODEL), 0.02),
        "norm_ffn_pre": 1.0 + nrm((DEPTH, D_MODEL), 0.02),
        "norm_ffn_post": 1.0 + nrm((DEPTH, D_MODEL), 0.02),
        "w_in": nrm((DEPTH, D_MODEL, IN_COLS), D_MODEL ** -0.5),
        "b_forget": FORGET_BIAS + nrm((DEPTH, N_ATT_HEADS), 0.1),
        "conv_w": nrm((DEPTH, CONV_K, CONV_DIM), CONV_K ** -0.5),
        "ssm_a_re": -0.5 + nrm((DEPTH, SSM_GROUPS, SSM_STATE), 0.01),
        "ssm_a_im": a_im,
        "ssm_log_dt": log_dt,
        "ssm_b_re": nrm((DEPTH, SSM_GROUPS, SSM_STATE, SSM_GROUP), (2 * SSM_GROUP) ** -0.5),
        "ssm_b_im": nrm((DEPTH, SSM_GROUPS, SSM_STATE, SSM_GROUP), (2 * SSM_GROUP) ** -0.5),
        "ssm_c_re": nrm((DEPTH, SSM_GROUPS, SSM_GROUP, SSM_STATE), (2 * SSM_STATE) ** -0.5),
        "ssm_c_im": nrm((DEPTH, SSM_GROUPS, SSM_GROUP, SSM_STATE), (2 * SSM_STATE) ** -0.5),
        "ssm_d": nrm((DEPTH, SSM_DIM)),
        "w_ssm_glu": nrm((DEPTH, SSM_DIM, SSM_DIM), SSM_DIM ** -0.5),
        "w_out": nrm((DEPTH, MIX_WIDTH, D_MODEL), MIX_WIDTH ** -0.5),
        "w_ffn_gate": nrm((DEPTH, D_MODEL, D_FF), D_MODEL ** -0.5),
        "w_ffn_up": nrm((DEPTH, D_MODEL, D_FF), D_MODEL ** -0.5),
        "w_ffn_down": nrm((DEPTH, D_FF, D_MODEL), D_FF ** -0.5),
        "w_ple_gate": nrm((DEPTH, D_MODEL, D_MODEL), D_MODEL ** -0.5),
        "w_ple_proj": nrm((DEPTH, PLE_DIM, D_MODEL), PLE_DIM ** -0.5),
    }


def reference(x_prompt, x_sample, cache_k, cache_v, cache_logf, state_conv, state_ssm_re, state_ssm_im,
              page_table, p_prompt, p_sample, norm_mix_pre, norm_mix_post, norm_ffn_pre, norm_ffn_post,
              w_in, b_forget, conv_w, ssm_a_re, ssm_a_im, ssm_log_dt, ssm_b_re, ssm_b_im, ssm_c_re, ssm_c_im,
              ssm_d, w_ssm_glu, w_out, w_ffn_gate, w_ffn_up, w_ffn_down, w_ple_gate, w_ple_proj):
    bp = x_prompt.shape[0]
    bs = x_sample.shape[0]
    xp = x_prompt
    xs = x_sample
    conv0 = jnp.zeros((bp, CONV_K - 1, CONV_DIM), x_prompt.dtype)
    ssm0 = jnp.zeros((bp, SSM_GROUPS, SSM_STATE), jnp.float32)
    sp = ([], [], [], [], [], [])
    ss = ([], [], [], [], [], [])
    for i in range(DEPTH):
        lw = dict(norm_mix_pre=norm_mix_pre[i], norm_mix_post=norm_mix_post[i],
                  norm_ffn_pre=norm_ffn_pre[i], norm_ffn_post=norm_ffn_post[i],
                  w_in=w_in[i], b_forget=b_forget[i], conv_w=conv_w[i],
                  ssm_a_re=ssm_a_re[i], ssm_a_im=ssm_a_im[i], ssm_log_dt=ssm_log_dt[i],
                  ssm_b_re=ssm_b_re[i], ssm_b_im=ssm_b_im[i], ssm_c_re=ssm_c_re[i], ssm_c_im=ssm_c_im[i],
                  ssm_d=ssm_d[i], w_ssm_glu=w_ssm_glu[i], w_out=w_out[i],
                  w_ffn_gate=w_ffn_gate[i], w_ffn_up=w_ffn_up[i], w_ffn_down=w_ffn_down[i],
                  w_ple_gate=w_ple_gate[i], w_ple_proj=w_ple_proj[i])
        xp, st_p = trunk_layer(xp, p_prompt[i], lw, conv0, ssm0, jnp.zeros_like(ssm0), None)
        k_past = cache_k[i][page_table].reshape(bs, -1, N_ATT_HEADS, HEAD_DIM)
        v_past = cache_v[i][page_table].reshape(bs, -1, N_ATT_HEADS, HEAD_DIM)
        lf_past = cache_logf[i][page_table].reshape(bs, -1, N_ATT_HEADS)
        xs, st_s = trunk_layer(xs, p_sample[i], lw, state_conv[i], state_ssm_re[i], state_ssm_im[i],
                               (k_past, v_past, lf_past))
        for j in range(6):
            sp[j].append(st_p[j])
            ss[j].append(st_s[j])
    k_p, v_p, lf_p, conv_p, re_p, im_p = [jnp.stack(a) for a in sp]
    k_s, v_s, lf_s, conv_s, re_s, im_s = [jnp.stack(a) for a in ss]
    return (xp, xs, k_p, v_p, lf_p, conv_p, re_p, im_p, k_s, v_s, lf_s, conv_s, re_s, im_s)
```

```python
import functools
import math

import jax
import jax.numpy as jnp
from jax import lax
from jax.experimental import pallas as pl
from jax.experimental.pallas import tpu as pltpu

F32 = jnp.float32
BF16 = jnp.bfloat16
HIGHEST = lax.Precision.HIGHEST

D_MODEL = 1024
CONV_DIM = 256
SSM_DIM = 256
ATT_DIM = 512
N_HEADS = 8
HEAD_DIM = 64
SSM_GROUPS = 16
SSM_GROUP = 16
SSM_STATE = 64
D_FF = 2816
PLE_DIM = 256
CONV_K = 3
EPS = 1e-6
LANES = 128
SSM_CHUNK = 16
FF_CHUNK = 256
STATE_W = SSM_GROUPS * SSM_STATE
CHUNK_W = SSM_CHUNK * SSM_DIM
VMEM_LIMIT = 60 * 1024 * 1024


def _params(sem, vmem=VMEM_LIMIT):
    return pltpu.CompilerParams(dimension_semantics=sem, vmem_limit_bytes=vmem)


def _rms(x, g):
    return x * lax.rsqrt(jnp.mean(x * x, axis=-1, keepdims=True) + EPS) * g


def _sigmoid(x):
    return 1.0 / (1.0 + jnp.exp(-x))


def _log_sigmoid(x):
    return jnp.minimum(x, 0.0) - jnp.log1p(jnp.exp(-jnp.abs(x)))


def _gelu_tanh(x):
    return 0.5 * x * (1.0 + jnp.tanh(math.sqrt(2.0 / math.pi) * (x + 0.044715 * (x * x * x))))


def _dot(a, b, **kw):
    return jnp.dot(a, b, preferred_element_type=F32, **kw)


def _dot_nt(a, b, **kw):
    return lax.dot_general(a, b, (((1,), (1,)), ((), ())), preferred_element_type=F32, **kw)


def _const_spec(shape):
    nd = len(shape)
    return pl.BlockSpec(shape, lambda *_: (0,) * nd, pipeline_mode=pl.Buffered(1))


def _ssm_prep_kernel(are_ref, aim_ref, ldt_ref, btr_ref, bti_ref, cre_ref, cim_ref,
                     wgr_ref, wgi_ref, wcr_ref, wci_ref, ks_ref, lam_ref, acat_r, acat_i):
    a_re = are_ref[0]
    a_im = aim_ref[0]
    dt = jnp.exp(ldt_ref[0])
    mag = jnp.exp(a_re * dt)
    lr = mag * jnp.cos(a_im * dt)
    li = mag * jnp.sin(a_im * dt)
    den = a_re * a_re + a_im * a_im
    xr = lr - 1.0
    cfr = (xr * a_re + li * a_im) / den
    cfi = (li * a_re - xr * a_im) / den
    pows = [(jnp.ones_like(lr), jnp.zeros_like(lr))]
    for _ in range(SSM_CHUNK):
        pr, pi = pows[-1]
        pows.append((pr * lr - pi * li, pr * li + pi * lr))
    lam_ref[0, 0] = lr
    lam_ref[0, 1] = li
    lam_ref[0, 2] = pows[SSM_CHUNK][0]
    lam_ref[0, 3] = pows[SSM_CHUNK][1]
    for g in range(SSM_GROUPS):
        btr = btr_ref[0, g]
        bti = bti_ref[0, g]
        c_r = cre_ref[0, g]
        c_i = cim_ref[0, g]
        kr = cfr[g:g + 1]
        ki = cfi[g:g + 1]
        bbr_pad = kr * btr - ki * bti
        bbi_pad = kr * bti + ki * btr
        bbr = bbr_pad[0:SSM_GROUP]
        bbi = bbi_pad[0:SSM_GROUP]
        for t in range(SSM_CHUNK + 1):
            pr = pows[t][0][g:g + 1]
            pi = pows[t][1][g:g + 1]
            rows = slice(t * SSM_GROUP, (t + 1) * SSM_GROUP)
            a_r = c_r * pr - c_i * pi
            a_i = c_r * pi + c_i * pr
            wcr_ref[0, g, rows, :] = a_r
            wci_ref[0, g, rows, :] = -a_i
            if t < SSM_CHUNK:
                acat_r[rows, :] = a_r
                acat_i[rows, :] = -a_i
                s = SSM_CHUNK - 1 - t
                srows = slice(s * SSM_GROUP, (s + 1) * SSM_GROUP)
                wgr_ref[0, g, srows, :] = bbr * pr - bbi * pi
                wgi_ref[0, g, srows, :] = bbr * pi + bbi * pr
        ks_ref[0, g] = (_dot_nt(acat_r[...], bbr_pad, precision=HIGHEST)
                        + _dot_nt(acat_i[...], bbi_pad, precision=HIGHEST))


def _ssm_prep(a_re, a_im, log_dt, b_re, b_im, c_re, c_im):
    depth = a_re.shape[0]
    g, p, c, l = SSM_GROUPS, SSM_STATE, SSM_GROUP, SSM_CHUNK
    pad_rows = ((0, 0), (0, 0), (0, LANES - c), (0, 0))
    btr = jnp.pad(jnp.swapaxes(b_re, -1, -2), pad_rows)
    bti = jnp.pad(jnp.swapaxes(b_im, -1, -2), pad_rows)
    spec_gp = pl.BlockSpec((1, g, p), lambda i: (i, 0, 0))
    spec_gcp = pl.BlockSpec((1, g, c, p), lambda i: (i, 0, 0, 0))
    spec_gbp = pl.BlockSpec((1, g, LANES, p), lambda i: (i, 0, 0, 0))
    out_shapes = (
        jax.ShapeDtypeStruct((depth, g, l * c, p), F32),
        jax.ShapeDtypeStruct((depth, g, l * c, p), F32),
        jax.ShapeDtypeStruct((depth, g, (l + 1) * c, p), F32),
        jax.ShapeDtypeStruct((depth, g, (l + 1) * c, p), F32),
        jax.ShapeDtypeStruct((depth, g, l * c, LANES), F32),
        jax.ShapeDtypeStruct((depth, 4, g, p), F32),
    )
    out_specs = (
        pl.BlockSpec((1, g, l * c, p), lambda i: (i, 0, 0, 0)),
        pl.BlockSpec((1, g, l * c, p), lambda i: (i, 0, 0, 0)),
        pl.BlockSpec((1, g, (l + 1) * c, p), lambda i: (i, 0, 0, 0)),
        pl.BlockSpec((1, g, (l + 1) * c, p), lambda i: (i, 0, 0, 0)),
        pl.BlockSpec((1, g, l * c, LANES), lambda i: (i, 0, 0, 0)),
        pl.BlockSpec((1, 4, g, p), lambda i: (i, 0, 0, 0)),
    )
    return pl.pallas_call(
        _ssm_prep_kernel,
        grid=(depth,),
        in_specs=[spec_gp, spec_gp, pl.BlockSpec((1, g, 1), lambda i: (i, 0, 0)),
                  spec_gbp, spec_gbp, spec_gcp, spec_gcp],
        out_specs=out_specs,
        out_shape=out_shapes,
        scratch_shapes=[pltpu.VMEM((l * c, p), F32), pltpu.VMEM((l * c, p), F32)],
        compiler_params=_params(("arbitrary",)),
        name="ssm_prep",
    )(a_re, a_im, log_dt[..., None], btr, bti, c_re, c_im)


def _ssm_dense_operators(wgr, wgi, wcr, wci, ks, lam):
    depth = wgr.shape[0]
    g, p, c, l = SSM_GROUPS, SSM_STATE, SSM_GROUP, SSM_CHUNK
    eye = jnp.eye(g, dtype=F32)

    def blockdiag_in(w):
        w = w.reshape(depth, g, l, c, p)
        return jnp.einsum('dgscp,gh->dsgchp', w, eye).reshape(depth, l * g * c, g * p)

    def blockdiag_out(w):
        w = w.reshape(depth, g, l, c, p)
        return jnp.einsum('dgjcp,gh->dgpjhc', w, eye).reshape(depth, g * p, l * g * c)

    w_g = jnp.concatenate([blockdiag_in(wgr), blockdiag_in(wgi)], axis=-1)
    w_c = jnp.concatenate([blockdiag_out(wcr[:, :, c:]), blockdiag_out(wci[:, :, c:])], axis=1)
    kk = ks[..., 0:c].reshape(depth, g, l, c, c)
    s_idx = jnp.arange(l)[:, None]
    j_idx = jnp.arange(l)[None, :]
    tau = jnp.clip(j_idx - s_idx, 0, l - 1)
    kt = jnp.where((j_idx >= s_idx)[None, None, :, :, None, None], kk[:, :, tau], 0.0)
    m = jnp.einsum('dgsjxc,gh->dsgcjhx', kt, eye).reshape(depth, l * g * c, l * g * c)
    b1r = blockdiag_in(wgr)[:, (l - 1) * g * c:]
    b1i = blockdiag_in(wgi)[:, (l - 1) * g * c:]
    def blockdiag_step(w):
        return jnp.einsum('dgcp,gh->dgphc', w, eye).reshape(depth, g * p, g * c)

    c1 = jnp.concatenate([blockdiag_step(wcr[:, :, :c]), blockdiag_step(wci[:, :, :c])], axis=1)
    lam1 = lam[:, 0:2].reshape(depth, 2, g * p)
    lam_l = lam[:, 2:4].reshape(depth, 2, g * p)
    return w_g.astype(BF16), w_c.astype(BF16), m.astype(BF16), b1r, b1i, c1, lam1, lam_l


def _inproj_prompt_kernel(x_ref, g_ref, wg_ref, wqkv_ref, wfl_ref, bfl_ref, cw_ref,
                          convo_ref, su_ref, sub_ref, q_ref, k_ref, v_ref, kb_ref, vb_ref,
                          lf_ref, ct_ref, convnew_ref, ubuf, ccar, *, tiles_per_seq, tm):
    i = pl.program_id(0)

    @pl.when(i % tiles_per_seq == 0)
    def _():
        ubuf[0:8, :] = jnp.zeros((8, CONV_DIM), F32)
        ccar[...] = jnp.zeros_like(ccar)

    h = _rms(x_ref[...], g_ref[...]).astype(BF16)
    z = _dot(h, wg_ref[...])
    cb = z[:, 0:CONV_DIM]
    cc = z[:, CONV_DIM:2 * CONV_DIM]
    cv = z[:, 2 * CONV_DIM:3 * CONV_DIM]
    su = z[:, 3 * CONV_DIM:]
    su_ref[...] = su
    sub_ref[...] = su.astype(BF16)
    u = cc * cv
    ubuf[8:8 + tm, :] = u
    u1 = ubuf[7:7 + tm, :]
    u2 = ubuf[6:6 + tm, :]
    cw = cw_ref[...]
    y = cw[0:1] * u2 + cw[1:2] * u1 + cw[2:3] * u
    convo_ref[...] = (cb * y).astype(BF16)
    convnew_ref[0] = u[tm - 2:tm, :]
    ubuf[0:8, :] = u[tm - 8:tm, :]

    zz = _dot(h, wqkv_ref[...])
    q_ref[...] = (zz[:, 0:ATT_DIM] * (HEAD_DIM ** -0.5)).astype(BF16)
    k = zz[:, ATT_DIM:2 * ATT_DIM]
    v = zz[:, 2 * ATT_DIM:]
    k_ref[...] = k
    v_ref[...] = v
    kb_ref[...] = k.astype(BF16)
    vb_ref[...] = v.astype(BF16)

    fl = _dot(h, wfl_ref[...]) + bfl_ref[...]
    lane = lax.broadcasted_iota(jnp.int32, fl.shape, 1)
    lf = jnp.where(lane < N_HEADS, _log_sigmoid(fl), 0.0)
    lf_ref[...] = lf[:, 0:N_HEADS]
    row = lax.broadcasted_iota(jnp.int32, (tm, tm), 0)
    col = lax.broadcasted_iota(jnp.int32, (tm, tm), 1)
    tri = (row >= col).astype(F32)
    c = _dot(tri, lf, precision=HIGHEST) + ccar[...]
    ccar[...] = c[tm - 1:tm, :]
    ct_ref[...] = c.T[0:N_HEADS, :]


def _inproj_prompt(x, g_pre, wg, wqkv, wfl, bfl, conv_w, *, seq_len, tm):
    n = x.shape[0]
    nt = n // tm
    tiles_per_seq = seq_len // tm
    nseq = n // seq_len
    row = lambda w: pl.BlockSpec((tm, w), lambda i: (i, 0))
    out_shapes = (
        jax.ShapeDtypeStruct((n, CONV_DIM), BF16),
        jax.ShapeDtypeStruct((n, SSM_DIM), F32),
        jax.ShapeDtypeStruct((n, SSM_DIM), BF16),
        jax.ShapeDtypeStruct((n, ATT_DIM), BF16),
        jax.ShapeDtypeStruct((n, ATT_DIM), F32),
        jax.ShapeDtypeStruct((n, ATT_DIM), F32),
        jax.ShapeDtypeStruct((n, ATT_DIM), BF16),
        jax.ShapeDtypeStruct((n, ATT_DIM), BF16),
        jax.ShapeDtypeStruct((n, N_HEADS), F32),
        jax.ShapeDtypeStruct((N_HEADS, n), F32),
        jax.ShapeDtypeStruct((nseq, CONV_K - 1, CONV_DIM), F32),
    )
    out_specs = (
        row(CONV_DIM), row(SSM_DIM), row(SSM_DIM), row(ATT_DIM), row(ATT_DIM), row(ATT_DIM),
        row(ATT_DIM), row(ATT_DIM), row(N_HEADS),
        pl.BlockSpec((N_HEADS, tm), lambda i: (0, i)),
        pl.BlockSpec((1, CONV_K - 1, CONV_DIM), lambda i: (i // tiles_per_seq, 0, 0)),
    )
    return pl.pallas_call(
        functools.partial(_inproj_prompt_kernel, tiles_per_seq=tiles_per_seq, tm=tm),
        grid=(nt,),
        in_specs=[row(D_MODEL), _const_spec((1, D_MODEL)), _const_spec(wg.shape), _const_spec(wqkv.shape),
                  _const_spec(wfl.shape), _const_spec((1, LANES)), _const_spec((CONV_K, CONV_DIM))],
        out_specs=out_specs,
        out_shape=out_shapes,
        scratch_shapes=[pltpu.VMEM((tm + 8, CONV_DIM), F32), pltpu.VMEM((1, LANES), F32)],
        compiler_params=_params(("arbitrary",)),
        name="inproj_prompt",
    )(x, g_pre, wg, wqkv, wfl, bfl, conv_w)


def _inproj_sample_kernel(x_ref, g_ref, wg_ref, wqkv_ref, wfl_ref, bfl_ref, cw_ref, b0_ref, b1_ref,
                          convo_ref, su_ref, u_ref, q_ref, k_ref, v_ref, lf_ref):
    h = _rms(x_ref[...], g_ref[...]).astype(BF16)
    z = _dot(h, wg_ref[...])
    cb = z[:, 0:CONV_DIM]
    u = z[:, CONV_DIM:2 * CONV_DIM] * z[:, 2 * CONV_DIM:3 * CONV_DIM]
    su_ref[...] = z[:, 3 * CONV_DIM:]
    u_ref[...] = u
    cw = cw_ref[...]
    y = cw[0:1] * b0_ref[...] + cw[1:2] * b1_ref[...] + cw[2:3] * u
    convo_ref[...] = (cb * y).astype(BF16)
    zz = _dot(h, wqkv_ref[...])
    q_ref[...] = (zz[:, 0:ATT_DIM] * (HEAD_DIM ** -0.5)).astype(BF16)
    k_ref[...] = zz[:, ATT_DIM:2 * ATT_DIM]
    v_ref[...] = zz[:, 2 * ATT_DIM:]
    fl = _dot(h, wfl_ref[...]) + bfl_ref[...]
    lane = lax.broadcasted_iota(jnp.int32, fl.shape, 1)
    lf_ref[...] = jnp.where(lane < N_HEADS, _log_sigmoid(fl), 0.0)


def _inproj_sample(x, g_pre, wg, wqkv, wfl, bfl, conv_w, buf0, buf1):
    n = x.shape[0]
    full = lambda s: pl.BlockSpec(s, lambda i: (0,) * len(s))
    out_shapes = (
        jax.ShapeDtypeStruct((n, CONV_DIM), BF16),
        jax.ShapeDtypeStruct((n, SSM_DIM), F32),
        jax.ShapeDtypeStruct((n, CONV_DIM), F32),
        jax.ShapeDtypeStruct((n, ATT_DIM), BF16),
        jax.ShapeDtypeStruct((n, ATT_DIM), F32),
        jax.ShapeDtypeStruct((n, ATT_DIM), F32),
        jax.ShapeDtypeStruct((n, LANES), F32),
    )
    args = (x, g_pre, wg, wqkv, wfl, bfl, conv_w, buf0, buf1)
    return pl.pallas_call(
        _inproj_sample_kernel,
        grid=(1,),
        in_specs=[full(a.shape) for a in args],
        out_specs=tuple(full(s.shape) for s in out_shapes),
        out_shape=out_shapes,
        compiler_params=_params(("arbitrary",)),
        name="inproj_sample",
    )(*args)


def _ssm_chunk_state_kernel(u_ref, w_ref, o_ref):
    o_ref[...] = _dot(u_ref[...], w_ref[...])


def _ssm_chunk_state(uflat, w_g, *, tm, tn):
    m, k = uflat.shape
    n = w_g.shape[1]
    return pl.pallas_call(
        _ssm_chunk_state_kernel,
        grid=(n // tn, m // tm),
        in_specs=[pl.BlockSpec((tm, k), lambda j, i: (i, 0)), pl.BlockSpec((k, tn), lambda j, i: (0, j))],
        out_specs=pl.BlockSpec((tm, tn), lambda j, i: (i, j)),
        out_shape=jax.ShapeDtypeStruct((m, n), F32),
        compiler_params=_params(("arbitrary", "arbitrary")),
        name="ssm_chunk_state",
    )(uflat, w_g)


def _ssm_scan_kernel(g_ref, lam_ref, hin_ref, hlast_ref, *, n_chunks):
    lr = lam_ref[0:1, :]
    li = lam_ref[1:2, :]

    def body(c, carry):
        h_re, h_im = carry
        hin_ref[pl.ds(c, 1), 0:STATE_W] = h_re
        hin_ref[pl.ds(c, 1), STATE_W:] = h_im
        g_re = g_ref[pl.ds(c, 1), 0:STATE_W]
        g_im = g_ref[pl.ds(c, 1), STATE_W:]
        return lr * h_re - li * h_im + g_re, lr * h_im + li * h_re + g_im

    zero = jnp.zeros((1, STATE_W), F32)
    h_re, h_im = lax.fori_loop(0, n_chunks, body, (zero, zero))
    hlast_ref[0, :, 0:STATE_W] = h_re
    hlast_ref[0, :, STATE_W:] = h_im


def _ssm_scan(gstate, lam_l, *, chunks_per_seq):
    m, w = gstate.shape
    nseq = m // chunks_per_seq
    return pl.pallas_call(
        functools.partial(_ssm_scan_kernel, n_chunks=chunks_per_seq),
        grid=(nseq,),
        in_specs=[pl.BlockSpec((chunks_per_seq, w), lambda b: (b, 0)), pl.BlockSpec((2, STATE_W), lambda b: (0, 0))],
        out_specs=(pl.BlockSpec((chunks_per_seq, w), lambda b: (b, 0)),
                   pl.BlockSpec((1, 1, w), lambda b: (b, 0, 0))),
        out_shape=(jax.ShapeDtypeStruct((m, w), F32), jax.ShapeDtypeStruct((nseq, 1, w), F32)),
        compiler_params=_params(("arbitrary",)),
        name="ssm_scan",
    )(gstate, lam_l)


def _ssm_output_kernel(ub_ref, hin_ref, uf_ref, m_ref, wc_ref, d_ref, y_ref):
    y = _dot(ub_ref[...], m_ref[...]) + _dot(hin_ref[...].astype(BF16), wc_ref[...])
    y_ref[...] = y + d_ref[...] * uf_ref[...]


def _ssm_output(uflat_b, hin, uflat_f, m_op, w_c, d_tiled, *, tm, tn):
    m, k = uflat_b.shape
    ks = hin.shape[1]
    return pl.pallas_call(
        _ssm_output_kernel,
        grid=(k // tn, m // tm),
        in_specs=[pl.BlockSpec((tm, k), lambda j, i: (i, 0)),
                  pl.BlockSpec((tm, ks), lambda j, i: (i, 0)),
                  pl.BlockSpec((tm, tn), lambda j, i: (i, j)),
                  pl.BlockSpec((k, tn), lambda j, i: (0, j)),
                  pl.BlockSpec((ks, tn), lambda j, i: (0, j)),
                  pl.BlockSpec((1, tn), lambda j, i: (0, j))],
        out_specs=pl.BlockSpec((tm, tn), lambda j, i: (i, j)),
        out_shape=jax.ShapeDtypeStruct((m, k), F32),
        compiler_params=_params(("arbitrary", "arbitrary")),
        name="ssm_output",
    )(uflat_b, hin, uflat_f, m_op, w_c, d_tiled)


def _ssm_step_kernel(u_ref, hre_ref, him_ref, br_ref, bi_ref, lam_ref, c_ref, d_ref,
                     y_ref, ore_ref, oim_ref):
    u = u_ref[...]
    lr = lam_ref[0:1, :]
    li = lam_ref[1:2, :]
    h_re = hre_ref[...]
    h_im = him_ref[...]
    n_re = lr * h_re - li * h_im + _dot(u, br_ref[...], precision=HIGHEST)
    n_im = lr * h_im + li * h_re + _dot(u, bi_ref[...], precision=HIGHEST)
    ore_ref[...] = n_re
    oim_ref[...] = n_im
    y = _dot(n_re, c_ref[0:STATE_W, :], precision=HIGHEST) + _dot(n_im, c_ref[STATE_W:, :], precision=HIGHEST)
    y_ref[...] = y + d_ref[...] * u


def _ssm_step(u, h_re, h_im, b1r, b1i, lam1, c1, d):
    n = u.shape[0]
    full = lambda s: pl.BlockSpec(s, lambda i: (0,) * len(s))
    args = (u, h_re, h_im, b1r, b1i, lam1, c1, d)
    out_shapes = (jax.ShapeDtypeStruct((n, SSM_DIM), F32), jax.ShapeDtypeStruct((n, STATE_W), F32),
                  jax.ShapeDtypeStruct((n, STATE_W), F32))
    return pl.pallas_call(
        _ssm_step_kernel,
        grid=(1,),
        in_specs=[full(a.shape) for a in args],
        out_specs=tuple(full(s.shape) for s in out_shapes),
        out_shape=out_shapes,
        compiler_params=_params(("arbitrary",)),
        name="ssm_step",
    )(*args)


def _fox_prompt_kernel(q_ref, k_ref, v_ref, cq_ref, ck_ref, o_ref, m_scr, l_scr, acc_scr, *, tq, tk):
    qi = pl.program_id(2)
    ki = pl.program_id(3)

    @pl.when(ki == 0)
    def _():
        m_scr[...] = jnp.full(m_scr.shape, -jnp.inf, F32)
        l_scr[...] = jnp.zeros_like(l_scr)
        acc_scr[...] = jnp.zeros_like(acc_scr)

    @pl.when(ki <= qi)
    def _():
        q2 = q_ref[...]
        k2 = k_ref[...]
        v2 = v_ref[...]
        lane = lax.broadcasted_iota(jnp.int32, q2.shape, 1)
        keep = [(lane < HEAD_DIM).astype(F32).astype(BF16), (lane >= HEAD_DIM).astype(F32).astype(BF16)]
        row = qi * tq + lax.broadcasted_iota(jnp.int32, (tq, tk), 0)
        col = ki * tk + lax.broadcasted_iota(jnp.int32, (tq, tk), 1)
        causal = col <= row
        alphas = []
        pvs = []
        for hh in range(2):
            s = _dot_nt(q2 * keep[hh], k2) - ck_ref[0, hh:hh + 1, :]
            s = jnp.where(causal, s, -jnp.inf)
            cq = cq_ref[hh]
            m_prev = m_scr[hh]
            m_new = jnp.maximum(m_prev, jnp.max(s, axis=-1, keepdims=True) + cq)
            p = jnp.exp(s - (m_new - cq))
            alpha = jnp.exp(m_prev - m_new)
            l_scr[hh] = alpha * l_scr[hh] + jnp.sum(p, axis=-1, keepdims=True)
            m_scr[hh] = m_new
            alphas.append(alpha)
            pvs.append(_dot(p.astype(BF16), v2))
        lane_o = lax.broadcasted_iota(jnp.int32, (tq, LANES), 1) < HEAD_DIM
        acc_scr[...] = jnp.where(lane_o, alphas[0], alphas[1]) * acc_scr[...] + jnp.where(lane_o, pvs[0], pvs[1])

    @pl.when(ki == qi)
    def _():
        lane_o = lax.broadcasted_iota(jnp.int32, (tq, LANES), 1) < HEAD_DIM
        o_ref[...] = (acc_scr[...] / jnp.where(lane_o, l_scr[0], l_scr[1])).astype(o_ref.dtype)


def _fox_prompt(q, kb, vb, ct, *, seq_len, tq, tk):
    n = q.shape[0]
    nseq = n // seq_len
    nq = seq_len // tq
    nk = seq_len // tk
    hp = N_HEADS // 2
    cq = ct.reshape(N_HEADS, n, 1)
    ck = ct.reshape(hp, 2, n)
    kv_idx = lambda b, h, i, j: (b * nk + jnp.minimum(j, i), h)
    return pl.pallas_call(
        functools.partial(_fox_prompt_kernel, tq=tq, tk=tk),
        grid=(nseq, hp, nq, nk),
        in_specs=[pl.BlockSpec((tq, LANES), lambda b, h, i, j: (b * nq + i, h)),
                  pl.BlockSpec((tk, LANES), kv_idx),
                  pl.BlockSpec((tk, LANES), kv_idx),
                  pl.BlockSpec((2, tq, 1), lambda b, h, i, j: (h, b * nq + i, 0)),
                  pl.BlockSpec((1, 2, tk), lambda b, h, i, j: (h, 0, b * nk + jnp.minimum(j, i)))],
        out_specs=pl.BlockSpec((tq, LANES), lambda b, h, i, j: (b * nq + i, h)),
        out_shape=jax.ShapeDtypeStruct((n, ATT_DIM), BF16),
        scratch_shapes=[pltpu.VMEM((2, tq, 1), F32), pltpu.VMEM((2, tq, 1), F32), pltpu.VMEM((tq, LANES), F32)],
        compiler_params=_params(("arbitrary", "arbitrary", "arbitrary", "arbitrary")),
        name="fox_prompt",
    )(q, kb, vb, cq, ck)


PAGES_PER_STEP = 8


def _fox_sample_kernel(pt_ref, q_ref, kn_ref, vn_ref, lfn_ref, *refs, page_size):
    np_ = PAGES_PER_STEP
    k_refs = refs[0:np_]
    v_refs = refs[np_:2 * np_]
    lf_refs = refs[2 * np_:3 * np_]
    o_ref = refs[3 * np_]
    m_scr, l_scr, acc_scr, car_scr, lf_scr = refs[3 * np_ + 1:]
    j = pl.program_id(1)
    nj = pl.num_programs(1)

    eh_row = lax.broadcasted_iota(jnp.int32, (LANES, ATT_DIM), 0)
    eh_col = lax.broadcasted_iota(jnp.int32, (LANES, ATT_DIM), 1) // HEAD_DIM
    head_of_lane = eh_row == eh_col
    expand = head_of_lane.astype(F32)

    def widen(x):
        return _dot(jnp.broadcast_to(x, (8, LANES)), expand, precision=HIGHEST)[0:1]

    q = q_ref[0]

    @pl.when(j == 0)
    def _():
        prod = q.astype(F32) * kn_ref[0].astype(BF16).astype(F32)
        s_new = _dot_nt(jnp.broadcast_to(prod, (8, ATT_DIM)), expand, precision=HIGHEST)[0:1]
        m_scr[...] = s_new
        l_scr[...] = jnp.ones_like(l_scr)
        acc_scr[...] = vn_ref[0]
        car_scr[...] = lfn_ref[0]

    qbd = (expand * q.astype(F32)).astype(BF16)
    r_i = lax.broadcasted_iota(jnp.int32, (page_size, page_size), 0)
    c_i = lax.broadcasted_iota(jnp.int32, (page_size, page_size), 1)
    later = (c_i > r_i).astype(F32)
    lf_scr[...] = jnp.zeros_like(lf_scr)
    for r in range(np_):
        lf_scr[r * page_size:(r + 1) * page_size, 0:N_HEADS] = lf_refs[r][0]
    carry = car_scr[...]
    s_list = []
    for r in range(np_):
        kb = k_refs[r][0].astype(BF16)
        lf = lf_scr[r * page_size:(r + 1) * page_size, :]
        bias = _dot(later, lf, precision=HIGHEST) + carry
        carry = carry + jnp.sum(lf, axis=0, keepdims=True)
        s_list.append(_dot_nt(kb, qbd) + bias)
    car_scr[...] = carry
    s_all = jnp.concatenate(s_list, axis=0)
    m_prev = m_scr[...]
    m_new = jnp.maximum(m_prev, jnp.max(s_all, axis=0, keepdims=True))
    alpha = jnp.exp(m_prev - m_new)
    p_all = jnp.exp(s_all - m_new)
    l_scr[...] = alpha * l_scr[...] + jnp.sum(p_all, axis=0, keepdims=True)
    m_scr[...] = m_new
    pe = _dot(p_all.astype(BF16), expand.astype(BF16))
    acc = widen(alpha) * acc_scr[...]
    for r in range(np_):
        acc = acc + jnp.sum(pe[r * page_size:(r + 1) * page_size] * v_refs[r][0], axis=0, keepdims=True)
    acc_scr[...] = acc

    @pl.when(j == nj - 1)
    def _():
        o_ref[0] = (acc / widen(l_scr[...])).astype(o_ref.dtype)


def _fox_sample(page_table, q, k_new, v_new, lf_new, cache_k, cache_v, cache_lf, *, layer, n_pool):
    nseq, n_pages = page_table.shape
    page_size = cache_k.shape[1]
    np_ = PAGES_PER_STEP
    nj = n_pages // np_
    base = layer * n_pool

    def page_idx(r):
        def f(b, j, pt):
            return (base + pt[b * n_pages + (n_pages - 1 - (j * np_ + r))], 0, 0)
        return f

    tok = lambda w: pl.BlockSpec((1, 1, w), lambda b, j, pt: (b, 0, 0))
    in_specs = [tok(ATT_DIM), tok(ATT_DIM), tok(ATT_DIM), tok(LANES)]
    in_specs += [pl.BlockSpec((1, page_size, ATT_DIM), page_idx(r)) for r in range(np_)]
    in_specs += [pl.BlockSpec((1, page_size, ATT_DIM), page_idx(r)) for r in range(np_)]
    in_specs += [pl.BlockSpec((1, page_size, N_HEADS), page_idx(r)) for r in range(np_)]
    grid_spec = pltpu.PrefetchScalarGridSpec(
        num_scalar_prefetch=1,
        grid=(nseq, nj),
        in_specs=in_specs,
        out_specs=pl.BlockSpec((1, 1, ATT_DIM), lambda b, j, pt: (b, 0, 0)),
        scratch_shapes=[pltpu.VMEM((1, LANES), F32), pltpu.VMEM((1, LANES), F32),
                        pltpu.VMEM((1, ATT_DIM), F32), pltpu.VMEM((1, LANES), F32),
                        pltpu.VMEM((np_ * page_size, LANES), F32)],
    )
    args = [q.reshape(nseq, 1, ATT_DIM), k_new.reshape(nseq, 1, ATT_DIM), v_new.reshape(nseq, 1, ATT_DIM),
            lf_new.reshape(nseq, 1, LANES)]
    args += [cache_k] * np_ + [cache_v] * np_ + [cache_lf] * np_
    out = pl.pallas_call(
        functools.partial(_fox_sample_kernel, page_size=page_size),
        grid_spec=grid_spec,
        out_shape=jax.ShapeDtypeStruct((nseq, 1, ATT_DIM), BF16),
        compiler_params=_params(("arbitrary", "arbitrary")),
        name="fox_sample",
    )(page_table.reshape(-1), *args)
    return out.reshape(nseq, ATT_DIM)


def _post_kernel(x_ref, convo_ref, ssmy_ref, att_ref, p_ref, wglu_ref, wout_ref, gpost_ref, gfpre_ref,
                 wgate_ref, wup_ref, wdown_ref, gfpost_ref, wpg_ref, wpp_ref, o_ref):
    y = _gelu_tanh(ssmy_ref[...])
    ssm_out = y * _sigmoid(_dot(y.astype(BF16), wglu_ref[...]))
    mix = (_dot(convo_ref[...], wout_ref[0:CONV_DIM, :])
           + _dot(ssm_out.astype(BF16), wout_ref[CONV_DIM:CONV_DIM + SSM_DIM, :])
           + _dot(att_ref[...], wout_ref[CONV_DIM + SSM_DIM:, :]))
    x1 = x_ref[...] + _rms(mix, gpost_ref[...])
    h2 = _rms(x1, gfpre_ref[...]).astype(BF16)
    ffn = jnp.zeros(x1.shape, F32)
    for c in range(D_FF // FF_CHUNK):
        cols = slice(c * FF_CHUNK, (c + 1) * FF_CHUNK)
        gate = _dot(h2, wgate_ref[:, cols])
        up = _dot(h2, wup_ref[:, cols])
        act = (gate * _sigmoid(gate) * up).astype(BF16)
        ffn = ffn + _dot(act, wdown_ref[cols, :])
    x2 = x1 + _rms(ffn, gfpost_ref[...])
    pgate = _sigmoid(_dot(x2.astype(BF16), wpg_ref[...]))
    o_ref[...] = x2 + pgate * _dot(p_ref[...].astype(BF16), wpp_ref[...])


def _post(x, convo, ssmy, att, pemb, wglu, wout, gpost, gfpre, wgate, wup, wdown, gfpost, wpg, wpp, *, tm):
    n = x.shape[0]
    row = lambda w: pl.BlockSpec((tm, w), lambda i: (i, 0))
    weights = (wglu, wout, gpost, gfpre, wgate, wup, wdown, gfpost, wpg, wpp)
    return pl.pallas_call(
        _post_kernel,
        grid=(n // tm,),
        in_specs=[row(D_MODEL), row(CONV_DIM), row(SSM_DIM), row(ATT_DIM), row(PLE_DIM)]
                 + [_const_spec(w.shape) for w in weights],
        out_specs=row(D_MODEL),
        out_shape=jax.ShapeDtypeStruct((n, D_MODEL), F32),
        compiler_params=_params(("arbitrary",)),
        name="post_mixer",
    )(x, convo, ssmy, att, pemb, *weights)


def kernel(x_prompt, x_sample, cache_k, cache_v, cache_logf, state_conv, state_ssm_re, state_ssm_im, page_table, p_prompt, p_sample, norm_mix_pre, norm_mix_post, norm_ffn_pre, norm_ffn_post, w_in, b_forget, conv_w, ssm_a_re, ssm_a_im, ssm_log_dt, ssm_b_re, ssm_b_im, ssm_c_re, ssm_c_im, ssm_d, w_ssm_glu, w_out, w_ffn_gate, w_ffn_up, w_ffn_down, w_ple_gate, w_ple_proj):
    depth = w_in.shape[0]
    bp, seq_len, _ = x_prompt.shape
    bs = x_sample.shape[0]
    n_pool = cache_k.shape[1]
    page_size = cache_k.shape[2]
    n_p = bp * seq_len
    tm = min(512, seq_len)
    tq = min(512, seq_len)
    chunks_per_seq = seq_len // SSM_CHUNK
    n_chunks = n_p // SSM_CHUNK
    tmc = min(512, n_chunks)

    gate_cols = 3 * CONV_DIM + SSM_DIM
    wg = w_in[:, :, 0:gate_cols].astype(BF16)
    wqkv = w_in[:, :, gate_cols:gate_cols + 3 * ATT_DIM].astype(BF16)
    wfl = jnp.pad(w_in[:, :, gate_cols + 3 * ATT_DIM:], ((0, 0), (0, 0), (0, LANES - N_HEADS))).astype(BF16)
    bfl = jnp.pad(b_forget, ((0, 0), (0, LANES - N_HEADS)))[:, None, :]
    wglu = w_ssm_glu.astype(BF16)
    wout = w_out.astype(BF16)
    wgate = w_ffn_gate.astype(BF16)
    wup = w_ffn_up.astype(BF16)
    wdown = w_ffn_down.astype(BF16)
    wpg = w_ple_gate.astype(BF16)
    wpp = w_ple_proj.astype(BF16)
    g_pre = norm_mix_pre[:, None, :]
    g_post = norm_mix_post[:, None, :]
    g_fpre = norm_ffn_pre[:, None, :]
    g_fpost = norm_ffn_post[:, None, :]
    d_row = ssm_d[:, None, :]
    d_tiled = jnp.tile(ssm_d, (1, SSM_CHUNK))[:, None, :]

    prep = _ssm_prep(ssm_a_re, ssm_a_im, ssm_log_dt, ssm_b_re, ssm_b_im, ssm_c_re, ssm_c_im)
    w_g, w_c, m_op, b1r, b1i, c1, lam1, lam_l = _ssm_dense_operators(*prep)

    ck_flat = cache_k.reshape(depth * n_pool, page_size, ATT_DIM)
    cv_flat = cache_v.reshape(depth * n_pool, page_size, ATT_DIM)
    clf_flat = cache_logf.reshape(depth * n_pool, page_size, N_HEADS)

    xp = x_prompt.reshape(n_p, D_MODEL)
    xs = x_sample.reshape(bs, D_MODEL)
    outs_p = [[] for _ in range(6)]
    outs_s = [[] for _ in range(6)]
    for i in range(depth):
        (convo, su, sub, q, k, v, kb, vb, lf, ct, conv_new) = _inproj_prompt(
            xp, g_pre[i], wg[i], wqkv[i], wfl[i], bfl[i], conv_w[i], seq_len=seq_len, tm=tm)
        uflat_b = sub.reshape(n_chunks, CHUNK_W)
        uflat_f = su.reshape(n_chunks, CHUNK_W)
        gstate = _ssm_chunk_state(uflat_b, w_g[i], tm=tmc, tn=1024)
        hin, hlast = _ssm_scan(gstate, lam_l[i], chunks_per_seq=chunks_per_seq)
        ssmy = _ssm_output(uflat_b, hin, uflat_f, m_op[i], w_c[i], d_tiled[i], tm=tmc, tn=512)
        att = _fox_prompt(q, kb, vb, ct, seq_len=seq_len, tq=tq, tk=tq)
        xp = _post(xp, convo, ssmy.reshape(n_p, SSM_DIM), att, p_prompt[i].reshape(n_p, PLE_DIM),
                   wglu[i], wout[i], g_post[i], g_fpre[i], wgate[i], wup[i], wdown[i], g_fpost[i],
                   wpg[i], wpp[i], tm=tm)
        outs_p[0].append(k.reshape(bp, seq_len, N_HEADS, HEAD_DIM))
        outs_p[1].append(v.reshape(bp, seq_len, N_HEADS, HEAD_DIM))
        outs_p[2].append(lf.reshape(bp, seq_len, N_HEADS))
        outs_p[3].append(conv_new)
        hl = hlast.reshape(bp, 2, SSM_GROUPS, SSM_STATE)
        outs_p[4].append(hl[:, 0])
        outs_p[5].append(hl[:, 1])

        (convo_s, su_s, u_s, q_s, k_s, v_s, lf_s) = _inproj_sample(
            xs, g_pre[i], wg[i], wqkv[i], wfl[i], bfl[i], conv_w[i], state_conv[i, :, 0], state_conv[i, :, 1])
        ssmy_s, hre_s, him_s = _ssm_step(
            su_s, state_ssm_re[i].reshape(bs, STATE_W), state_ssm_im[i].reshape(bs, STATE_W),
            b1r[i], b1i[i], lam1[i], c1[i], d_row[i])
        att_s = _fox_sample(page_table, q_s, k_s, v_s, lf_s, ck_flat, cv_flat, clf_flat, layer=i, n_pool=n_pool)
        xs = _post(xs, convo_s, ssmy_s, att_s, p_sample[i].reshape(bs, PLE_DIM),
                   wglu[i], wout[i], g_post[i], g_fpre[i], wgate[i], wup[i], wdown[i], g_fpost[i],
                   wpg[i], wpp[i], tm=bs)
        outs_s[0].append(k_s.reshape(bs, 1, N_HEADS, HEAD_DIM))
        outs_s[1].append(v_s.reshape(bs, 1, N_HEADS, HEAD_DIM))
        outs_s[2].append(lf_s[:, 0:N_HEADS].reshape(bs, 1, N_HEADS))
        outs_s[3].append(jnp.stack([state_conv[i, :, 1], u_s], axis=1))
        outs_s[4].append(hre_s.reshape(bs, SSM_GROUPS, SSM_STATE))
        outs_s[5].append(him_s.reshape(bs, SSM_GROUPS, SSM_STATE))

    k_p, v_p, lf_p, conv_p, re_p, im_p = [jnp.stack(a) for a in outs_p]
    k_s, v_s, lf_s, conv_s, re_s, im_s = [jnp.stack(a) for a in outs_s]
    return (xp.reshape(bp, seq_len, D_MODEL), xs.reshape(bs, 1, D_MODEL),
            k_p, v_p, lf_p, conv_p, re_p, im_p, k_s, v_s, lf_s, conv_s, re_s, im_s)
```

```python
import functools
import math

import jax
import jax.numpy as jnp
from jax import lax
from jax.experimental import pallas as pl
from jax.experimental.pallas import tpu as pltpu

F32 = jnp.float32
BF16 = jnp.bfloat16
HIGHEST = lax.Precision.HIGHEST

D_MODEL = 1024
CONV_DIM = 256
SSM_DIM = 256
ATT_DIM = 512
N_HEADS = 8
HEAD_DIM = 64
SSM_GROUPS = 16
SSM_GROUP = 16
SSM_STATE = 64
D_FF = 2816
PLE_DIM = 256
CONV_K = 3
EPS = 1e-6
LOG2E = math.log2(math.e)
LANES = 128
SSM_CHUNK = 16
FF_CHUNK = 256
STATE_W = SSM_GROUPS * SSM_STATE
CHUNK_W = SSM_CHUNK * SSM_DIM
VMEM_LIMIT = 60 * 1024 * 1024


def _params(sem, vmem=VMEM_LIMIT):
    return pltpu.CompilerParams(dimension_semantics=sem, vmem_limit_bytes=vmem)


def _rms(x, g):
    return x * lax.rsqrt(jnp.mean(x * x, axis=-1, keepdims=True) + EPS) * g


def _sigmoid(x):
    return 1.0 / (1.0 + jnp.exp(-x))


def _log_sigmoid(x):
    return jnp.minimum(x, 0.0) - jnp.log1p(jnp.exp(-jnp.abs(x)))


def _gelu_tanh(x):
    return 0.5 * x * (1.0 + jnp.tanh(math.sqrt(2.0 / math.pi) * (x + 0.044715 * (x * x * x))))


def _dot(a, b, **kw):
    return jnp.dot(a, b, preferred_element_type=F32, **kw)


def _dot_nt(a, b, **kw):
    return lax.dot_general(a, b, (((1,), (1,)), ((), ())), preferred_element_type=F32, **kw)


def _const_spec(shape):
    nd = len(shape)
    return pl.BlockSpec(shape, lambda *_: (0,) * nd, pipeline_mode=pl.Buffered(1))


def _ssm_prep_kernel(are_ref, aim_ref, ldt_ref, btr_ref, bti_ref, ctr_ref, cti_ref,
                     wg_ref, wct_ref, bd_ref, b1_ref, c1t_ref, lam_ref, pw_scr):
    t = pl.program_id(1)
    a_re = are_ref[0]
    a_im = aim_ref[0]
    dt = jnp.exp(ldt_ref[0])
    mag = jnp.exp(a_re * dt)
    lr = mag * jnp.cos(a_im * dt)
    li = mag * jnp.sin(a_im * dt)
    den = a_re * a_re + a_im * a_im
    xr = lr - 1.0
    cfr = (xr * a_re + li * a_im) / den
    cfi = (li * a_re - xr * a_im) / den
    btr = btr_ref[0]
    bti = bti_ref[0]
    bbr = cfr * btr - cfi * bti
    bbi = cfr * bti + cfi * btr
    ctr = ctr_ref[0]
    cti = cti_ref[0]

    rows = SSM_GROUPS * SSM_GROUP
    grp_r = lax.broadcasted_iota(jnp.int32, (rows, STATE_W), 0) // SSM_GROUP
    grp_c = lax.broadcasted_iota(jnp.int32, (rows, STATE_W), 1) // SSM_STATE
    mask = (grp_r == grp_c).astype(F32)

    def blockdiag(x):
        return jnp.concatenate([x] * SSM_GROUPS, axis=0) * mask

    @pl.when(t == 0)
    def _():
        pw_scr[0:1, :] = jnp.ones((1, STATE_W), F32)
        pw_scr[1:2, :] = jnp.zeros((1, STATE_W), F32)
        lam_ref[0] = jnp.zeros((8, STATE_W), F32)
        lam_ref[0, 0:1, :] = lr
        lam_ref[0, 1:2, :] = li
        b1_ref[0, :, 0:STATE_W] = blockdiag(bbr)
        b1_ref[0, :, STATE_W:] = blockdiag(bbi)
        c1t_ref[0, :, 0:STATE_W] = blockdiag(ctr)
        c1t_ref[0, :, STATE_W:] = blockdiag(-cti)

    pr = pw_scr[0:1, :]
    pi = pw_scr[1:2, :]
    nr = pr * lr - pi * li
    ni = pr * li + pi * lr
    pw_scr[0:1, :] = nr
    pw_scr[1:2, :] = ni

    @pl.when(t == SSM_CHUNK - 1)
    def _():
        lam_ref[0, 2:3, :] = nr
        lam_ref[0, 3:4, :] = ni

    wg_ref[0, :, 0:STATE_W] = blockdiag(bbr * pr - bbi * pi).astype(BF16)
    wg_ref[0, :, STATE_W:] = blockdiag(bbr * pi + bbi * pr).astype(BF16)
    wct_ref[0, :, 0:STATE_W] = blockdiag(ctr * nr - cti * ni).astype(BF16)
    wct_ref[0, :, STATE_W:] = blockdiag(-(ctr * ni + cti * nr)).astype(BF16)
    pb = jnp.concatenate([blockdiag(bbr), blockdiag(bbi)], axis=1)
    pa = jnp.concatenate([blockdiag(ctr * pr - cti * pi), blockdiag(-(ctr * pi + cti * pr))], axis=1)
    bd_ref[0, 0] = _dot_nt(pb, pa, precision=HIGHEST).astype(BF16)


def _ssm_prep(a_re, a_im, log_dt, b_re, b_im, c_re, c_im):
    depth = a_re.shape[0]
    g, p, c, l = SSM_GROUPS, SSM_STATE, SSM_GROUP, SSM_CHUNK
    rows = g * c
    row1 = lambda x: x.reshape(depth, 1, g * p)
    chan = lambda x: x.reshape(depth, c, g * p)
    args = (row1(a_re), row1(a_im), row1(jnp.repeat(log_dt, p, axis=-1)),
            chan(jnp.transpose(b_re, (0, 3, 1, 2))), chan(jnp.transpose(b_im, (0, 3, 1, 2))),
            chan(jnp.transpose(c_re, (0, 2, 1, 3))), chan(jnp.transpose(c_im, (0, 2, 1, 3))))
    spec_row = pl.BlockSpec((1, 1, g * p), lambda i, t: (i, 0, 0))
    spec_chan = pl.BlockSpec((1, c, g * p), lambda i, t: (i, 0, 0))
    out_shapes = (
        jax.ShapeDtypeStruct((depth, l * rows, 2 * g * p), BF16),
        jax.ShapeDtypeStruct((depth, l * rows, 2 * g * p), BF16),
        jax.ShapeDtypeStruct((depth, l, rows, rows), BF16),
        jax.ShapeDtypeStruct((depth, rows, 2 * g * p), F32),
        jax.ShapeDtypeStruct((depth, rows, 2 * g * p), F32),
        jax.ShapeDtypeStruct((depth, 8, g * p), F32),
    )
    out_specs = (
        pl.BlockSpec((1, rows, 2 * g * p), lambda i, t: (i, l - 1 - t, 0)),
        pl.BlockSpec((1, rows, 2 * g * p), lambda i, t: (i, t, 0)),
        pl.BlockSpec((1, 1, rows, rows), lambda i, t: (i, t, 0, 0)),
        pl.BlockSpec((1, rows, 2 * g * p), lambda i, t: (i, 0, 0)),
        pl.BlockSpec((1, rows, 2 * g * p), lambda i, t: (i, 0, 0)),
        pl.BlockSpec((1, 8, g * p), lambda i, t: (i, 0, 0)),
    )
    return pl.pallas_call(
        _ssm_prep_kernel,
        grid=(depth, l),
        in_specs=[spec_row, spec_row, spec_row, spec_chan, spec_chan, spec_chan, spec_chan],
        out_specs=out_specs,
        out_shape=out_shapes,
        scratch_shapes=[pltpu.VMEM((2, g * p), F32)],
        compiler_params=_params(("arbitrary", "arbitrary")),
        name="ssm_prep",
    )(*args)


def _inproj_prompt_kernel(x_ref, g_ref, wg_ref, wqkv_ref, wfl_ref, bfl_ref, cw_ref,
                          convo_ref, su_ref, sub_ref, q_ref, k_ref, v_ref, kb_ref, vb_ref,
                          lf_ref, ct_ref, convnew_ref, ubuf, ccar, *, tiles_per_seq, tm):
    i = pl.program_id(0)

    @pl.when(i % tiles_per_seq == 0)
    def _():
        ubuf[0:8, :] = jnp.zeros((8, CONV_DIM), F32)
        ccar[...] = jnp.zeros_like(ccar)

    h = _rms(x_ref[...], g_ref[...]).astype(BF16)
    z = _dot(h, wg_ref[...])
    cb = z[:, 0:CONV_DIM]
    cc = z[:, CONV_DIM:2 * CONV_DIM]
    cv = z[:, 2 * CONV_DIM:3 * CONV_DIM]
    su = z[:, 3 * CONV_DIM:]
    su_ref[...] = su
    sub_ref[...] = su.astype(BF16)
    u = cc * cv
    ubuf[8:8 + tm, :] = u
    u1 = ubuf[7:7 + tm, :]
    u2 = ubuf[6:6 + tm, :]
    cw = cw_ref[...]
    y = cw[0:1] * u2 + cw[1:2] * u1 + cw[2:3] * u
    convo_ref[...] = (cb * y).astype(BF16)
    convnew_ref[0] = u[tm - 2:tm, :]
    ubuf[0:8, :] = u[tm - 8:tm, :]

    zz = _dot(h, wqkv_ref[...])
    q_ref[...] = (zz[:, 0:ATT_DIM] * (HEAD_DIM ** -0.5 * LOG2E)).astype(BF16)
    k = zz[:, ATT_DIM:2 * ATT_DIM]
    v = zz[:, 2 * ATT_DIM:]
    k_ref[...] = k
    v_ref[...] = v
    kb_ref[...] = k.astype(BF16)
    vb_ref[...] = v.astype(BF16)

    fl = _dot(h, wfl_ref[...]) + bfl_ref[...]
    lane = lax.broadcasted_iota(jnp.int32, fl.shape, 1)
    lf = jnp.where(lane < N_HEADS, _log_sigmoid(fl), 0.0)
    lf_ref[...] = lf[:, 0:N_HEADS]
    row = lax.broadcasted_iota(jnp.int32, (tm, tm), 0)
    col = lax.broadcasted_iota(jnp.int32, (tm, tm), 1)
    tri = (row >= col).astype(F32)
    c = _dot(tri, lf, precision=HIGHEST) + ccar[...]
    ccar[...] = c[tm - 1:tm, :]
    ct_ref[...] = c.T[0:N_HEADS, :]


def _inproj_prompt(x, g_pre, wg, wqkv, wfl, bfl, conv_w, *, seq_len, tm):
    n = x.shape[0]
    nt = n // tm
    tiles_per_seq = seq_len // tm
    nseq = n // seq_len
    row = lambda w: pl.BlockSpec((tm, w), lambda i: (i, 0))
    out_shapes = (
        jax.ShapeDtypeStruct((n, CONV_DIM), BF16),
        jax.ShapeDtypeStruct((n, SSM_DIM), F32),
        jax.ShapeDtypeStruct((n, SSM_DIM), BF16),
        jax.ShapeDtypeStruct((n, ATT_DIM), BF16),
        jax.ShapeDtypeStruct((n, ATT_DIM), F32),
        jax.ShapeDtypeStruct((n, ATT_DIM), F32),
        jax.ShapeDtypeStruct((n, ATT_DIM), BF16),
        jax.ShapeDtypeStruct((n, ATT_DIM), BF16),
        jax.ShapeDtypeStruct((n, N_HEADS), F32),
        jax.ShapeDtypeStruct((N_HEADS, n), F32),
        jax.ShapeDtypeStruct((nseq, CONV_K - 1, CONV_DIM), F32),
    )
    out_specs = (
        row(CONV_DIM), row(SSM_DIM), row(SSM_DIM), row(ATT_DIM), row(ATT_DIM), row(ATT_DIM),
        row(ATT_DIM), row(ATT_DIM), row(N_HEADS),
        pl.BlockSpec((N_HEADS, tm), lambda i: (0, i)),
        pl.BlockSpec((1, CONV_K - 1, CONV_DIM), lambda i: (i // tiles_per_seq, 0, 0)),
    )
    return pl.pallas_call(
        functools.partial(_inproj_prompt_kernel, tiles_per_seq=tiles_per_seq, tm=tm),
        grid=(nt,),
        in_specs=[row(D_MODEL), _const_spec((1, D_MODEL)), _const_spec(wg.shape), _const_spec(wqkv.shape),
                  _const_spec(wfl.shape), _const_spec((1, LANES)), _const_spec((CONV_K, CONV_DIM))],
        out_specs=out_specs,
        out_shape=out_shapes,
        scratch_shapes=[pltpu.VMEM((tm + 8, CONV_DIM), F32), pltpu.VMEM((1, LANES), F32)],
        compiler_params=_params(("arbitrary",)),
        name="inproj_prompt",
    )(x, g_pre, wg, wqkv, wfl, bfl, conv_w)


def _inproj_sample_kernel(x_ref, g_ref, wg_ref, wqkv_ref, wfl_ref, bfl_ref, cw_ref, b0_ref, b1_ref,
                          convo_ref, su_ref, u_ref, q_ref, k_ref, v_ref, lf_ref, sn_ref):
    h = _rms(x_ref[...], g_ref[...]).astype(BF16)
    z = _dot(h, wg_ref[...])
    cb = z[:, 0:CONV_DIM]
    u = z[:, CONV_DIM:2 * CONV_DIM] * z[:, 2 * CONV_DIM:3 * CONV_DIM]
    su_ref[...] = z[:, 3 * CONV_DIM:]
    u_ref[...] = u
    cw = cw_ref[...]
    y = cw[0:1] * b0_ref[...] + cw[1:2] * b1_ref[...] + cw[2:3] * u
    convo_ref[...] = (cb * y).astype(BF16)
    zz = _dot(h, wqkv_ref[...])
    q = zz[:, 0:ATT_DIM] * (HEAD_DIM ** -0.5)
    k = zz[:, ATT_DIM:2 * ATT_DIM]
    q_ref[...] = q
    k_ref[...] = k
    v_ref[...] = zz[:, 2 * ATT_DIM:]
    fl = _dot(h, wfl_ref[...]) + bfl_ref[...]
    lane = lax.broadcasted_iota(jnp.int32, fl.shape, 1)
    lf_ref[...] = jnp.where(lane < N_HEADS, _log_sigmoid(fl), 0.0)
    hd_row = lax.broadcasted_iota(jnp.int32, (ATT_DIM, LANES), 0) // HEAD_DIM
    hd_col = lax.broadcasted_iota(jnp.int32, (ATT_DIM, LANES), 1)
    sn_ref[...] = _dot(q * k, (hd_row == hd_col).astype(F32), precision=HIGHEST)


def _inproj_sample(x, g_pre, wg, wqkv, wfl, bfl, conv_w, buf0, buf1):
    n = x.shape[0]
    full = lambda s: pl.BlockSpec(s, lambda i: (0,) * len(s))
    out_shapes = (
        jax.ShapeDtypeStruct((n, CONV_DIM), BF16),
        jax.ShapeDtypeStruct((n, SSM_DIM), F32),
        jax.ShapeDtypeStruct((n, CONV_DIM), F32),
        jax.ShapeDtypeStruct((n, ATT_DIM), F32),
        jax.ShapeDtypeStruct((n, ATT_DIM), F32),
        jax.ShapeDtypeStruct((n, ATT_DIM), F32),
        jax.ShapeDtypeStruct((n, LANES), F32),
        jax.ShapeDtypeStruct((n, LANES), F32),
    )
    args = (x, g_pre, wg, wqkv, wfl, bfl, conv_w, buf0, buf1)
    return pl.pallas_call(
        _inproj_sample_kernel,
        grid=(1,),
        in_specs=[full(a.shape) for a in args],
        out_specs=tuple(full(s.shape) for s in out_shapes),
        out_shape=out_shapes,
        compiler_params=_params(("arbitrary",)),
        name="inproj_sample",
    )(*args)


def _ssm_chunk_state_kernel(u_ref, w_ref, o_ref):
    o_ref[...] = _dot(u_ref[...], w_ref[0])


def _ssm_chunk_state(uflat, w_g, layer, *, tm, tn):
    m, k = uflat.shape
    n = w_g.shape[2]
    return pl.pallas_call(
        _ssm_chunk_state_kernel,
        grid=(n // tn, m // tm),
        in_specs=[pl.BlockSpec((tm, k), lambda j, i: (i, 0)),
                  pl.BlockSpec((1, k, tn), lambda j, i: (layer, 0, j))],
        out_specs=pl.BlockSpec((tm, tn), lambda j, i: (i, j)),
        out_shape=jax.ShapeDtypeStruct((m, n), F32),
        compiler_params=_params(("arbitrary", "arbitrary")),
        name="ssm_chunk_state",
    )(uflat, w_g)


def _ssm_scan_kernel(g_ref, lam_ref, hin_ref, hlast_ref, *, n_chunks):
    lr = lam_ref[0, 2:3, :]
    li = lam_ref[0, 3:4, :]

    def body(c, carry):
        h_re, h_im = carry
        hin_ref[pl.ds(c, 1), 0:STATE_W] = h_re
        hin_ref[pl.ds(c, 1), STATE_W:] = h_im
        g_re = g_ref[pl.ds(c, 1), 0:STATE_W]
        g_im = g_ref[pl.ds(c, 1), STATE_W:]
        return lr * h_re - li * h_im + g_re, lr * h_im + li * h_re + g_im

    zero = jnp.zeros((1, STATE_W), F32)
    h_re, h_im = lax.fori_loop(0, n_chunks, body, (zero, zero))
    hlast_ref[0, :, 0:STATE_W] = h_re
    hlast_ref[0, :, STATE_W:] = h_im


def _ssm_scan(gstate, lam, layer, *, chunks_per_seq):
    m, w = gstate.shape
    nseq = m // chunks_per_seq
    return pl.pallas_call(
        functools.partial(_ssm_scan_kernel, n_chunks=chunks_per_seq),
        grid=(nseq,),
        in_specs=[pl.BlockSpec((chunks_per_seq, w), lambda b: (b, 0)),
                  pl.BlockSpec((1, 8, STATE_W), lambda b: (layer, 0, 0))],
        out_specs=(pl.BlockSpec((chunks_per_seq, w), lambda b: (b, 0)),
                   pl.BlockSpec((1, 1, w), lambda b: (b, 0, 0))),
        out_shape=(jax.ShapeDtypeStruct((m, w), F32), jax.ShapeDtypeStruct((nseq, 1, w), F32)),
        compiler_params=_params(("arbitrary",)),
        name="ssm_scan",
    )(gstate, lam)


def _ssm_output_kernel(ub_ref, hin_ref, uf_ref, bd_ref, wct_ref, d_ref, y_ref, hb_scr):
    j = pl.program_id(1)
    cw = SSM_DIM

    @pl.when(j == 0)
    def _():
        hb_scr[...] = hin_ref[...].astype(BF16)

    y_ref[...] = _dot_nt(hb_scr[...], wct_ref[0]) + d_ref[...] * uf_ref[...]
    for s in range(SSM_CHUNK):
        @pl.when(s <= j)
        def _():
            y_ref[...] += _dot(ub_ref[:, s * cw:(s + 1) * cw], bd_ref[0, j - s])


def _ssm_output(uflat_b, hin, uflat_f, bd, wct, d_row, layer, *, tm):
    m, k = uflat_b.shape
    ks = hin.shape[1]
    cw = SSM_DIM
    return pl.pallas_call(
        _ssm_output_kernel,
        grid=(m // tm, SSM_CHUNK),
        in_specs=[pl.BlockSpec((tm, k), lambda i, j: (i, 0)),
                  pl.BlockSpec((tm, ks), lambda i, j: (i, 0)),
                  pl.BlockSpec((tm, cw), lambda i, j: (i, j)),
                  pl.BlockSpec((1, SSM_CHUNK, cw, cw), lambda i, j: (layer, 0, 0, 0)),
                  pl.BlockSpec((1, cw, ks), lambda i, j: (layer, j, 0)),
                  pl.BlockSpec((1, cw), lambda i, j: (0, 0))],
        out_specs=pl.BlockSpec((tm, cw), lambda i, j: (i, j)),
        out_shape=jax.ShapeDtypeStruct((m, k), F32),
        scratch_shapes=[pltpu.VMEM((tm, ks), BF16)],
        compiler_params=_params(("arbitrary", "arbitrary")),
        name="ssm_output",
    )(uflat_b, hin, uflat_f, bd, wct, d_row)


def _ssm_step_kernel(u_ref, hre_ref, him_ref, b1_ref, lam_ref, c1t_ref, d_ref,
                     y_ref, ore_ref, oim_ref):
    u = u_ref[...]
    lr = lam_ref[0, 0:1, :]
    li = lam_ref[0, 1:2, :]
    h_re = hre_ref[...]
    h_im = him_ref[...]
    bu = _dot(u, b1_ref[0], precision=HIGHEST)
    n_re = lr * h_re - li * h_im + bu[:, 0:STATE_W]
    n_im = lr * h_im + li * h_re + bu[:, STATE_W:]
    ore_ref[...] = n_re
    oim_ref[...] = n_im
    y = (_dot_nt(n_re, c1t_ref[0, :, 0:STATE_W], precision=HIGHEST)
         + _dot_nt(n_im, c1t_ref[0, :, STATE_W:], precision=HIGHEST))
    y_ref[...] = y + d_ref[...] * u


def _ssm_step(u, h_re, h_im, b1, lam, c1t, d, layer):
    n = u.shape[0]
    full = lambda s: pl.BlockSpec(s, lambda i: (0,) * len(s))
    lay = lambda s: pl.BlockSpec((1,) + s[1:], lambda i: (layer,) + (0,) * (len(s) - 1))
    out_shapes = (jax.ShapeDtypeStruct((n, SSM_DIM), F32), jax.ShapeDtypeStruct((n, STATE_W), F32),
                  jax.ShapeDtypeStruct((n, STATE_W), F32))
    return pl.pallas_call(
        _ssm_step_kernel,
        grid=(1,),
        in_specs=[full(u.shape), full(h_re.shape), full(h_im.shape), lay(b1.shape), lay(lam.shape),
                  lay(c1t.shape), full(d.shape)],
        out_specs=tuple(full(s.shape) for s in out_shapes),
        out_shape=out_shapes,
        compiler_params=_params(("arbitrary",)),
        name="ssm_step",
    )(u, h_re, h_im, b1, lam, c1t, d)


def _fox_prompt_kernel(q_ref, k_ref, v_ref, cq_ref, ck_ref, o_ref, m_scr, l_scr, acc_scr, cq_scr, *, tq, tk):
    qi = pl.program_id(2)
    ki = pl.program_id(3)
    reps = tk // LANES

    @pl.when(ki == 0)
    def _():
        m_scr[...] = jnp.full(m_scr.shape, -jnp.inf, F32)
        l_scr[...] = jnp.zeros_like(l_scr)
        acc_scr[...] = jnp.zeros_like(acc_scr)
        for hh in range(2):
            cq_scr[hh] = jnp.broadcast_to(cq_ref[hh] * LOG2E, (tq, LANES))

    def step(on_diagonal):
        q2 = q_ref[...]
        k2 = k_ref[...]
        v2 = v_ref[...]
        lane = lax.broadcasted_iota(jnp.int32, q2.shape, 1)
        keep = [(lane < HEAD_DIM).astype(F32).astype(BF16), (lane >= HEAD_DIM).astype(F32).astype(BF16)]
        if on_diagonal:
            causal = (lax.broadcasted_iota(jnp.int32, (tq, tk), 1)
                      <= lax.broadcasted_iota(jnp.int32, (tq, tk), 0))
        alphas = []
        pvs = []
        for hh in range(2):
            s = _dot_nt(q2 * keep[hh], k2) - ck_ref[0, hh:hh + 1, :] * LOG2E
            if on_diagonal:
                s = jnp.where(causal, s, -jnp.inf)
            cq = cq_scr[hh]
            m_prev = m_scr[hh]
            m_new = jnp.maximum(m_prev, jnp.max(s, axis=-1, keepdims=True) + cq)
            p = jnp.exp2(s - jnp.tile(m_new - cq, (1, reps)))
            alpha = jnp.exp2(m_prev - m_new)
            l_scr[hh] = alpha * l_scr[hh] + jnp.sum(p, axis=-1, keepdims=True)
            m_scr[hh] = m_new
            alphas.append(alpha)
            pvs.append(_dot(p.astype(BF16), v2))
        lane_o = lax.broadcasted_iota(jnp.int32, (tq, LANES), 1) < HEAD_DIM
        acc_scr[...] = jnp.where(lane_o, alphas[0], alphas[1]) * acc_scr[...] + jnp.where(lane_o, pvs[0], pvs[1])

    @pl.when(ki < qi)
    def _():
        step(False)

    @pl.when(ki == qi)
    def _():
        step(True)
        lane_o = lax.broadcasted_iota(jnp.int32, (tq, LANES), 1) < HEAD_DIM
        o_ref[...] = (acc_scr[...] / jnp.where(lane_o, l_scr[0], l_scr[1])).astype(o_ref.dtype)


def _fox_prompt(q, kb, vb, ct, *, seq_len, tq):
    n = q.shape[0]
    nseq = n // seq_len
    nq = seq_len // tq
    hp = N_HEADS // 2
    cq = ct.reshape(N_HEADS, n, 1)
    ck = ct.reshape(hp, 2, n)
    kv_idx = lambda b, h, i, j: (b * nq + jnp.minimum(j, i), h)
    return pl.pallas_call(
        functools.partial(_fox_prompt_kernel, tq=tq, tk=tq),
        grid=(nseq, hp, nq, nq),
        in_specs=[pl.BlockSpec((tq, LANES), lambda b, h, i, j: (b * nq + i, h)),
                  pl.BlockSpec((tq, LANES), kv_idx),
                  pl.BlockSpec((tq, LANES), kv_idx),
                  pl.BlockSpec((2, tq, 1), lambda b, h, i, j: (h, b * nq + i, 0)),
                  pl.BlockSpec((1, 2, tq), lambda b, h, i, j: (h, 0, b * nq + jnp.minimum(j, i)))],
        out_specs=pl.BlockSpec((tq, LANES), lambda b, h, i, j: (b * nq + i, h)),
        out_shape=jax.ShapeDtypeStruct((n, ATT_DIM), BF16),
        scratch_shapes=[pltpu.VMEM((2, tq, LANES), F32), pltpu.VMEM((2, tq, LANES), F32),
                        pltpu.VMEM((tq, LANES), F32), pltpu.VMEM((2, tq, LANES), F32)],
        compiler_params=_params(("arbitrary", "arbitrary", "arbitrary", "arbitrary")),
        name="fox_prompt",
    )(q, kb, vb, cq, ck)


PAGES_PER_STEP = 8


def _fox_sample_kernel(pt_ref, q_ref, vn_ref, sn_ref, lfn_ref, *refs, page_size):
    np_ = PAGES_PER_STEP
    k_refs = refs[0:np_]
    v_refs = refs[np_:2 * np_]
    lf_refs = refs[2 * np_:3 * np_]
    o_ref = refs[3 * np_]
    m_scr, l_scr, car_scr, acc_scr = refs[3 * np_ + 1:]
    j = pl.program_id(1)
    nj = pl.num_programs(1)

    eye = (lax.broadcasted_iota(jnp.int32, (N_HEADS, LANES), 0)
           == lax.broadcasted_iota(jnp.int32, (N_HEADS, LANES), 1)).astype(F32)

    def to_col(row):
        return jnp.sum(eye * row, axis=-1, keepdims=True)

    @pl.when(j == 0)
    def _():
        m_scr[...] = to_col(sn_ref[0])
        l_scr[...] = jnp.ones_like(l_scr)
        car_scr[...] = to_col(lfn_ref[0])
        lane0 = lax.broadcasted_iota(jnp.int32, (HEAD_DIM, page_size), 1) == 0
        for h in range(N_HEADS):
            vcol = vn_ref[0, h * HEAD_DIM:(h + 1) * HEAD_DIM, :]
            acc_scr[h] = jnp.where(lane0, vcol, 0.0)

    later = (lax.broadcasted_iota(jnp.int32, (page_size, page_size), 0)
             > lax.broadcasted_iota(jnp.int32, (page_size, page_size), 1)).astype(F32)
    fold = (lax.broadcasted_iota(jnp.int32, (N_HEADS, N_HEADS * 8), 1) // 8
            == lax.broadcasted_iota(jnp.int32, (N_HEADS, N_HEADS * 8), 0)).astype(F32)
    carry = car_scr[...]
    s_list = []
    for r in range(np_):
        parts = []
        for h in range(N_HEADS):
            prod = k_refs[r][0, 0, h] * q_ref[0, h * HEAD_DIM:(h + 1) * HEAD_DIM, :]
            t = prod[0:8]
            for a in range(1, HEAD_DIM // 8):
                t = t + prod[a * 8:(a + 1) * 8]
            parts.append(t)
        s = _dot(fold, jnp.concatenate(parts, axis=0), precision=HIGHEST)
        lf = lf_refs[r][0, 0]
        bias = _dot(lf, later, precision=HIGHEST) + carry
        carry = carry + jnp.sum(lf, axis=-1, keepdims=True)
        s_list.append(s + bias)
    car_scr[...] = carry
    s_all = jnp.concatenate(s_list, axis=1)
    m_prev = m_scr[...]
    m_new = jnp.maximum(m_prev, jnp.max(s_all, axis=-1, keepdims=True))
    alpha = jnp.exp(m_prev - m_new)
    p_all = jnp.exp(s_all - m_new)
    l_scr[...] = alpha * l_scr[...] + jnp.sum(p_all, axis=-1, keepdims=True)
    m_scr[...] = m_new
    for h in range(N_HEADS):
        a = acc_scr[h] * alpha[h:h + 1, :]
        for r in range(np_):
            a = a + p_all[h:h + 1, r * page_size:(r + 1) * page_size] * v_refs[r][0, 0, h]
        acc_scr[h] = a

    @pl.when(j == nj - 1)
    def _():
        inv = 1.0 / l_scr[...]
        for h in range(N_HEADS):
            o_ref[0, h] = jnp.sum(acc_scr[h], axis=-1, keepdims=True) * inv[h:h + 1, :]


def _fox_sample(page_table, q, v_new, s_new, lf_new, cache_kt, cache_vt, cache_lft, *, layer):
    nseq, n_pages = page_table.shape
    page_size = cache_kt.shape[-1]
    np_ = PAGES_PER_STEP
    nj = n_pages // np_

    def page_idx(r, nd):
        def f(b, j, pt):
            return (layer, pt[b * n_pages + (n_pages - 1 - (j * np_ + r))]) + (0,) * nd
        return f

    col = pl.BlockSpec((1, ATT_DIM, 1), lambda b, j, pt: (b, 0, 0))
    tok = pl.BlockSpec((1, 1, LANES), lambda b, j, pt: (b, 0, 0))
    kv_block = (1, 1, N_HEADS, HEAD_DIM, page_size)
    in_specs = [col, col, tok, tok]
    in_specs += [pl.BlockSpec(kv_block, page_idx(r, 3)) for r in range(np_)]
    in_specs += [pl.BlockSpec(kv_block, page_idx(r, 3)) for r in range(np_)]
    in_specs += [pl.BlockSpec((1, 1, N_HEADS, page_size), page_idx(r, 2)) for r in range(np_)]
    grid_spec = pltpu.PrefetchScalarGridSpec(
        num_scalar_prefetch=1,
        grid=(nseq, nj),
        in_specs=in_specs,
        out_specs=pl.BlockSpec((1, N_HEADS, HEAD_DIM, 1), lambda b, j, pt: (b, 0, 0, 0)),
        scratch_shapes=[pltpu.VMEM((N_HEADS, 1), F32), pltpu.VMEM((N_HEADS, 1), F32),
                        pltpu.VMEM((N_HEADS, 1), F32), pltpu.VMEM((N_HEADS, HEAD_DIM, page_size), F32)],
    )
    args = [q.reshape(nseq, ATT_DIM, 1), v_new.reshape(nseq, ATT_DIM, 1),
            s_new.reshape(nseq, 1, LANES), lf_new.reshape(nseq, 1, LANES)]
    args += [cache_kt] * np_ + [cache_vt] * np_ + [cache_lft] * np_
    out = pl.pallas_call(
        functools.partial(_fox_sample_kernel, page_size=page_size),
        grid_spec=grid_spec,
        out_shape=jax.ShapeDtypeStruct((nseq, N_HEADS, HEAD_DIM, 1), F32),
        compiler_params=_params(("arbitrary", "arbitrary")),
        name="fox_sample",
    )(page_table.reshape(-1), *args)
    return out.reshape(nseq, ATT_DIM).astype(BF16)


def _post_kernel(x_ref, convo_ref, ssmy_ref, att_ref, p_ref, wglu_ref, wout_ref, gpost_ref, gfpre_ref,
                 wgate_ref, wup_ref, wdown_ref, gfpost_ref, wpg_ref, wpp_ref, o_ref):
    y = _gelu_tanh(ssmy_ref[...])
    ssm_out = y * _sigmoid(_dot(y.astype(BF16), wglu_ref[...]))
    mix = (_dot(convo_ref[...], wout_ref[0:CONV_DIM, :])
           + _dot(ssm_out.astype(BF16), wout_ref[CONV_DIM:CONV_DIM + SSM_DIM, :])
           + _dot(att_ref[...], wout_ref[CONV_DIM + SSM_DIM:, :]))
    x1 = x_ref[...] + _rms(mix, gpost_ref[...])
    h2 = _rms(x1, gfpre_ref[...]).astype(BF16)
    ffn = jnp.zeros(x1.shape, F32)
    for c in range(D_FF // FF_CHUNK):
        cols = slice(c * FF_CHUNK, (c + 1) * FF_CHUNK)
        gate = _dot(h2, wgate_ref[:, cols])
        up = _dot(h2, wup_ref[:, cols])
        act = (gate * _sigmoid(gate) * up).astype(BF16)
        ffn = ffn + _dot(act, wdown_ref[cols, :])
    x2 = x1 + _rms(ffn, gfpost_ref[...])
    pgate = _sigmoid(_dot(x2.astype(BF16), wpg_ref[...]))
    o_ref[...] = x2 + pgate * _dot(p_ref[...].astype(BF16), wpp_ref[...])


def _post(x, convo, ssmy, att, pemb, wglu, wout, gpost, gfpre, wgate, wup, wdown, gfpost, wpg, wpp, *, tm):
    n = x.shape[0]
    row = lambda w: pl.BlockSpec((tm, w), lambda i: (i, 0))
    weights = (wglu, wout, gpost, gfpre, wgate, wup, wdown, gfpost, wpg, wpp)
    return pl.pallas_call(
        _post_kernel,
        grid=(n // tm,),
        in_specs=[row(D_MODEL), row(CONV_DIM), row(SSM_DIM), row(ATT_DIM), row(PLE_DIM)]
                 + [_const_spec(w.shape) for w in weights],
        out_specs=row(D_MODEL),
        out_shape=jax.ShapeDtypeStruct((n, D_MODEL), F32),
        compiler_params=_params(("arbitrary",)),
        name="post_mixer",
    )(x, convo, ssmy, att, pemb, *weights)


def kernel(x_prompt, x_sample, cache_k, cache_v, cache_logf, state_conv, state_ssm_re, state_ssm_im, page_table, p_prompt, p_sample, norm_mix_pre, norm_mix_post, norm_ffn_pre, norm_ffn_post, w_in, b_forget, conv_w, ssm_a_re, ssm_a_im, ssm_log_dt, ssm_b_re, ssm_b_im, ssm_c_re, ssm_c_im, ssm_d, w_ssm_glu, w_out, w_ffn_gate, w_ffn_up, w_ffn_down, w_ple_gate, w_ple_proj):
    depth = w_in.shape[0]
    bp, seq_len, _ = x_prompt.shape
    bs = x_sample.shape[0]
    n_p = bp * seq_len
    tm = min(512, seq_len)
    tq = min(512, seq_len)
    chunks_per_seq = seq_len // SSM_CHUNK
    n_chunks = n_p // SSM_CHUNK
    tmc = min(512, n_chunks)

    gate_cols = 3 * CONV_DIM + SSM_DIM
    wg = w_in[:, :, 0:gate_cols].astype(BF16)
    wqkv = w_in[:, :, gate_cols:gate_cols + 3 * ATT_DIM].astype(BF16)
    wfl = jnp.pad(w_in[:, :, gate_cols + 3 * ATT_DIM:], ((0, 0), (0, 0), (0, LANES - N_HEADS))).astype(BF16)
    bfl = jnp.pad(b_forget, ((0, 0), (0, LANES - N_HEADS)))[:, None, :]
    wglu = w_ssm_glu.astype(BF16)
    wout = w_out.astype(BF16)
    wgate = w_ffn_gate.astype(BF16)
    wup = w_ffn_up.astype(BF16)
    wdown = w_ffn_down.astype(BF16)
    wpg = w_ple_gate.astype(BF16)
    wpp = w_ple_proj.astype(BF16)
    g_pre = norm_mix_pre[:, None, :]
    g_post = norm_mix_post[:, None, :]
    g_fpre = norm_ffn_pre[:, None, :]
    g_fpost = norm_ffn_post[:, None, :]
    d_row = ssm_d[:, None, :]

    w_g, w_ct, bd, b1, c1t, lam = _ssm_prep(ssm_a_re, ssm_a_im, ssm_log_dt, ssm_b_re, ssm_b_im,
                                            ssm_c_re, ssm_c_im)

    cache_kt = jnp.transpose(cache_k, (0, 1, 3, 4, 2))
    cache_vt = jnp.transpose(cache_v, (0, 1, 3, 4, 2))
    cache_lft = jnp.transpose(cache_logf, (0, 1, 3, 2))

    xp = x_prompt.reshape(n_p, D_MODEL)
    xs = x_sample.reshape(bs, D_MODEL)
    outs_p = [[] for _ in range(6)]
    outs_s = [[] for _ in range(6)]
    for i in range(depth):
        (convo, su, sub, q, k, v, kb, vb, lf, ct, conv_new) = _inproj_prompt(
            xp, g_pre[i], wg[i], wqkv[i], wfl[i], bfl[i], conv_w[i], seq_len=seq_len, tm=tm)
        uflat_b = sub.reshape(n_chunks, CHUNK_W)
        uflat_f = su.reshape(n_chunks, CHUNK_W)
        gstate = _ssm_chunk_state(uflat_b, w_g, i, tm=tmc, tn=1024)
        hin, hlast = _ssm_scan(gstate, lam, i, chunks_per_seq=chunks_per_seq)
        ssmy = _ssm_output(uflat_b, hin, uflat_f, bd, w_ct, d_row[i], i, tm=tmc)
        att = _fox_prompt(q, kb, vb, ct, seq_len=seq_len, tq=tq)
        xp = _post(xp, convo, ssmy.reshape(n_p, SSM_DIM), att, p_prompt[i].reshape(n_p, PLE_DIM),
                   wglu[i], wout[i], g_post[i], g_fpre[i], wgate[i], wup[i], wdown[i], g_fpost[i],
                   wpg[i], wpp[i], tm=tm)
        outs_p[0].append(k.reshape(bp, seq_len, N_HEADS, HEAD_DIM))
        outs_p[1].append(v.reshape(bp, seq_len, N_HEADS, HEAD_DIM))
        outs_p[2].append(lf.reshape(bp, seq_len, N_HEADS))
        outs_p[3].append(conv_new)
        hl = hlast.reshape(bp, 2, SSM_GROUPS, SSM_STATE)
        outs_p[4].append(hl[:, 0])
        outs_p[5].append(hl[:, 1])

        (convo_s, su_s, u_s, q_s, k_s, v_s, lf_s, sn_s) = _inproj_sample(
            xs, g_pre[i], wg[i], wqkv[i], wfl[i], bfl[i], conv_w[i], state_conv[i, :, 0], state_conv[i, :, 1])
        ssmy_s, hre_s, him_s = _ssm_step(
            su_s, state_ssm_re[i].reshape(bs, STATE_W), state_ssm_im[i].reshape(bs, STATE_W),
            b1, lam, c1t, d_row[i], i)
        att_s = _fox_sample(page_table, q_s, v_s, sn_s, lf_s, cache_kt, cache_vt, cache_lft, layer=i)
        xs = _post(xs, convo_s, ssmy_s, att_s, p_sample[i].reshape(bs, PLE_DIM),
                   wglu[i], wout[i], g_post[i], g_fpre[i], wgate[i], wup[i], wdown[i], g_fpost[i],
                   wpg[i], wpp[i], tm=bs)
        outs_s[0].append(k_s.reshape(bs, 1, N_HEADS, HEAD_DIM))
        outs_s[1].append(v_s.reshape(bs, 1, N_HEADS, HEAD_DIM))
        outs_s[2].append(lf_s[:, 0:N_HEADS].reshape(bs, 1, N_HEADS))
        outs_s[3].append(jnp.stack([state_conv[i, :, 1], u_s], axis=1))
        outs_s[4].append(hre_s.reshape(bs, SSM_GROUPS, SSM_STATE))
        outs_s[5].append(him_s.reshape(bs, SSM_GROUPS, SSM_STATE))

    k_p, v_p, lf_p, conv_p, re_p, im_p = [jnp.stack(a) for a in outs_p]
    k_s, v_s, lf_s, conv_s, re_s, im_s = [jnp.stack(a) for a in outs_s]
    return (xp.reshape(bp, seq_len, D_MODEL), xs.reshape(bs, 1, D_MODEL),
            k_p, v_p, lf_p, conv_p, re_p, im_p, k_s, v_s, lf_s, conv_s, re_s, im_s)
```

```python
import functools
import math

import jax
import jax.numpy as jnp
from jax import lax
from jax.experimental import pallas as pl
from jax.experimental.pallas import tpu as pltpu

F32 = jnp.float32
BF16 = jnp.bfloat16
HIGHEST = lax.Precision.HIGHEST

D_MODEL = 1024
CONV_DIM = 256
SSM_DIM = 256
ATT_DIM = 512
N_HEADS = 8
HEAD_DIM = 64
SSM_GROUPS = 16
SSM_GROUP = 16
SSM_STATE = 64
D_FF = 2816
PLE_DIM = 256
CONV_K = 3
EPS = 1e-6
LOG2E = math.log2(math.e)
LANES = 128
SSM_CHUNK = 16
FF_CHUNK = 256
STATE_W = SSM_GROUPS * SSM_STATE
CHUNK_W = SSM_CHUNK * SSM_DIM
VMEM_LIMIT = 60 * 1024 * 1024


def _params(sem, vmem=VMEM_LIMIT):
    return pltpu.CompilerParams(dimension_semantics=sem, vmem_limit_bytes=vmem)


def _rms(x, g):
    return x * lax.rsqrt(jnp.mean(x * x, axis=-1, keepdims=True) + EPS) * g


def _sigmoid(x):
    return 1.0 / (1.0 + jnp.exp(-x))


def _log_sigmoid(x):
    return jnp.minimum(x, 0.0) - jnp.log1p(jnp.exp(-jnp.abs(x)))


def _gelu_tanh(x):
    return 0.5 * x * (1.0 + jnp.tanh(math.sqrt(2.0 / math.pi) * (x + 0.044715 * (x * x * x))))


def _dot(a, b, **kw):
    return jnp.dot(a, b, preferred_element_type=F32, **kw)


def _dot_nt(a, b, **kw):
    return lax.dot_general(a, b, (((1,), (1,)), ((), ())), preferred_element_type=F32, **kw)


def _split3(x):
    hi = x.astype(BF16)
    r1 = x - hi.astype(F32)
    mid = r1.astype(BF16)
    lo = (r1 - mid.astype(F32)).astype(BF16)
    return hi, mid, lo


def _layer_spec(arr, layer):
    zeros = (0,) * (arr.ndim - 1)
    return pl.BlockSpec((1,) + arr.shape[1:], lambda *_: (layer,) + zeros, pipeline_mode=pl.Buffered(1))


def _ssm_prep_kernel(are_ref, aim_ref, ldt_ref, btr_ref, bti_ref, ctr_ref, cti_ref,
                     wg_ref, wct_ref, bd_ref, b1_ref, c1t_ref, lam_ref, pw_scr):
    t = pl.program_id(1)
    a_re = are_ref[0]
    a_im = aim_ref[0]
    dt = jnp.exp(ldt_ref[0])
    mag = jnp.exp(a_re * dt)
    lr = mag * jnp.cos(a_im * dt)
    li = mag * jnp.sin(a_im * dt)
    den = a_re * a_re + a_im * a_im
    xr = lr - 1.0
    cfr = (xr * a_re + li * a_im) / den
    cfi = (li * a_re - xr * a_im) / den
    btr = btr_ref[0]
    bti = bti_ref[0]
    bbr = cfr * btr - cfi * bti
    bbi = cfr * bti + cfi * btr
    ctr = ctr_ref[0]
    cti = cti_ref[0]

    rows = SSM_GROUPS * SSM_GROUP
    grp_r = lax.broadcasted_iota(jnp.int32, (rows, STATE_W), 0) // SSM_GROUP
    grp_c = lax.broadcasted_iota(jnp.int32, (rows, STATE_W), 1) // SSM_STATE
    mask = (grp_r == grp_c).astype(F32)

    def blockdiag(x):
        return jnp.concatenate([x] * SSM_GROUPS, axis=0) * mask

    @pl.when(t == 0)
    def _():
        pw_scr[0:1, :] = jnp.ones((1, STATE_W), F32)
        pw_scr[1:2, :] = jnp.zeros((1, STATE_W), F32)
        lam_ref[0] = jnp.zeros((8, STATE_W), F32)
        lam_ref[0, 0:1, :] = lr
        lam_ref[0, 1:2, :] = li
        b1_ref[0, :, 0:STATE_W] = blockdiag(bbr)
        b1_ref[0, :, STATE_W:] = blockdiag(bbi)
        c1t_ref[0, :, 0:STATE_W] = blockdiag(ctr)
        c1t_ref[0, :, STATE_W:] = blockdiag(-cti)

    pr = pw_scr[0:1, :]
    pi = pw_scr[1:2, :]
    nr = pr * lr - pi * li
    ni = pr * li + pi * lr
    pw_scr[0:1, :] = nr
    pw_scr[1:2, :] = ni

    @pl.when(t == SSM_CHUNK - 1)
    def _():
        lam_ref[0, 2:3, :] = nr
        lam_ref[0, 3:4, :] = ni

    wg_ref[0, :, 0:STATE_W] = blockdiag(bbr * pr - bbi * pi).astype(BF16)
    wg_ref[0, :, STATE_W:] = blockdiag(bbr * pi + bbi * pr).astype(BF16)
    wct_ref[0, :, 0:STATE_W] = blockdiag(ctr * nr - cti * ni).astype(BF16)
    wct_ref[0, :, STATE_W:] = blockdiag(-(ctr * ni + cti * nr)).astype(BF16)
    pb = jnp.concatenate([blockdiag(bbr), blockdiag(bbi)], axis=1)
    pa = jnp.concatenate([blockdiag(ctr * pr - cti * pi), blockdiag(-(ctr * pi + cti * pr))], axis=1)
    bd_ref[0, 0] = _dot_nt(pb, pa, precision=HIGHEST).astype(BF16)


def _ssm_prep(a_re, a_im, log_dt, b_re, b_im, c_re, c_im):
    depth = a_re.shape[0]
    g, p, c, l = SSM_GROUPS, SSM_STATE, SSM_GROUP, SSM_CHUNK
    rows = g * c
    row1 = lambda x: x.reshape(depth, 1, g * p)
    chan = lambda x: x.reshape(depth, c, g * p)
    args = (row1(a_re), row1(a_im), row1(jnp.repeat(log_dt, p, axis=-1)),
            chan(jnp.transpose(b_re, (0, 3, 1, 2))), chan(jnp.transpose(b_im, (0, 3, 1, 2))),
            chan(jnp.transpose(c_re, (0, 2, 1, 3))), chan(jnp.transpose(c_im, (0, 2, 1, 3))))
    spec_row = pl.BlockSpec((1, 1, g * p), lambda i, t: (i, 0, 0))
    spec_chan = pl.BlockSpec((1, c, g * p), lambda i, t: (i, 0, 0))
    out_shapes = (
        jax.ShapeDtypeStruct((depth, l * rows, 2 * g * p), BF16),
        jax.ShapeDtypeStruct((depth, l * rows, 2 * g * p), BF16),
        jax.ShapeDtypeStruct((depth, l, rows, rows), BF16),
        jax.ShapeDtypeStruct((depth, rows, 2 * g * p), F32),
        jax.ShapeDtypeStruct((depth, rows, 2 * g * p), F32),
        jax.ShapeDtypeStruct((depth, 8, g * p), F32),
    )
    out_specs = (
        pl.BlockSpec((1, rows, 2 * g * p), lambda i, t: (i, l - 1 - t, 0)),
        pl.BlockSpec((1, rows, 2 * g * p), lambda i, t: (i, t, 0)),
        pl.BlockSpec((1, 1, rows, rows), lambda i, t: (i, t, 0, 0)),
        pl.BlockSpec((1, rows, 2 * g * p), lambda i, t: (i, 0, 0)),
        pl.BlockSpec((1, rows, 2 * g * p), lambda i, t: (i, 0, 0)),
        pl.BlockSpec((1, 8, g * p), lambda i, t: (i, 0, 0)),
    )
    return pl.pallas_call(
        _ssm_prep_kernel,
        grid=(depth, l),
        in_specs=[spec_row, spec_row, spec_row, spec_chan, spec_chan, spec_chan, spec_chan],
        out_specs=out_specs,
        out_shape=out_shapes,
        scratch_shapes=[pltpu.VMEM((2, g * p), F32)],
        compiler_params=_params(("arbitrary", "arbitrary")),
        name="ssm_prep",
    )(*args)


def _inproj_prompt_kernel(x_ref, g_ref, wg_ref, wqkv_ref, wfl_ref, bfl_ref, cw_ref,
                          convo_ref, su_ref, sub_ref, q_ref, k_ref, v_ref, kb_ref, vb_ref,
                          lf_ref, ct_ref, crow_ref, convnew_ref, ubuf, ccar, su_lo, su_hi, *, tiles_per_seq, tm):
    i = pl.program_id(0)

    @pl.when(i % tiles_per_seq == 0)
    def _():
        ubuf[0:8, :] = jnp.zeros((8, CONV_DIM), F32)
        ccar[...] = jnp.zeros_like(ccar)

    h = _rms(x_ref[...], g_ref[0]).astype(BF16)
    z = _dot(h, wg_ref[0])
    cb = z[:, 0:CONV_DIM]
    cc = z[:, CONV_DIM:2 * CONV_DIM]
    cv = z[:, 2 * CONV_DIM:3 * CONV_DIM]
    for half, scr in enumerate((su_lo, su_hi)):
        scr[...] = z[:, 3 * CONV_DIM + half * LANES:3 * CONV_DIM + (half + 1) * LANES]
        for s in range(SSM_CHUNK):
            piece = scr[pl.ds(s, tm // SSM_CHUNK, stride=SSM_CHUNK), :]
            cols = slice(s * SSM_DIM + half * LANES, s * SSM_DIM + (half + 1) * LANES)
            su_ref[:, cols] = piece
            sub_ref[:, cols] = piece.astype(BF16)
    u = cc * cv
    ubuf[8:8 + tm, :] = u
    u1 = ubuf[7:7 + tm, :]
    u2 = ubuf[6:6 + tm, :]
    cw = cw_ref[0]
    y = cw[0:1] * u2 + cw[1:2] * u1 + cw[2:3] * u
    convo_ref[...] = (cb * y).astype(BF16)
    convnew_ref[0] = u[tm - 2:tm, :]
    ubuf[0:8, :] = u[tm - 8:tm, :]

    zz = _dot(h, wqkv_ref[0])
    q_ref[...] = (zz[:, 0:ATT_DIM] * (HEAD_DIM ** -0.5 * LOG2E)).astype(BF16)
    k = zz[:, ATT_DIM:2 * ATT_DIM]
    v = zz[:, 2 * ATT_DIM:]
    k_ref[...] = k
    v_ref[...] = v
    kb_ref[...] = k.astype(BF16)
    vb_ref[...] = v.astype(BF16)

    fl = _dot(h, wfl_ref[0]) + bfl_ref[0]
    lane = lax.broadcasted_iota(jnp.int32, fl.shape, 1)
    lf = jnp.where(lane < N_HEADS, _log_sigmoid(fl), 0.0)
    lf_ref[...] = lf[:, 0:N_HEADS]
    row = lax.broadcasted_iota(jnp.int32, (LANES, LANES), 0)
    col = lax.broadcasted_iota(jnp.int32, (LANES, LANES), 1)
    tri = (row >= col).astype(F32).astype(BF16)
    carry = ccar[...]
    for r0 in range(0, tm, LANES):
        hi, mid, lo = _split3(lf[r0:r0 + LANES, :])
        c = _dot(tri, hi) + _dot(tri, mid) + _dot(tri, lo) + carry
        carry = c[LANES - 1:LANES, :]
        crow_ref[r0:r0 + LANES, :] = c
        ct_ref[:, r0:r0 + LANES] = c.T[0:N_HEADS, :]
    ccar[...] = carry


def _inproj_prompt(x, g_pre, wg, wqkv, wfl, bfl, conv_w, layer, *, seq_len, tm):
    n = x.shape[0]
    nt = n // tm
    tiles_per_seq = seq_len // tm
    nseq = n // seq_len
    tc = tm // SSM_CHUNK
    row = lambda w: pl.BlockSpec((tm, w), lambda i: (i, 0))
    chunk_rows = pl.BlockSpec((tc, CHUNK_W), lambda i: (i, 0))
    out_shapes = (
        jax.ShapeDtypeStruct((n, CONV_DIM), BF16),
        jax.ShapeDtypeStruct((n // SSM_CHUNK, CHUNK_W), F32),
        jax.ShapeDtypeStruct((n // SSM_CHUNK, CHUNK_W), BF16),
        jax.ShapeDtypeStruct((n, ATT_DIM), BF16),
        jax.ShapeDtypeStruct((n, ATT_DIM), F32),
        jax.ShapeDtypeStruct((n, ATT_DIM), F32),
        jax.ShapeDtypeStruct((n, ATT_DIM), BF16),
        jax.ShapeDtypeStruct((n, ATT_DIM), BF16),
        jax.ShapeDtypeStruct((n, N_HEADS), F32),
        jax.ShapeDtypeStruct((N_HEADS, n), F32),
        jax.ShapeDtypeStruct((n, LANES), F32),
        jax.ShapeDtypeStruct((nseq, CONV_K - 1, CONV_DIM), F32),
    )
    out_specs = (
        row(CONV_DIM), chunk_rows, chunk_rows, row(ATT_DIM), row(ATT_DIM), row(ATT_DIM),
        row(ATT_DIM), row(ATT_DIM), row(N_HEADS),
        pl.BlockSpec((N_HEADS, tm), lambda i: (0, i)),
        row(LANES),
        pl.BlockSpec((1, CONV_K - 1, CONV_DIM), lambda i: (i // tiles_per_seq, 0, 0)),
    )
    weights = (g_pre, wg, wqkv, wfl, bfl, conv_w)
    return pl.pallas_call(
        functools.partial(_inproj_prompt_kernel, tiles_per_seq=tiles_per_seq, tm=tm),
        grid=(nt,),
        in_specs=[row(D_MODEL)] + [_layer_spec(w, layer) for w in weights],
        out_specs=out_specs,
        out_shape=out_shapes,
        scratch_shapes=[pltpu.VMEM((tm + 8, CONV_DIM), F32), pltpu.VMEM((1, LANES), F32),
                        pltpu.VMEM((tm, LANES), F32), pltpu.VMEM((tm, LANES), F32)],
        compiler_params=_params(("arbitrary",)),
        name="inproj_prompt",
    )(x, *weights)


def _inproj_sample_kernel(x_ref, g_ref, wg_ref, wqkv_ref, wfl_ref, bfl_ref, cw_ref, b0_ref, b1_ref,
                          convo_ref, su_ref, u_ref, q_ref, k_ref, v_ref, lf_ref, sn_ref):
    h = _rms(x_ref[...], g_ref[0]).astype(BF16)
    z = _dot(h, wg_ref[0])
    cb = z[:, 0:CONV_DIM]
    u = z[:, CONV_DIM:2 * CONV_DIM] * z[:, 2 * CONV_DIM:3 * CONV_DIM]
    su_ref[...] = z[:, 3 * CONV_DIM:]
    u_ref[...] = u
    cw = cw_ref[0]
    y = cw[0:1] * b0_ref[...] + cw[1:2] * b1_ref[...] + cw[2:3] * u
    convo_ref[...] = (cb * y).astype(BF16)
    zz = _dot(h, wqkv_ref[0])
    q = zz[:, 0:ATT_DIM] * (HEAD_DIM ** -0.5)
    k = zz[:, ATT_DIM:2 * ATT_DIM]
    q_ref[...] = q
    k_ref[...] = k
    v_ref[...] = zz[:, 2 * ATT_DIM:]
    fl = _dot(h, wfl_ref[0]) + bfl_ref[0]
    lane = lax.broadcasted_iota(jnp.int32, fl.shape, 1)
    lf_ref[...] = jnp.where(lane < N_HEADS, _log_sigmoid(fl), 0.0)
    hd_row = lax.broadcasted_iota(jnp.int32, (ATT_DIM, LANES), 0) // HEAD_DIM
    hd_col = lax.broadcasted_iota(jnp.int32, (ATT_DIM, LANES), 1)
    sn_ref[...] = _dot(q * k, (hd_row == hd_col).astype(F32), precision=HIGHEST)


def _inproj_sample(x, g_pre, wg, wqkv, wfl, bfl, conv_w, buf0, buf1, layer):
    n = x.shape[0]
    full = lambda s: pl.BlockSpec(s, lambda i: (0,) * len(s))
    weights = (g_pre, wg, wqkv, wfl, bfl, conv_w)
    out_shapes = (
        jax.ShapeDtypeStruct((n, CONV_DIM), BF16),
        jax.ShapeDtypeStruct((n, SSM_DIM), F32),
        jax.ShapeDtypeStruct((n, CONV_DIM), F32),
        jax.ShapeDtypeStruct((n, ATT_DIM), F32),
        jax.ShapeDtypeStruct((n, ATT_DIM), F32),
        jax.ShapeDtypeStruct((n, ATT_DIM), F32),
        jax.ShapeDtypeStruct((n, LANES), F32),
        jax.ShapeDtypeStruct((n, LANES), F32),
    )
    args = (x,) + weights + (buf0, buf1)
    return pl.pallas_call(
        _inproj_sample_kernel,
        grid=(1,),
        in_specs=[full(x.shape)] + [_layer_spec(w, layer) for w in weights] + [full(buf0.shape), full(buf1.shape)],
        out_specs=tuple(full(s.shape) for s in out_shapes),
        out_shape=out_shapes,
        compiler_params=_params(("arbitrary",)),
        name="inproj_sample",
    )(*args)


def _ssm_chunk_state_kernel(u_ref, w_ref, o_ref):
    o_ref[...] = _dot(u_ref[...], w_ref[0])


def _ssm_chunk_state(uflat, w_g, layer, *, tm, tn):
    m, k = uflat.shape
    n = w_g.shape[2]
    return pl.pallas_call(
        _ssm_chunk_state_kernel,
        grid=(n // tn, m // tm),
        in_specs=[pl.BlockSpec((tm, k), lambda j, i: (i, 0)),
                  pl.BlockSpec((1, k, tn), lambda j, i: (layer, 0, j))],
        out_specs=pl.BlockSpec((tm, tn), lambda j, i: (i, j)),
        out_shape=jax.ShapeDtypeStruct((m, n), F32),
        compiler_params=_params(("arbitrary", "arbitrary")),
        name="ssm_chunk_state",
    )(uflat, w_g)


def _ssm_scan_kernel(g_ref, lam_ref, hin_ref, hlast_ref, *, n_chunks):
    lr = lam_ref[0, 2:3, :]
    li = lam_ref[0, 3:4, :]

    def body(c, carry):
        h_re, h_im = carry
        hin_ref[pl.ds(c, 1), 0:STATE_W] = h_re
        hin_ref[pl.ds(c, 1), STATE_W:] = h_im
        g_re = g_ref[pl.ds(c, 1), 0:STATE_W]
        g_im = g_ref[pl.ds(c, 1), STATE_W:]
        return lr * h_re - li * h_im + g_re, lr * h_im + li * h_re + g_im

    zero = jnp.zeros((1, STATE_W), F32)
    h_re, h_im = lax.fori_loop(0, n_chunks, body, (zero, zero))
    hlast_ref[0, :, 0:STATE_W] = h_re
    hlast_ref[0, :, STATE_W:] = h_im


def _ssm_scan(gstate, lam, layer, *, chunks_per_seq):
    m, w = gstate.shape
    nseq = m // chunks_per_seq
    return pl.pallas_call(
        functools.partial(_ssm_scan_kernel, n_chunks=chunks_per_seq),
        grid=(nseq,),
        in_specs=[pl.BlockSpec((chunks_per_seq, w), lambda b: (b, 0)),
                  pl.BlockSpec((1, 8, STATE_W), lambda b: (layer, 0, 0))],
        out_specs=(pl.BlockSpec((chunks_per_seq, w), lambda b: (b, 0)),
                   pl.BlockSpec((1, 1, w), lambda b: (b, 0, 0))),
        out_shape=(jax.ShapeDtypeStruct((m, w), F32), jax.ShapeDtypeStruct((nseq, 1, w), F32)),
        compiler_params=_params(("arbitrary",)),
        name="ssm_scan",
    )(gstate, lam)


def _ssm_output_kernel(ub_ref, hin_ref, uf_ref, bd_ref, wct_ref, d_ref, y_ref, hb_scr):
    j = pl.program_id(1)
    cw = SSM_DIM

    @pl.when(j == 0)
    def _():
        hb_scr[...] = hin_ref[...].astype(BF16)

    y_ref[...] = _dot_nt(hb_scr[...], wct_ref[0]) + d_ref[0] * uf_ref[...]
    for s in range(SSM_CHUNK):
        @pl.when(s <= j)
        def _():
            y_ref[...] += _dot(ub_ref[:, s * cw:(s + 1) * cw], bd_ref[0, j - s])


def _ssm_output(uflat_b, hin, uflat_f, bd, wct, d_row, layer, *, tm):
    m, k = uflat_b.shape
    ks = hin.shape[1]
    cw = SSM_DIM
    return pl.pallas_call(
        _ssm_output_kernel,
        grid=(m // tm, SSM_CHUNK),
        in_specs=[pl.BlockSpec((tm, k), lambda i, j: (i, 0)),
                  pl.BlockSpec((tm, ks), lambda i, j: (i, 0)),
                  pl.BlockSpec((tm, cw), lambda i, j: (i, j)),
                  pl.BlockSpec((1, SSM_CHUNK, cw, cw), lambda i, j: (layer, 0, 0, 0)),
                  pl.BlockSpec((1, cw, ks), lambda i, j: (layer, j, 0)),
                  pl.BlockSpec((1, 1, cw), lambda i, j: (layer, 0, 0))],
        out_specs=pl.BlockSpec((tm, cw), lambda i, j: (i, j)),
        out_shape=jax.ShapeDtypeStruct((m, k), F32),
        scratch_shapes=[pltpu.VMEM((tm, ks), BF16)],
        compiler_params=_params(("arbitrary", "arbitrary")),
        name="ssm_output",
    )(uflat_b, hin, uflat_f, bd, wct, d_row)


def _ssm_step_kernel(u_ref, hre_ref, him_ref, b1_ref, lam_ref, c1t_ref, d_ref,
                     y_ref, ore_ref, oim_ref):
    u = u_ref[...]
    lr = lam_ref[0, 0:1, :]
    li = lam_ref[0, 1:2, :]
    h_re = hre_ref[...]
    h_im = him_ref[...]
    bu = _dot(u, b1_ref[0], precision=HIGHEST)
    n_re = lr * h_re - li * h_im + bu[:, 0:STATE_W]
    n_im = lr * h_im + li * h_re + bu[:, STATE_W:]
    ore_ref[...] = n_re
    oim_ref[...] = n_im
    y = (_dot_nt(n_re, c1t_ref[0, :, 0:STATE_W], precision=HIGHEST)
         + _dot_nt(n_im, c1t_ref[0, :, STATE_W:], precision=HIGHEST))
    y_ref[...] = y + d_ref[0] * u


def _ssm_step(u, h_re, h_im, b1, lam, c1t, d, layer):
    n = u.shape[0]
    full = lambda s: pl.BlockSpec(s, lambda i: (0,) * len(s))
    lay = lambda s: pl.BlockSpec((1,) + s[1:], lambda i: (layer,) + (0,) * (len(s) - 1))
    out_shapes = (jax.ShapeDtypeStruct((n, SSM_DIM), F32), jax.ShapeDtypeStruct((n, STATE_W), F32),
                  jax.ShapeDtypeStruct((n, STATE_W), F32))
    return pl.pallas_call(
        _ssm_step_kernel,
        grid=(1,),
        in_specs=[full(u.shape), full(h_re.shape), full(h_im.shape), lay(b1.shape), lay(lam.shape),
                  lay(c1t.shape), lay(d.shape)],
        out_specs=tuple(full(s.shape) for s in out_shapes),
        out_shape=out_shapes,
        compiler_params=_params(("arbitrary",)),
        name="ssm_step",
    )(u, h_re, h_im, b1, lam, c1t, d)


def _fox_prompt_kernel(qi_ref, ki_ref, q_ref, k_ref, v_ref, cq_ref, ck_ref, o_ref,
                       m_scr, l_scr, acc_scr, cq_scr, *, tq, tk, rb):
    qi = qi_ref[pl.program_id(2)]
    ki = ki_ref[pl.program_id(2)]
    reps = tk // LANES

    @pl.when(ki == 0)
    def _():
        m_scr[...] = jnp.full(m_scr.shape, -jnp.inf, F32)
        l_scr[...] = jnp.zeros_like(l_scr)
        acc_scr[...] = jnp.zeros_like(acc_scr)
        src_lane = lax.broadcasted_iota(jnp.int32, (LANES, 2 * LANES), 0)
        dst_head = lax.broadcasted_iota(jnp.int32, (LANES, 2 * LANES), 1) // LANES
        pick = (src_lane == 2 * pl.program_id(1) + dst_head).astype(F32).astype(BF16)
        hi, mid, lo = _split3(cq_ref[...])
        spread = (_dot(hi, pick) + _dot(mid, pick) + _dot(lo, pick)) * LOG2E
        for hh in range(2):
            cq_scr[hh] = spread[:, hh * LANES:(hh + 1) * LANES]

    def step(on_diagonal):
        k2 = k_ref[...]
        v2 = v_ref[...]
        lane = lax.broadcasted_iota(jnp.int32, (rb, LANES), 1)
        keep = [(lane < HEAD_DIM).astype(F32).astype(BF16), (lane >= HEAD_DIM).astype(F32).astype(BF16)]
        lane_o = lane < HEAD_DIM
        ck2 = [ck_ref[0, hh:hh + 1, :] * LOG2E for hh in range(2)]
        scores = [[_dot_nt(q_ref[r0:r0 + rb, :] * keep[hh], k2) for hh in range(2)]
                  for r0 in range(0, tq, rb)]
        for bi, r0 in enumerate(range(0, tq, rb)):
            rows = slice(r0, r0 + rb)
            if on_diagonal:
                rel = (lax.broadcasted_iota(jnp.int32, (rb, LANES), 1)
                       - lax.broadcasted_iota(jnp.int32, (rb, LANES), 0))
            alphas = []
            pvs = []
            for hh in range(2):
                sc = []
                for c in range(reps):
                    cols = slice(c * LANES, (c + 1) * LANES)
                    s_c = scores[bi][hh][:, cols] - ck2[hh][:, cols]
                    if on_diagonal:
                        s_c = jnp.where(rel <= r0 - c * LANES, s_c, -jnp.inf)
                    sc.append(s_c)
                mx = sc[0]
                for c in range(1, reps):
                    mx = jnp.maximum(mx, sc[c])
                cq = cq_scr[hh, rows, :]
                m_prev = m_scr[hh, rows, :]
                m_new = jnp.maximum(m_prev, jnp.max(mx, axis=-1, keepdims=True) + cq)
                t = m_new - cq
                ps = [jnp.exp2(s_c - t) for s_c in sc]
                tot = ps[0]
                for c in range(1, reps):
                    tot = tot + ps[c]
                alpha = jnp.exp2(m_prev - m_new)
                l_scr[hh, rows, :] = alpha * l_scr[hh, rows, :] + jnp.sum(tot, axis=-1, keepdims=True)
                m_scr[hh, rows, :] = m_new
                alphas.append(alpha)
                p = jnp.concatenate([p_c.astype(BF16) for p_c in ps], axis=1)
                pvs.append(_dot(p, v2))
            acc_scr[rows, :] = (jnp.where(lane_o, alphas[0], alphas[1]) * acc_scr[rows, :]
                                + jnp.where(lane_o, pvs[0], pvs[1]))

    @pl.when(ki < qi)
    def _():
        step(False)

    @pl.when(ki == qi)
    def _():
        step(True)
        lane_o = lax.broadcasted_iota(jnp.int32, (tq, LANES), 1) < HEAD_DIM
        o_ref[...] = (acc_scr[...] / jnp.where(lane_o, l_scr[0], l_scr[1])).astype(o_ref.dtype)


def _fox_prompt(q, kb, vb, ct, cq, *, seq_len, tq):
    n = q.shape[0]
    nseq = n // seq_len
    nq = seq_len // tq
    hp = N_HEADS // 2
    ck = ct.reshape(hp, 2, n)
    pairs = [(i, j) for i in range(nq) for j in range(i + 1)]
    qi_tab = jnp.asarray([p[0] for p in pairs], jnp.int32)
    ki_tab = jnp.asarray([p[1] for p in pairs], jnp.int32)
    q_idx = lambda b, h, t, qt, kt: (b * nq + qt[t], h)
    kv_idx = lambda b, h, t, qt, kt: (b * nq + kt[t], h)
    grid_spec = pltpu.PrefetchScalarGridSpec(
        num_scalar_prefetch=2,
        grid=(nseq, hp, len(pairs)),
        in_specs=[pl.BlockSpec((tq, LANES), q_idx),
                  pl.BlockSpec((tq, LANES), kv_idx),
                  pl.BlockSpec((tq, LANES), kv_idx),
                  pl.BlockSpec((tq, LANES), lambda b, h, t, qt, kt: (b * nq + qt[t], 0)),
                  pl.BlockSpec((1, 2, tq), lambda b, h, t, qt, kt: (h, 0, b * nq + kt[t]))],
        out_specs=pl.BlockSpec((tq, LANES), q_idx),
        scratch_shapes=[pltpu.VMEM((2, tq, LANES), F32), pltpu.VMEM((2, tq, LANES), F32),
                        pltpu.VMEM((tq, LANES), F32), pltpu.VMEM((2, tq, LANES), F32)],
    )
    return pl.pallas_call(
        functools.partial(_fox_prompt_kernel, tq=tq, tk=tq, rb=min(256, tq)),
        grid_spec=grid_spec,
        out_shape=jax.ShapeDtypeStruct((n, ATT_DIM), BF16),
        compiler_params=_params(("arbitrary", "arbitrary", "arbitrary")),
        name="fox_prompt",
    )(qi_tab, ki_tab, q, kb, vb, cq, ck)


PAGES_PER_STEP = 8


def _fox_sample_kernel(pt_ref, q_ref, vn_ref, sn_ref, lfn_ref, *refs, page_size):
    np_ = PAGES_PER_STEP
    k_refs = refs[0:np_]
    v_refs = refs[np_:2 * np_]
    lf_refs = refs[2 * np_:3 * np_]
    o_ref = refs[3 * np_]
    m_scr, l_scr, car_scr, acc_scr = refs[3 * np_ + 1:]
    j = pl.program_id(1)
    nj = pl.num_programs(1)

    eye = (lax.broadcasted_iota(jnp.int32, (N_HEADS, LANES), 0)
           == lax.broadcasted_iota(jnp.int32, (N_HEADS, LANES), 1)).astype(F32)

    def to_col(row):
        return jnp.sum(eye * row, axis=-1, keepdims=True)

    @pl.when(j == 0)
    def _():
        m_scr[...] = to_col(sn_ref[0])
        l_scr[...] = jnp.ones_like(l_scr)
        car_scr[...] = to_col(lfn_ref[0])
        lane0 = lax.broadcasted_iota(jnp.int32, (HEAD_DIM, page_size), 1) == 0
        for h in range(N_HEADS):
            vcol = vn_ref[0, h * HEAD_DIM:(h + 1) * HEAD_DIM, :]
            acc_scr[h] = jnp.where(lane0, vcol, 0.0)

    later = (lax.broadcasted_iota(jnp.int32, (page_size, page_size), 0)
             > lax.broadcasted_iota(jnp.int32, (page_size, page_size), 1)).astype(F32)
    fold = (lax.broadcasted_iota(jnp.int32, (N_HEADS, N_HEADS * 8), 1) // 8
            == lax.broadcasted_iota(jnp.int32, (N_HEADS, N_HEADS * 8), 0)).astype(F32)
    partial_sums = []
    for r in range(np_):
        parts = []
        for h in range(N_HEADS):
            prod = k_refs[r][0, 0, h] * q_ref[0, h * HEAD_DIM:(h + 1) * HEAD_DIM, :]
            t = prod[0:8]
            for a in range(1, HEAD_DIM // 8):
                t = t + prod[a * 8:(a + 1) * 8]
            parts.append(t)
        partial_sums.append(jnp.concatenate(parts, axis=0))
    s_qk = _dot(fold, jnp.concatenate(partial_sums, axis=1), precision=HIGHEST)
    lf_all = jnp.concatenate([lf_refs[r][0, 0] for r in range(np_)], axis=0)
    within = _dot(lf_all, later, precision=HIGHEST)
    page_tot = jnp.sum(lf_all, axis=-1, keepdims=True)
    carry = car_scr[...]
    bias = []
    for r in range(np_):
        bias.append(within[r * N_HEADS:(r + 1) * N_HEADS] + carry)
        carry = carry + page_tot[r * N_HEADS:(r + 1) * N_HEADS]
    car_scr[...] = carry
    s_all = s_qk + jnp.concatenate(bias, axis=1)
    m_prev = m_scr[...]
    m_new = jnp.maximum(m_prev, jnp.max(s_all, axis=-1, keepdims=True))
    alpha = jnp.exp(m_prev - m_new)
    p_all = jnp.exp(s_all - m_new)
    l_scr[...] = alpha * l_scr[...] + jnp.sum(p_all, axis=-1, keepdims=True)
    m_scr[...] = m_new
    for h in range(N_HEADS):
        a = acc_scr[h] * alpha[h:h + 1, :]
        for r in range(np_):
            a = a + p_all[h:h + 1, r * page_size:(r + 1) * page_size] * v_refs[r][0, 0, h]
        acc_scr[h] = a

    @pl.when(j == nj - 1)
    def _():
        inv = 1.0 / l_scr[...]
        for h in range(N_HEADS):
            o_ref[0, h] = jnp.sum(acc_scr[h], axis=-1, keepdims=True) * inv[h:h + 1, :]


def _fox_sample(page_table, q, v_new, s_new, lf_new, cache_kt, cache_vt, cache_lft, *, layer):
    nseq, n_pages = page_table.shape
    page_size = cache_kt.shape[-1]
    np_ = PAGES_PER_STEP
    nj = n_pages // np_

    def page_idx(r, nd):
        def f(b, j, pt):
            return (layer, pt[b * n_pages + (n_pages - 1 - (j * np_ + r))]) + (0,) * nd
        return f

    col = pl.BlockSpec((1, ATT_DIM, 1), lambda b, j, pt: (b, 0, 0))
    tok = pl.BlockSpec((1, 1, LANES), lambda b, j, pt: (b, 0, 0))
    kv_block = (1, 1, N_HEADS, HEAD_DIM, page_size)
    in_specs = [col, col, tok, tok]
    in_specs += [pl.BlockSpec(kv_block, page_idx(r, 3)) for r in range(np_)]
    in_specs += [pl.BlockSpec(kv_block, page_idx(r, 3)) for r in range(np_)]
    in_specs += [pl.BlockSpec((1, 1, N_HEADS, page_size), page_idx(r, 2)) for r in range(np_)]
    grid_spec = pltpu.PrefetchScalarGridSpec(
        num_scalar_prefetch=1,
        grid=(nseq, nj),
        in_specs=in_specs,
        out_specs=pl.BlockSpec((1, N_HEADS, HEAD_DIM, 1), lambda b, j, pt: (b, 0, 0, 0)),
        scratch_shapes=[pltpu.VMEM((N_HEADS, 1), F32), pltpu.VMEM((N_HEADS, 1), F32),
                        pltpu.VMEM((N_HEADS, 1), F32), pltpu.VMEM((N_HEADS, HEAD_DIM, page_size), F32)],
    )
    args = [q.reshape(nseq, ATT_DIM, 1), v_new.reshape(nseq, ATT_DIM, 1),
            s_new.reshape(nseq, 1, LANES), lf_new.reshape(nseq, 1, LANES)]
    args += [cache_kt] * np_ + [cache_vt] * np_ + [cache_lft] * np_
    out = pl.pallas_call(
        functools.partial(_fox_sample_kernel, page_size=page_size),
        grid_spec=grid_spec,
        out_shape=jax.ShapeDtypeStruct((nseq, N_HEADS, HEAD_DIM, 1), F32),
        compiler_params=_params(("arbitrary", "arbitrary")),
        name="fox_sample",
    )(page_table.reshape(-1), *args)
    return out.reshape(nseq, ATT_DIM).astype(BF16)


def _post_kernel(x_ref, convo_ref, ssmy_ref, att_ref, p_ref, wglu_ref, wout_ref, gpost_ref, gfpre_ref,
                 wgate_ref, wup_ref, wdown_ref, gfpost_ref, wpg_ref, wpp_ref, o_ref):
    y = _gelu_tanh(ssmy_ref[...])
    ssm_out = y * _sigmoid(_dot(y.astype(BF16), wglu_ref[0]))
    mix = (_dot(convo_ref[...], wout_ref[0, 0:CONV_DIM, :])
           + _dot(ssm_out.astype(BF16), wout_ref[0, CONV_DIM:CONV_DIM + SSM_DIM, :])
           + _dot(att_ref[...], wout_ref[0, CONV_DIM + SSM_DIM:, :]))
    x1 = x_ref[...] + _rms(mix, gpost_ref[0])
    h2 = _rms(x1, gfpre_ref[0]).astype(BF16)
    ffn = jnp.zeros(x1.shape, F32)
    for c in range(D_FF // FF_CHUNK):
        cols = slice(c * FF_CHUNK, (c + 1) * FF_CHUNK)
        gate = _dot(h2, wgate_ref[0, :, cols])
        up = _dot(h2, wup_ref[0, :, cols])
        act = (gate * _sigmoid(gate) * up).astype(BF16)
        ffn = ffn + _dot(act, wdown_ref[0, cols, :])
    x2 = x1 + _rms(ffn, gfpost_ref[0])
    pgate = _sigmoid(_dot(x2.astype(BF16), wpg_ref[0]))
    o_ref[...] = x2 + pgate * _dot(p_ref[0].astype(BF16), wpp_ref[0])


def _post(x, convo, ssmy, att, pemb, wglu, wout, gpost, gfpre, wgate, wup, wdown, gfpost, wpg, wpp, layer, *, tm):
    n = x.shape[0]
    row = lambda w: pl.BlockSpec((tm, w), lambda i: (i, 0))
    weights = (wglu, wout, gpost, gfpre, wgate, wup, wdown, gfpost, wpg, wpp)
    return pl.pallas_call(
        _post_kernel,
        grid=(n // tm,),
        in_specs=[row(D_MODEL), row(CONV_DIM), row(SSM_DIM), row(ATT_DIM),
                  pl.BlockSpec((1, tm, PLE_DIM), lambda i: (layer, i, 0))]
                 + [_layer_spec(w, layer) for w in weights],
        out_specs=row(D_MODEL),
        out_shape=jax.ShapeDtypeStruct((n, D_MODEL), F32),
        compiler_params=_params(("arbitrary",)),
        name="post_mixer",
    )(x, convo, ssmy, att, pemb, *weights)


def kernel(x_prompt, x_sample, cache_k, cache_v, cache_logf, state_conv, state_ssm_re, state_ssm_im, page_table, p_prompt, p_sample, norm_mix_pre, norm_mix_post, norm_ffn_pre, norm_ffn_post, w_in, b_forget, conv_w, ssm_a_re, ssm_a_im, ssm_log_dt, ssm_b_re, ssm_b_im, ssm_c_re, ssm_c_im, ssm_d, w_ssm_glu, w_out, w_ffn_gate, w_ffn_up, w_ffn_down, w_ple_gate, w_ple_proj):
    depth = w_in.shape[0]
    bp, seq_len, _ = x_prompt.shape
    bs = x_sample.shape[0]
    n_p = bp * seq_len
    tm = min(512, seq_len)
    tq = min(512, seq_len)
    chunks_per_seq = seq_len // SSM_CHUNK
    n_chunks = n_p // SSM_CHUNK
    tmc = min(512, n_chunks)

    gate_cols = 3 * CONV_DIM + SSM_DIM
    wg = w_in[:, :, 0:gate_cols].astype(BF16)
    wqkv = w_in[:, :, gate_cols:gate_cols + 3 * ATT_DIM].astype(BF16)
    wfl = jnp.pad(w_in[:, :, gate_cols + 3 * ATT_DIM:], ((0, 0), (0, 0), (0, LANES - N_HEADS))).astype(BF16)
    bfl = jnp.pad(b_forget, ((0, 0), (0, LANES - N_HEADS)))[:, None, :]
    wglu = w_ssm_glu.astype(BF16)
    wout = w_out.astype(BF16)
    wgate = w_ffn_gate.astype(BF16)
    wup = w_ffn_up.astype(BF16)
    wdown = w_ffn_down.astype(BF16)
    wpg = w_ple_gate.astype(BF16)
    wpp = w_ple_proj.astype(BF16)
    g_pre = norm_mix_pre[:, None, :]
    g_post = norm_mix_post[:, None, :]
    g_fpre = norm_ffn_pre[:, None, :]
    g_fpost = norm_ffn_post[:, None, :]
    d_row = ssm_d[:, None, :]

    w_g, w_ct, bd, b1, c1t, lam = _ssm_prep(ssm_a_re, ssm_a_im, ssm_log_dt, ssm_b_re, ssm_b_im,
                                            ssm_c_re, ssm_c_im)

    cache_kt = jnp.transpose(cache_k, (0, 1, 3, 4, 2))
    cache_vt = jnp.transpose(cache_v, (0, 1, 3, 4, 2))
    cache_lft = jnp.transpose(cache_logf, (0, 1, 3, 2))

    xp = x_prompt.reshape(n_p, D_MODEL)
    xs = x_sample.reshape(bs, D_MODEL)
    pe_prompt = p_prompt.reshape(depth, n_p, PLE_DIM)
    pe_sample = p_sample.reshape(depth, bs, PLE_DIM)
    outs_p = [[] for _ in range(6)]
    outs_s = [[] for _ in range(6)]
    for i in range(depth):
        (convo, uflat_f, uflat_b, q, k, v, kb, vb, lf, ct, crow, conv_new) = _inproj_prompt(
            xp, g_pre, wg, wqkv, wfl, bfl, conv_w, i, seq_len=seq_len, tm=tm)
        gstate = _ssm_chunk_state(uflat_b, w_g, i, tm=tmc, tn=1024)
        hin, hlast = _ssm_scan(gstate, lam, i, chunks_per_seq=chunks_per_seq)
        ssmy = _ssm_output(uflat_b, hin, uflat_f, bd, w_ct, d_row, i, tm=tmc)
        att = _fox_prompt(q, kb, vb, ct, crow, seq_len=seq_len, tq=tq)
        xp = _post(xp, convo, ssmy.reshape(n_p, SSM_DIM), att, pe_prompt,
                   wglu, wout, g_post, g_fpre, wgate, wup, wdown, g_fpost, wpg, wpp, i, tm=tm)
        outs_p[0].append(k.reshape(bp, seq_len, N_HEADS, HEAD_DIM))
        outs_p[1].append(v.reshape(bp, seq_len, N_HEADS, HEAD_DIM))
        outs_p[2].append(lf.reshape(bp, seq_len, N_HEADS))
        outs_p[3].append(conv_new)
        hl = hlast.reshape(bp, 2, SSM_GROUPS, SSM_STATE)
        outs_p[4].append(hl[:, 0])
        outs_p[5].append(hl[:, 1])

        (convo_s, su_s, u_s, q_s, k_s, v_s, lf_s, sn_s) = _inproj_sample(
            xs, g_pre, wg, wqkv, wfl, bfl, conv_w, state_conv[i, :, 0], state_conv[i, :, 1], i)
        ssmy_s, hre_s, him_s = _ssm_step(
            su_s, state_ssm_re[i].reshape(bs, STATE_W), state_ssm_im[i].reshape(bs, STATE_W),
            b1, lam, c1t, d_row, i)
        att_s = _fox_sample(page_table, q_s, v_s, sn_s, lf_s, cache_kt, cache_vt, cache_lft, layer=i)
        xs = _post(xs, convo_s, ssmy_s, att_s, pe_sample,
                   wglu, wout, g_post, g_fpre, wgate, wup, wdown, g_fpost, wpg, wpp, i, tm=bs)
        outs_s[0].append(k_s.reshape(bs, 1, N_HEADS, HEAD_DIM))
        outs_s[1].append(v_s.reshape(bs, 1, N_HEADS, HEAD_DIM))
        outs_s[2].append(lf_s[:, 0:N_HEADS].reshape(bs, 1, N_HEADS))
        outs_s[3].append(jnp.stack([state_conv[i, :, 1], u_s], axis=1))
        outs_s[4].append(hre_s.reshape(bs, SSM_GROUPS, SSM_STATE))
        outs_s[5].append(him_s.reshape(bs, SSM_GROUPS, SSM_STATE))

    k_p, v_p, lf_p, conv_p, re_p, im_p = [jnp.stack(a) for a in outs_p]
    k_s, v_s, lf_s, conv_s, re_s, im_s = [jnp.stack(a) for a in outs_s]
    return (xp.reshape(bp, seq_len, D_MODEL), xs.reshape(bs, 1, D_MODEL),
            k_p, v_p, lf_p, conv_p, re_p, im_p, k_s, v_s, lf_s, conv_s, re_s, im_s)
```

```python
import functools
import math

import jax
import jax.numpy as jnp
from jax import lax
from jax.experimental import pallas as pl
from jax.experimental.pallas import tpu as pltpu

F32 = jnp.float32
BF16 = jnp.bfloat16
HIGHEST = lax.Precision.HIGHEST

D_MODEL = 1024
CONV_DIM = 256
SSM_DIM = 256
ATT_DIM = 512
N_HEADS = 8
HEAD_DIM = 64
SSM_GROUPS = 16
SSM_GROUP = 16
SSM_STATE = 64
D_FF = 2816
PLE_DIM = 256
CONV_K = 3
EPS = 1e-6
LOG2E = math.log2(math.e)
LANES = 128
SSM_CHUNK = 16
FF_CHUNK = 256
STATE_W = SSM_GROUPS * SSM_STATE
CHUNK_W = SSM_CHUNK * SSM_DIM
VMEM_LIMIT = 60 * 1024 * 1024


def _params(sem, vmem=VMEM_LIMIT):
    return pltpu.CompilerParams(dimension_semantics=sem, vmem_limit_bytes=vmem)


def _rms(x, g):
    return x * lax.rsqrt(jnp.mean(x * x, axis=-1, keepdims=True) + EPS) * g


def _sigmoid(x):
    return 1.0 / (1.0 + jnp.exp(-x))


def _log_sigmoid(x):
    return jnp.minimum(x, 0.0) - jnp.log1p(jnp.exp(-jnp.abs(x)))


def _gelu_tanh(x):
    return 0.5 * x * (1.0 + jnp.tanh(math.sqrt(2.0 / math.pi) * (x + 0.044715 * (x * x * x))))


def _dot(a, b, **kw):
    return jnp.dot(a, b, preferred_element_type=F32, **kw)


def _dot_nt(a, b, **kw):
    return lax.dot_general(a, b, (((1,), (1,)), ((), ())), preferred_element_type=F32, **kw)


def _split3(x):
    hi = x.astype(BF16)
    r1 = x - hi.astype(F32)
    mid = r1.astype(BF16)
    lo = (r1 - mid.astype(F32)).astype(BF16)
    return hi, mid, lo


def _layer_spec(arr, layer):
    zeros = (0,) * (arr.ndim - 1)
    return pl.BlockSpec((1,) + arr.shape[1:], lambda *_: (layer,) + zeros, pipeline_mode=pl.Buffered(1))


def _ssm_prep_kernel(are_ref, aim_ref, ldt_ref, btr_ref, bti_ref, ctr_ref, cti_ref,
                     wg_ref, wct_ref, bd_ref, b1_ref, c1t_ref, lam_ref, pw_scr):
    t = pl.program_id(1)
    a_re = are_ref[0]
    a_im = aim_ref[0]
    dt = jnp.exp(ldt_ref[0])
    mag = jnp.exp(a_re * dt)
    lr = mag * jnp.cos(a_im * dt)
    li = mag * jnp.sin(a_im * dt)
    den = a_re * a_re + a_im * a_im
    xr = lr - 1.0
    cfr = (xr * a_re + li * a_im) / den
    cfi = (li * a_re - xr * a_im) / den
    btr = btr_ref[0]
    bti = bti_ref[0]
    bbr = cfr * btr - cfi * bti
    bbi = cfr * bti + cfi * btr
    ctr = ctr_ref[0]
    cti = cti_ref[0]

    rows = SSM_GROUPS * SSM_GROUP
    grp_r = lax.broadcasted_iota(jnp.int32, (rows, STATE_W), 0) // SSM_GROUP
    grp_c = lax.broadcasted_iota(jnp.int32, (rows, STATE_W), 1) // SSM_STATE
    mask = (grp_r == grp_c).astype(F32)

    def blockdiag(x):
        return jnp.concatenate([x] * SSM_GROUPS, axis=0) * mask

    @pl.when(t == 0)
    def _():
        pw_scr[0:1, :] = jnp.ones((1, STATE_W), F32)
        pw_scr[1:2, :] = jnp.zeros((1, STATE_W), F32)
        lam_ref[0] = jnp.zeros((8, STATE_W), F32)
        lam_ref[0, 0:1, :] = lr
        lam_ref[0, 1:2, :] = li
        b1_ref[0, :, 0:STATE_W] = blockdiag(bbr)
        b1_ref[0, :, STATE_W:] = blockdiag(bbi)
        c1t_ref[0, :, 0:STATE_W] = blockdiag(ctr)
        c1t_ref[0, :, STATE_W:] = blockdiag(-cti)

    pr = pw_scr[0:1, :]
    pi = pw_scr[1:2, :]
    nr = pr * lr - pi * li
    ni = pr * li + pi * lr
    pw_scr[0:1, :] = nr
    pw_scr[1:2, :] = ni

    @pl.when(t == SSM_CHUNK - 1)
    def _():
        lam_ref[0, 2:3, :] = nr
        lam_ref[0, 3:4, :] = ni

    wg_ref[0, :, 0:STATE_W] = blockdiag(bbr * pr - bbi * pi).astype(BF16)
    wg_ref[0, :, STATE_W:] = blockdiag(bbr * pi + bbi * pr).astype(BF16)
    wct_ref[0, :, 0:STATE_W] = blockdiag(ctr * nr - cti * ni).astype(BF16)
    wct_ref[0, :, STATE_W:] = blockdiag(-(ctr * ni + cti * nr)).astype(BF16)
    pb = jnp.concatenate([blockdiag(bbr), blockdiag(bbi)], axis=1)
    pa = jnp.concatenate([blockdiag(ctr * pr - cti * pi), blockdiag(-(ctr * pi + cti * pr))], axis=1)
    bd_ref[0, 0] = _dot_nt(pb, pa, precision=HIGHEST).astype(BF16)


def _ssm_prep(a_re, a_im, log_dt, b_re, b_im, c_re, c_im):
    depth = a_re.shape[0]
    g, p, c, l = SSM_GROUPS, SSM_STATE, SSM_GROUP, SSM_CHUNK
    rows = g * c
    row1 = lambda x: x.reshape(depth, 1, g * p)
    chan = lambda x: x.reshape(depth, c, g * p)
    args = (row1(a_re), row1(a_im), row1(jnp.repeat(log_dt, p, axis=-1)),
            chan(jnp.transpose(b_re, (0, 3, 1, 2))), chan(jnp.transpose(b_im, (0, 3, 1, 2))),
            chan(jnp.transpose(c_re, (0, 2, 1, 3))), chan(jnp.transpose(c_im, (0, 2, 1, 3))))
    spec_row = pl.BlockSpec((1, 1, g * p), lambda i, t: (i, 0, 0))
    spec_chan = pl.BlockSpec((1, c, g * p), lambda i, t: (i, 0, 0))
    out_shapes = (
        jax.ShapeDtypeStruct((depth, l * rows, 2 * g * p), BF16),
        jax.ShapeDtypeStruct((depth, l * rows, 2 * g * p), BF16),
        jax.ShapeDtypeStruct((depth, l, rows, rows), BF16),
        jax.ShapeDtypeStruct((depth, rows, 2 * g * p), F32),
        jax.ShapeDtypeStruct((depth, rows, 2 * g * p), F32),
        jax.ShapeDtypeStruct((depth, 8, g * p), F32),
    )
    out_specs = (
        pl.BlockSpec((1, rows, 2 * g * p), lambda i, t: (i, l - 1 - t, 0)),
        pl.BlockSpec((1, rows, 2 * g * p), lambda i, t: (i, t, 0)),
        pl.BlockSpec((1, 1, rows, rows), lambda i, t: (i, t, 0, 0)),
        pl.BlockSpec((1, rows, 2 * g * p), lambda i, t: (i, 0, 0)),
        pl.BlockSpec((1, rows, 2 * g * p), lambda i, t: (i, 0, 0)),
        pl.BlockSpec((1, 8, g * p), lambda i, t: (i, 0, 0)),
    )
    return pl.pallas_call(
        _ssm_prep_kernel,
        grid=(depth, l),
        in_specs=[spec_row, spec_row, spec_row, spec_chan, spec_chan, spec_chan, spec_chan],
        out_specs=out_specs,
        out_shape=out_shapes,
        scratch_shapes=[pltpu.VMEM((2, g * p), F32)],
        compiler_params=_params(("arbitrary", "arbitrary")),
        name="ssm_prep",
    )(*args)


def _inproj_prompt_kernel(x_ref, g_ref, wg_ref, wqkv_ref, wfl_ref, bfl_ref, cw_ref,
                          convo_ref, su_ref, sub_ref, q_ref, k_ref, v_ref, kb_ref, vb_ref,
                          lf_ref, ct_ref, crow_ref, convnew_ref, ubuf, ccar, su_lo, su_hi, *, tiles_per_seq, tm):
    i = pl.program_id(0)

    @pl.when(i % tiles_per_seq == 0)
    def _():
        ubuf[0:8, :] = jnp.zeros((8, CONV_DIM), F32)
        ccar[...] = jnp.zeros_like(ccar)

    h = _rms(x_ref[...], g_ref[0]).astype(BF16)
    z = _dot(h, wg_ref[0])
    cb = z[:, 0:CONV_DIM]
    cc = z[:, CONV_DIM:2 * CONV_DIM]
    cv = z[:, 2 * CONV_DIM:3 * CONV_DIM]
    for half, scr in enumerate((su_lo, su_hi)):
        scr[...] = z[:, 3 * CONV_DIM + half * LANES:3 * CONV_DIM + (half + 1) * LANES]
        for s in range(SSM_CHUNK):
            piece = scr[pl.ds(s, tm // SSM_CHUNK, stride=SSM_CHUNK), :]
            cols = slice(s * SSM_DIM + half * LANES, s * SSM_DIM + (half + 1) * LANES)
            su_ref[:, cols] = piece
            sub_ref[:, cols] = piece.astype(BF16)
    u = cc * cv
    ubuf[8:8 + tm, :] = u
    u1 = ubuf[7:7 + tm, :]
    u2 = ubuf[6:6 + tm, :]
    cw = cw_ref[0]
    y = cw[0:1] * u2 + cw[1:2] * u1 + cw[2:3] * u
    convo_ref[...] = (cb * y).astype(BF16)
    convnew_ref[0] = u[tm - 2:tm, :]
    ubuf[0:8, :] = u[tm - 8:tm, :]

    zz = _dot(h, wqkv_ref[0])
    q_ref[...] = (zz[:, 0:ATT_DIM] * (HEAD_DIM ** -0.5 * LOG2E)).astype(BF16)
    k = zz[:, ATT_DIM:2 * ATT_DIM]
    v = zz[:, 2 * ATT_DIM:]
    k_ref[...] = k
    v_ref[...] = v
    kb_ref[...] = k.astype(BF16)
    vb_ref[...] = v.astype(BF16)

    fl = _dot(h, wfl_ref[0]) + bfl_ref[0]
    lane = lax.broadcasted_iota(jnp.int32, fl.shape, 1)
    lf = jnp.where(lane < N_HEADS, _log_sigmoid(fl), 0.0)
    lf_ref[...] = lf[:, 0:N_HEADS]
    row = lax.broadcasted_iota(jnp.int32, (LANES, LANES), 0)
    col = lax.broadcasted_iota(jnp.int32, (LANES, LANES), 1)
    tri = (row >= col).astype(F32).astype(BF16)
    carry = ccar[...]
    for r0 in range(0, tm, LANES):
        hi, mid, lo = _split3(lf[r0:r0 + LANES, :])
        c = _dot(tri, hi) + _dot(tri, mid) + _dot(tri, lo) + carry
        carry = c[LANES - 1:LANES, :]
        crow_ref[r0:r0 + LANES, :] = c
        ct_ref[:, r0:r0 + LANES] = c.T[0:N_HEADS, :]
    ccar[...] = carry


def _inproj_prompt(x, g_pre, wg, wqkv, wfl, bfl, conv_w, layer, *, seq_len, tm):
    n = x.shape[0]
    nt = n // tm
    tiles_per_seq = seq_len // tm
    nseq = n // seq_len
    tc = tm // SSM_CHUNK
    row = lambda w: pl.BlockSpec((tm, w), lambda i: (i, 0))
    chunk_rows = pl.BlockSpec((tc, CHUNK_W), lambda i: (i, 0))
    out_shapes = (
        jax.ShapeDtypeStruct((n, CONV_DIM), BF16),
        jax.ShapeDtypeStruct((n // SSM_CHUNK, CHUNK_W), F32),
        jax.ShapeDtypeStruct((n // SSM_CHUNK, CHUNK_W), BF16),
        jax.ShapeDtypeStruct((n, ATT_DIM), BF16),
        jax.ShapeDtypeStruct((n, ATT_DIM), F32),
        jax.ShapeDtypeStruct((n, ATT_DIM), F32),
        jax.ShapeDtypeStruct((n, ATT_DIM), BF16),
        jax.ShapeDtypeStruct((n, ATT_DIM), BF16),
        jax.ShapeDtypeStruct((n, N_HEADS), F32),
        jax.ShapeDtypeStruct((N_HEADS, n), F32),
        jax.ShapeDtypeStruct((n, LANES), F32),
        jax.ShapeDtypeStruct((nseq, CONV_K - 1, CONV_DIM), F32),
    )
    out_specs = (
        row(CONV_DIM), chunk_rows, chunk_rows, row(ATT_DIM), row(ATT_DIM), row(ATT_DIM),
        row(ATT_DIM), row(ATT_DIM), row(N_HEADS),
        pl.BlockSpec((N_HEADS, tm), lambda i: (0, i)),
        row(LANES),
        pl.BlockSpec((1, CONV_K - 1, CONV_DIM), lambda i: (i // tiles_per_seq, 0, 0)),
    )
    weights = (g_pre, wg, wqkv, wfl, bfl, conv_w)
    return pl.pallas_call(
        functools.partial(_inproj_prompt_kernel, tiles_per_seq=tiles_per_seq, tm=tm),
        grid=(nt,),
        in_specs=[row(D_MODEL)] + [_layer_spec(w, layer) for w in weights],
        out_specs=out_specs,
        out_shape=out_shapes,
        scratch_shapes=[pltpu.VMEM((tm + 8, CONV_DIM), F32), pltpu.VMEM((1, LANES), F32),
                        pltpu.VMEM((tm, LANES), F32), pltpu.VMEM((tm, LANES), F32)],
        compiler_params=_params(("arbitrary",)),
        name="inproj_prompt",
    )(x, *weights)


def _inproj_sample_kernel(x_ref, g_ref, wg_ref, wqkv_ref, wfl_ref, bfl_ref, cw_ref, b0_ref, b1_ref,
                          convo_ref, su_ref, u_ref, q_ref, k_ref, v_ref, lf_ref, sn_ref):
    h = _rms(x_ref[...], g_ref[0]).astype(BF16)
    z = _dot(h, wg_ref[0])
    cb = z[:, 0:CONV_DIM]
    u = z[:, CONV_DIM:2 * CONV_DIM] * z[:, 2 * CONV_DIM:3 * CONV_DIM]
    su_ref[...] = z[:, 3 * CONV_DIM:]
    u_ref[...] = u
    cw = cw_ref[0]
    y = cw[0:1] * b0_ref[...] + cw[1:2] * b1_ref[...] + cw[2:3] * u
    convo_ref[...] = (cb * y).astype(BF16)
    zz = _dot(h, wqkv_ref[0])
    q = zz[:, 0:ATT_DIM] * (HEAD_DIM ** -0.5)
    k = zz[:, ATT_DIM:2 * ATT_DIM]
    q_ref[...] = q
    k_ref[...] = k
    v_ref[...] = zz[:, 2 * ATT_DIM:]
    fl = _dot(h, wfl_ref[0]) + bfl_ref[0]
    lane = lax.broadcasted_iota(jnp.int32, fl.shape, 1)
    lf_ref[...] = jnp.where(lane < N_HEADS, _log_sigmoid(fl), 0.0)
    hd_row = lax.broadcasted_iota(jnp.int32, (ATT_DIM, LANES), 0) // HEAD_DIM
    hd_col = lax.broadcasted_iota(jnp.int32, (ATT_DIM, LANES), 1)
    sn_ref[...] = _dot(q * k, (hd_row == hd_col).astype(F32), precision=HIGHEST)


def _inproj_sample(x, g_pre, wg, wqkv, wfl, bfl, conv_w, buf0, buf1, layer):
    n = x.shape[0]
    full = lambda s: pl.BlockSpec(s, lambda i: (0,) * len(s))
    weights = (g_pre, wg, wqkv, wfl, bfl, conv_w)
    out_shapes = (
        jax.ShapeDtypeStruct((n, CONV_DIM), BF16),
        jax.ShapeDtypeStruct((n, SSM_DIM), F32),
        jax.ShapeDtypeStruct((n, CONV_DIM), F32),
        jax.ShapeDtypeStruct((n, ATT_DIM), F32),
        jax.ShapeDtypeStruct((n, ATT_DIM), F32),
        jax.ShapeDtypeStruct((n, ATT_DIM), F32),
        jax.ShapeDtypeStruct((n, LANES), F32),
        jax.ShapeDtypeStruct((n, LANES), F32),
    )
    args = (x,) + weights + (buf0, buf1)
    return pl.pallas_call(
        _inproj_sample_kernel,
        grid=(1,),
        in_specs=[full(x.shape)] + [_layer_spec(w, layer) for w in weights] + [full(buf0.shape), full(buf1.shape)],
        out_specs=tuple(full(s.shape) for s in out_shapes),
        out_shape=out_shapes,
        compiler_params=_params(("arbitrary",)),
        name="inproj_sample",
    )(*args)


def _ssm_chunk_state_kernel(u_ref, w_ref, o_ref):
    o_ref[...] = _dot(u_ref[...], w_ref[0])


def _ssm_chunk_state(uflat, w_g, layer, *, tm, tn):
    m, k = uflat.shape
    n = w_g.shape[2]
    return pl.pallas_call(
        _ssm_chunk_state_kernel,
        grid=(n // tn, m // tm),
        in_specs=[pl.BlockSpec((tm, k), lambda j, i: (i, 0)),
                  pl.BlockSpec((1, k, tn), lambda j, i: (layer, 0, j))],
        out_specs=pl.BlockSpec((tm, tn), lambda j, i: (i, j)),
        out_shape=jax.ShapeDtypeStruct((m, n), F32),
        compiler_params=_params(("arbitrary", "arbitrary")),
        name="ssm_chunk_state",
    )(uflat, w_g)


def _ssm_scan_kernel(g_ref, lam_ref, hin_ref, hlast_ref, *, n_chunks):
    lr = lam_ref[0, 2:3, :]
    li = lam_ref[0, 3:4, :]

    def body(c, carry):
        h_re, h_im = carry
        hin_ref[pl.ds(c, 1), 0:STATE_W] = h_re
        hin_ref[pl.ds(c, 1), STATE_W:] = h_im
        g_re = g_ref[pl.ds(c, 1), 0:STATE_W]
        g_im = g_ref[pl.ds(c, 1), STATE_W:]
        return lr * h_re - li * h_im + g_re, lr * h_im + li * h_re + g_im

    zero = jnp.zeros((1, STATE_W), F32)
    h_re, h_im = lax.fori_loop(0, n_chunks, body, (zero, zero))
    hlast_ref[0, :, 0:STATE_W] = h_re
    hlast_ref[0, :, STATE_W:] = h_im


def _ssm_scan(gstate, lam, layer, *, chunks_per_seq):
    m, w = gstate.shape
    nseq = m // chunks_per_seq
    return pl.pallas_call(
        functools.partial(_ssm_scan_kernel, n_chunks=chunks_per_seq),
        grid=(nseq,),
        in_specs=[pl.BlockSpec((chunks_per_seq, w), lambda b: (b, 0)),
                  pl.BlockSpec((1, 8, STATE_W), lambda b: (layer, 0, 0))],
        out_specs=(pl.BlockSpec((chunks_per_seq, w), lambda b: (b, 0)),
                   pl.BlockSpec((1, 1, w), lambda b: (b, 0, 0))),
        out_shape=(jax.ShapeDtypeStruct((m, w), F32), jax.ShapeDtypeStruct((nseq, 1, w), F32)),
        compiler_params=_params(("arbitrary",)),
        name="ssm_scan",
    )(gstate, lam)


def _ssm_output_kernel(ub_ref, hin_ref, uf_ref, bd_ref, wct_ref, d_ref, y_ref, hb_scr):
    j = pl.program_id(1)
    cw = SSM_DIM

    @pl.when(j == 0)
    def _():
        hb_scr[...] = hin_ref[...].astype(BF16)

    y_ref[...] = _dot_nt(hb_scr[...], wct_ref[0]) + d_ref[0] * uf_ref[...]
    for s in range(SSM_CHUNK):
        @pl.when(s <= j)
        def _():
            y_ref[...] += _dot(ub_ref[:, s * cw:(s + 1) * cw], bd_ref[0, j - s])


def _ssm_output(uflat_b, hin, uflat_f, bd, wct, d_row, layer, *, tm):
    m, k = uflat_b.shape
    ks = hin.shape[1]
    cw = SSM_DIM
    return pl.pallas_call(
        _ssm_output_kernel,
        grid=(m // tm, SSM_CHUNK),
        in_specs=[pl.BlockSpec((tm, k), lambda i, j: (i, 0)),
                  pl.BlockSpec((tm, ks), lambda i, j: (i, 0)),
                  pl.BlockSpec((tm, cw), lambda i, j: (i, j)),
                  pl.BlockSpec((1, SSM_CHUNK, cw, cw), lambda i, j: (layer, 0, 0, 0)),
                  pl.BlockSpec((1, cw, ks), lambda i, j: (layer, j, 0)),
                  pl.BlockSpec((1, 1, cw), lambda i, j: (layer, 0, 0))],
        out_specs=pl.BlockSpec((tm, cw), lambda i, j: (i, j)),
        out_shape=jax.ShapeDtypeStruct((m, k), F32),
        scratch_shapes=[pltpu.VMEM((tm, ks), BF16)],
        compiler_params=_params(("arbitrary", "arbitrary")),
        name="ssm_output",
    )(uflat_b, hin, uflat_f, bd, wct, d_row)


def _ssm_step_kernel(u_ref, hre_ref, him_ref, b1_ref, lam_ref, c1t_ref, d_ref,
                     y_ref, ore_ref, oim_ref):
    u = u_ref[...]
    lr = lam_ref[0, 0:1, :]
    li = lam_ref[0, 1:2, :]
    h_re = hre_ref[...]
    h_im = him_ref[...]
    bu = _dot(u, b1_ref[0], precision=HIGHEST)
    n_re = lr * h_re - li * h_im + bu[:, 0:STATE_W]
    n_im = lr * h_im + li * h_re + bu[:, STATE_W:]
    ore_ref[...] = n_re
    oim_ref[...] = n_im
    y = (_dot_nt(n_re, c1t_ref[0, :, 0:STATE_W], precision=HIGHEST)
         + _dot_nt(n_im, c1t_ref[0, :, STATE_W:], precision=HIGHEST))
    y_ref[...] = y + d_ref[0] * u


def _ssm_step(u, h_re, h_im, b1, lam, c1t, d, layer):
    n = u.shape[0]
    full = lambda s: pl.BlockSpec(s, lambda i: (0,) * len(s))
    lay = lambda s: pl.BlockSpec((1,) + s[1:], lambda i: (layer,) + (0,) * (len(s) - 1))
    out_shapes = (jax.ShapeDtypeStruct((n, SSM_DIM), F32), jax.ShapeDtypeStruct((n, STATE_W), F32),
                  jax.ShapeDtypeStruct((n, STATE_W), F32))
    return pl.pallas_call(
        _ssm_step_kernel,
        grid=(1,),
        in_specs=[full(u.shape), full(h_re.shape), full(h_im.shape), lay(b1.shape), lay(lam.shape),
                  lay(c1t.shape), lay(d.shape)],
        out_specs=tuple(full(s.shape) for s in out_shapes),
        out_shape=out_shapes,
        compiler_params=_params(("arbitrary",)),
        name="ssm_step",
    )(u, h_re, h_im, b1, lam, c1t, d)


def _fox_prompt_kernel(qi_ref, ki_ref, q_ref, k_ref, v_ref, cq_ref, ck_ref, o_ref,
                       m_scr, l_scr, acc_scr, cq_scr, *, tq, tk, rb):
    qi = qi_ref[pl.program_id(2)]
    ki = ki_ref[pl.program_id(2)]
    reps = tk // LANES

    @pl.when(ki == 0)
    def _():
        m_scr[...] = jnp.full(m_scr.shape, -jnp.inf, F32)
        l_scr[...] = jnp.zeros_like(l_scr)
        acc_scr[...] = jnp.zeros_like(acc_scr)
        src_lane = lax.broadcasted_iota(jnp.int32, (LANES, 2 * LANES), 0)
        dst_head = lax.broadcasted_iota(jnp.int32, (LANES, 2 * LANES), 1) // LANES
        pick = (src_lane == 2 * pl.program_id(1) + dst_head).astype(F32).astype(BF16)
        hi, mid, lo = _split3(cq_ref[...])
        spread = (_dot(hi, pick) + _dot(mid, pick) + _dot(lo, pick)) * LOG2E
        for hh in range(2):
            cq_scr[hh] = spread[:, hh * LANES:(hh + 1) * LANES]

    def step(on_diagonal):
        k2 = k_ref[...]
        v2 = v_ref[...]
        lane = lax.broadcasted_iota(jnp.int32, (rb, LANES), 1)
        keep = [(lane < HEAD_DIM).astype(F32).astype(BF16), (lane >= HEAD_DIM).astype(F32).astype(BF16)]
        lane_o = lane < HEAD_DIM
        ck2 = [ck_ref[0, hh:hh + 1, :] * LOG2E for hh in range(2)]
        scores = [[_dot_nt(q_ref[r0:r0 + rb, :] * keep[hh], k2) for hh in range(2)]
                  for r0 in range(0, tq, rb)]
        for bi, r0 in enumerate(range(0, tq, rb)):
            rows = slice(r0, r0 + rb)
            if on_diagonal:
                rel = (lax.broadcasted_iota(jnp.int32, (rb, LANES), 1)
                       - lax.broadcasted_iota(jnp.int32, (rb, LANES), 0))
            alphas = []
            pvs = []
            for hh in range(2):
                sc = []
                for c in range(reps):
                    cols = slice(c * LANES, (c + 1) * LANES)
                    s_c = scores[bi][hh][:, cols] - ck2[hh][:, cols]
                    if on_diagonal:
                        s_c = jnp.where(rel <= r0 - c * LANES, s_c, -jnp.inf)
                    sc.append(s_c)
                mx = sc[0]
                for c in range(1, reps):
                    mx = jnp.maximum(mx, sc[c])
                cq = cq_scr[hh, rows, :]
                m_prev = m_scr[hh, rows, :]
                m_new = jnp.maximum(m_prev, jnp.max(mx, axis=-1, keepdims=True) + cq)
                t = m_new - cq
                ps = [jnp.exp2(s_c - t) for s_c in sc]
                tot = ps[0]
                for c in range(1, reps):
                    tot = tot + ps[c]
                alpha = jnp.exp2(m_prev - m_new)
                l_scr[hh, rows, :] = alpha * l_scr[hh, rows, :] + jnp.sum(tot, axis=-1, keepdims=True)
                m_scr[hh, rows, :] = m_new
                alphas.append(alpha)
                p = jnp.concatenate([p_c.astype(BF16) for p_c in ps], axis=1)
                pvs.append(_dot(p, v2))
            acc_scr[rows, :] = (jnp.where(lane_o, alphas[0], alphas[1]) * acc_scr[rows, :]
                                + jnp.where(lane_o, pvs[0], pvs[1]))

    @pl.when(ki < qi)
    def _():
        step(False)

    @pl.when(ki == qi)
    def _():
        step(True)
        lane_o = lax.broadcasted_iota(jnp.int32, (tq, LANES), 1) < HEAD_DIM
        o_ref[...] = (acc_scr[...] / jnp.where(lane_o, l_scr[0], l_scr[1])).astype(o_ref.dtype)


def _fox_prompt(q, kb, vb, ct, cq, *, seq_len, tq):
    n = q.shape[0]
    nseq = n // seq_len
    nq = seq_len // tq
    hp = N_HEADS // 2
    ck = ct.reshape(hp, 2, n)
    pairs = [(i, j) for i in range(nq) for j in range(i + 1)]
    qi_tab = jnp.asarray([p[0] for p in pairs], jnp.int32)
    ki_tab = jnp.asarray([p[1] for p in pairs], jnp.int32)
    q_idx = lambda b, h, t, qt, kt: (b * nq + qt[t], h)
    kv_idx = lambda b, h, t, qt, kt: (b * nq + kt[t], h)
    grid_spec = pltpu.PrefetchScalarGridSpec(
        num_scalar_prefetch=2,
        grid=(nseq, hp, len(pairs)),
        in_specs=[pl.BlockSpec((tq, LANES), q_idx),
                  pl.BlockSpec((tq, LANES), kv_idx),
                  pl.BlockSpec((tq, LANES), kv_idx),
                  pl.BlockSpec((tq, LANES), lambda b, h, t, qt, kt: (b * nq + qt[t], 0)),
                  pl.BlockSpec((1, 2, tq), lambda b, h, t, qt, kt: (h, 0, b * nq + kt[t]))],
        out_specs=pl.BlockSpec((tq, LANES), q_idx),
        scratch_shapes=[pltpu.VMEM((2, tq, LANES), F32), pltpu.VMEM((2, tq, LANES), F32),
                        pltpu.VMEM((tq, LANES), F32), pltpu.VMEM((2, tq, LANES), F32)],
    )
    return pl.pallas_call(
        functools.partial(_fox_prompt_kernel, tq=tq, tk=tq, rb=min(256, tq)),
        grid_spec=grid_spec,
        out_shape=jax.ShapeDtypeStruct((n, ATT_DIM), BF16),
        compiler_params=_params(("arbitrary", "arbitrary", "arbitrary")),
        name="fox_prompt",
    )(qi_tab, ki_tab, q, kb, vb, cq, ck)


PAGES_PER_STEP = 8


def _decode_init(vn_ref, sn_ref, lfn_ref, m_scr, l_scr, car_scr, acc_scr, page_size):
    eye = (lax.broadcasted_iota(jnp.int32, (N_HEADS, LANES), 0)
           == lax.broadcasted_iota(jnp.int32, (N_HEADS, LANES), 1)).astype(F32)

    def to_col(row):
        return jnp.sum(eye * row, axis=-1, keepdims=True)

    m_scr[...] = to_col(sn_ref[0])
    l_scr[...] = jnp.ones_like(l_scr)
    car_scr[...] = to_col(lfn_ref[0])
    lane0 = lax.broadcasted_iota(jnp.int32, (HEAD_DIM, page_size), 1) == 0
    for h in range(N_HEADS):
        vcol = vn_ref[0, h * HEAD_DIM:(h + 1) * HEAD_DIM, :]
        acc_scr[h] = jnp.where(lane0, vcol, 0.0)


def _decode_finish(o_ref, l_scr, acc_scr):
    inv = 1.0 / l_scr[...]
    for h in range(N_HEADS):
        o_ref[0, h] = jnp.sum(acc_scr[h], axis=-1, keepdims=True) * inv[h:h + 1, :]


def _decode_update(q_ref, k_refs, v_refs, lf_refs, m_scr, l_scr, car_scr, acc_scr, page_size):
    np_ = PAGES_PER_STEP
    later = (lax.broadcasted_iota(jnp.int32, (page_size, page_size), 0)
             > lax.broadcasted_iota(jnp.int32, (page_size, page_size), 1)).astype(F32)
    fold = (lax.broadcasted_iota(jnp.int32, (N_HEADS, N_HEADS * 8), 1) // 8
            == lax.broadcasted_iota(jnp.int32, (N_HEADS, N_HEADS * 8), 0)).astype(F32)
    partial_sums = []
    for r in range(np_):
        parts = []
        for h in range(N_HEADS):
            prod = k_refs[r][0, 0, h] * q_ref[0, h * HEAD_DIM:(h + 1) * HEAD_DIM, :]
            t = prod[0:8]
            for a in range(1, HEAD_DIM // 8):
                t = t + prod[a * 8:(a + 1) * 8]
            parts.append(t)
        partial_sums.append(jnp.concatenate(parts, axis=0))
    s_qk = _dot(fold, jnp.concatenate(partial_sums, axis=1), precision=HIGHEST)
    lf_all = jnp.concatenate([lf_refs[r][0, 0] for r in range(np_)], axis=0)
    within = _dot(lf_all, later, precision=HIGHEST)
    page_tot = jnp.sum(lf_all, axis=-1, keepdims=True)
    carry = car_scr[...]
    bias = []
    for r in range(np_):
        bias.append(within[r * N_HEADS:(r + 1) * N_HEADS] + carry)
        carry = carry + page_tot[r * N_HEADS:(r + 1) * N_HEADS]
    car_scr[...] = carry
    s_all = s_qk + jnp.concatenate(bias, axis=1)
    m_prev = m_scr[...]
    m_new = jnp.maximum(m_prev, jnp.max(s_all, axis=-1, keepdims=True))
    alpha = jnp.exp(m_prev - m_new)
    p_all = jnp.exp(s_all - m_new)
    l_scr[...] = alpha * l_scr[...] + jnp.sum(p_all, axis=-1, keepdims=True)
    m_scr[...] = m_new
    for h in range(N_HEADS):
        a = acc_scr[h] * alpha[h:h + 1, :]
        for r in range(np_):
            a = a + p_all[h:h + 1, r * page_size:(r + 1) * page_size] * v_refs[r][0, 0, h]
        acc_scr[h] = a


def _fox_sample_kernel(pt_ref, q_ref, vn_ref, sn_ref, lfn_ref, *refs, page_size):
    np_ = PAGES_PER_STEP
    k_refs = refs[0:np_]
    v_refs = refs[np_:2 * np_]
    lf_refs = refs[2 * np_:3 * np_]
    o_ref = refs[3 * np_]
    m_scr, l_scr, car_scr, acc_scr = refs[3 * np_ + 1:]
    j = pl.program_id(1)

    @pl.when(j == 0)
    def _():
        _decode_init(vn_ref, sn_ref, lfn_ref, m_scr, l_scr, car_scr, acc_scr, page_size)

    _decode_update(q_ref, k_refs, v_refs, lf_refs, m_scr, l_scr, car_scr, acc_scr, page_size)

    @pl.when(j == pl.num_programs(1) - 1)
    def _():
        _decode_finish(o_ref, l_scr, acc_scr)


def _decode_operands(page_table, q, v_new, s_new, lf_new, cache_kt, cache_vt, cache_lft, layer):
    nseq, n_pages = page_table.shape
    page_size = cache_kt.shape[-1]
    np_ = PAGES_PER_STEP

    def page_idx(r, nd):
        def f(b, j, pt):
            return (layer, pt[b * n_pages + (n_pages - 1 - (j * np_ + r))]) + (0,) * nd
        return f

    col = pl.BlockSpec((1, ATT_DIM, 1), lambda b, j, pt: (b, 0, 0))
    tok = pl.BlockSpec((1, 1, LANES), lambda b, j, pt: (b, 0, 0))
    kv_block = (1, 1, N_HEADS, HEAD_DIM, page_size)
    in_specs = [col, col, tok, tok]
    in_specs += [pl.BlockSpec(kv_block, page_idx(r, 3)) for r in range(np_)]
    in_specs += [pl.BlockSpec(kv_block, page_idx(r, 3)) for r in range(np_)]
    in_specs += [pl.BlockSpec((1, 1, N_HEADS, page_size), page_idx(r, 2)) for r in range(np_)]
    args = [q.reshape(nseq, ATT_DIM, 1), v_new.reshape(nseq, ATT_DIM, 1),
            s_new.reshape(nseq, 1, LANES), lf_new.reshape(nseq, 1, LANES)]
    args += [cache_kt] * np_ + [cache_vt] * np_ + [cache_lft] * np_
    out_spec = pl.BlockSpec((1, N_HEADS, HEAD_DIM, 1), lambda b, j, pt: (b, 0, 0, 0))
    out_shape = jax.ShapeDtypeStruct((nseq, N_HEADS, HEAD_DIM, 1), F32)
    scratch = [pltpu.VMEM((N_HEADS, 1), F32), pltpu.VMEM((N_HEADS, 1), F32),
               pltpu.VMEM((N_HEADS, 1), F32), pltpu.VMEM((N_HEADS, HEAD_DIM, page_size), F32)]
    return in_specs, args, out_spec, out_shape, scratch, n_pages // np_, page_size


def _fox_sample(page_table, q, v_new, s_new, lf_new, cache_kt, cache_vt, cache_lft, *, layer):
    nseq = page_table.shape[0]
    in_specs, args, out_spec, out_shape, scratch, nj, page_size = _decode_operands(
        page_table, q, v_new, s_new, lf_new, cache_kt, cache_vt, cache_lft, layer)
    grid_spec = pltpu.PrefetchScalarGridSpec(
        num_scalar_prefetch=1, grid=(nseq, nj), in_specs=in_specs, out_specs=out_spec, scratch_shapes=scratch)
    out = pl.pallas_call(
        functools.partial(_fox_sample_kernel, page_size=page_size),
        grid_spec=grid_spec,
        out_shape=out_shape,
        compiler_params=_params(("arbitrary", "arbitrary")),
        name="fox_sample",
    )(page_table.reshape(-1), *args)
    return out.reshape(nseq, ATT_DIM).astype(BF16)


def _post_kernel(x_ref, convo_ref, ssmy_ref, att_ref, p_ref, wglu_ref, wout_ref, gpost_ref, gfpre_ref,
                 wgate_ref, wup_ref, wdown_ref, gfpost_ref, wpg_ref, wpp_ref, o_ref):
    y = _gelu_tanh(ssmy_ref[...])
    ssm_out = y * _sigmoid(_dot(y.astype(BF16), wglu_ref[0]))
    mix = (_dot(convo_ref[...], wout_ref[0, 0:CONV_DIM, :])
           + _dot(ssm_out.astype(BF16), wout_ref[0, CONV_DIM:CONV_DIM + SSM_DIM, :])
           + _dot(att_ref[...], wout_ref[0, CONV_DIM + SSM_DIM:, :]))
    x1 = x_ref[...] + _rms(mix, gpost_ref[0])
    h2 = _rms(x1, gfpre_ref[0]).astype(BF16)
    ffn = jnp.zeros(x1.shape, F32)
    for c in range(D_FF // FF_CHUNK):
        cols = slice(c * FF_CHUNK, (c + 1) * FF_CHUNK)
        gate = _dot(h2, wgate_ref[0, :, cols])
        up = _dot(h2, wup_ref[0, :, cols])
        act = (gate * _sigmoid(gate) * up).astype(BF16)
        ffn = ffn + _dot(act, wdown_ref[0, cols, :])
    x2 = x1 + _rms(ffn, gfpost_ref[0])
    pgate = _sigmoid(_dot(x2.astype(BF16), wpg_ref[0]))
    o_ref[...] = x2 + pgate * _dot(p_ref[0].astype(BF16), wpp_ref[0])


def _post(x, convo, ssmy, att, pemb, wglu, wout, gpost, gfpre, wgate, wup, wdown, gfpost, wpg, wpp, layer, *, tm):
    n = x.shape[0]
    row = lambda w: pl.BlockSpec((tm, w), lambda i: (i, 0))
    weights = (wglu, wout, gpost, gfpre, wgate, wup, wdown, gfpost, wpg, wpp)
    return pl.pallas_call(
        _post_kernel,
        grid=(n // tm,),
        in_specs=[row(D_MODEL), row(CONV_DIM), row(SSM_DIM), row(ATT_DIM),
                  pl.BlockSpec((1, tm, PLE_DIM), lambda i: (layer, i, 0))]
                 + [_layer_spec(w, layer) for w in weights],
        out_specs=row(D_MODEL),
        out_shape=jax.ShapeDtypeStruct((n, D_MODEL), F32),
        compiler_params=_params(("arbitrary",)),
        name="post_mixer",
    )(x, convo, ssmy, att, pemb, *weights)


N_POST_WEIGHTS = 10


def _post_schedule(nsub):
    n_ff = D_FF // FF_CHUNK
    cost = ([("head", SSM_DIM * SSM_DIM + D_MODEL * D_MODEL)]
            + [("ffn", 3 * D_MODEL * FF_CHUNK)] * n_ff
            + [("gate", D_MODEL * D_MODEL), ("tail", PLE_DIM * D_MODEL)])
    total = sum(c for _, c in cost)
    groups = [[] for _ in range(nsub)]
    done = 0
    ff = 0
    for name, c in cost:
        g = min(nsub - 1, int((done + c / 2) * nsub / total))
        groups[g].append((name, ff))
        ff += name == "ffn"
        done += c
    return groups


def _post_decode_kernel(pt_ref, x_ref, convo_ref, ssmy_ref, att_ref, p_ref, *refs, page_size, nsub):
    (wglu_ref, wout_ref, gpost_ref, gfpre_ref, wgate_ref, wup_ref, wdown_ref, gfpost_ref,
     wpg_ref, wpp_ref) = refs[0:N_POST_WEIGHTS]
    q_ref, vn_ref, sn_ref, lfn_ref = refs[N_POST_WEIGHTS:N_POST_WEIGHTS + 4]
    np_ = PAGES_PER_STEP
    base = N_POST_WEIGHTS + 4
    k_refs = refs[base:base + np_]
    v_refs = refs[base + np_:base + 2 * np_]
    lf_refs = refs[base + 2 * np_:base + 3 * np_]
    o_ref, od_ref = refs[base + 3 * np_:base + 3 * np_ + 2]
    x1_scr, h2_scr, ffn_scr, m_scr, l_scr, car_scr, acc_scr = refs[base + 3 * np_ + 2:]
    step = pl.program_id(1)

    def head():
        y = _gelu_tanh(ssmy_ref[...])
        ssm_out = y * _sigmoid(_dot(y.astype(BF16), wglu_ref[0]))
        mix = (_dot(convo_ref[...], wout_ref[0, 0:CONV_DIM, :])
               + _dot(ssm_out.astype(BF16), wout_ref[0, CONV_DIM:CONV_DIM + SSM_DIM, :])
               + _dot(att_ref[...], wout_ref[0, CONV_DIM + SSM_DIM:, :]))
        x1 = x_ref[...] + _rms(mix, gpost_ref[0])
        x1_scr[...] = x1
        h2_scr[...] = _rms(x1, gfpre_ref[0]).astype(BF16)

    def ffn(c):
        cols = slice(c * FF_CHUNK, (c + 1) * FF_CHUNK)
        h2 = h2_scr[...]
        gate = _dot(h2, wgate_ref[0, :, cols])
        up = _dot(h2, wup_ref[0, :, cols])
        part = _dot((gate * _sigmoid(gate) * up).astype(BF16), wdown_ref[0, cols, :])
        if c == 0:
            ffn_scr[...] = part
        else:
            ffn_scr[...] += part

    def gate():
        x2 = x1_scr[...] + _rms(ffn_scr[...], gfpost_ref[0])
        x1_scr[...] = x2
        ffn_scr[...] = _sigmoid(_dot(x2.astype(BF16), wpg_ref[0]))

    def tail():
        o_ref[...] = x1_scr[...] + ffn_scr[...] * _dot(p_ref[0].astype(BF16), wpp_ref[0])

    work = {"head": lambda c: head(), "ffn": ffn, "gate": lambda c: gate(), "tail": lambda c: tail()}
    for k, items in enumerate(_post_schedule(nsub)):
        @pl.when(step == k)
        def _():
            lead = 1 if (len(items) >= 2 and all(name == "ffn" for name, _ in items)) else 0
            for name, c in items[:lead]:
                work[name](c)
            if k == 0:
                _decode_init(vn_ref, sn_ref, lfn_ref, m_scr, l_scr, car_scr, acc_scr, page_size)
            _decode_update(q_ref, k_refs, v_refs, lf_refs, m_scr, l_scr, car_scr, acc_scr, page_size)
            if k == nsub - 1:
                _decode_finish(od_ref, l_scr, acc_scr)
            for name, c in items[lead:]:
                work[name](c)


def _post_decode(x, convo, ssmy, att, pemb, weights, page_table, q, v_new, s_new, lf_new,
                 cache_kt, cache_vt, cache_lft, layer, *, tm):
    n = x.shape[0]
    nseq = page_table.shape[0]
    dec_specs, dec_args, dec_out_spec, dec_out_shape, dec_scratch, nsub, page_size = _decode_operands(
        page_table, q, v_new, s_new, lf_new, cache_kt, cache_vt, cache_lft, layer)
    row = lambda w: pl.BlockSpec((tm, w), lambda i, s, pt: (i, 0))
    grid_spec = pltpu.PrefetchScalarGridSpec(
        num_scalar_prefetch=1,
        grid=(nseq, nsub),
        in_specs=[row(D_MODEL), row(CONV_DIM), row(SSM_DIM), row(ATT_DIM),
                  pl.BlockSpec((1, tm, PLE_DIM), lambda i, s, pt: (layer, i, 0))]
                 + [_layer_spec(w, layer) for w in weights] + dec_specs,
        out_specs=(row(D_MODEL), dec_out_spec),
        scratch_shapes=[pltpu.VMEM((tm, D_MODEL), F32), pltpu.VMEM((tm, D_MODEL), BF16),
                        pltpu.VMEM((tm, D_MODEL), F32)] + dec_scratch,
    )
    xo, att_s = pl.pallas_call(
        functools.partial(_post_decode_kernel, page_size=page_size, nsub=nsub),
        grid_spec=grid_spec,
        out_shape=(jax.ShapeDtypeStruct((n, D_MODEL), F32), dec_out_shape),
        compiler_params=_params(("arbitrary", "arbitrary")),
        name="post_mixer_decode",
    )(page_table.reshape(-1), x, convo, ssmy, att, pemb, *weights, *dec_args)
    return xo, att_s.reshape(nseq, ATT_DIM).astype(BF16)


def kernel(x_prompt, x_sample, cache_k, cache_v, cache_logf, state_conv, state_ssm_re, state_ssm_im, page_table, p_prompt, p_sample, norm_mix_pre, norm_mix_post, norm_ffn_pre, norm_ffn_post, w_in, b_forget, conv_w, ssm_a_re, ssm_a_im, ssm_log_dt, ssm_b_re, ssm_b_im, ssm_c_re, ssm_c_im, ssm_d, w_ssm_glu, w_out, w_ffn_gate, w_ffn_up, w_ffn_down, w_ple_gate, w_ple_proj):
    depth = w_in.shape[0]
    bp, seq_len, _ = x_prompt.shape
    bs = x_sample.shape[0]
    n_p = bp * seq_len
    tm = min(512, seq_len)
    tq = min(512, seq_len)
    chunks_per_seq = seq_len // SSM_CHUNK
    n_chunks = n_p // SSM_CHUNK
    tmc = min(512, n_chunks)
    tm_fused = n_p // bs
    fuse_decode = (n_p % bs == 0 and tm_fused % 16 == 0 and tm_fused <= 512
                   and page_table.shape[1] % PAGES_PER_STEP == 0)

    gate_cols = 3 * CONV_DIM + SSM_DIM
    wg = w_in[:, :, 0:gate_cols].astype(BF16)
    wqkv = w_in[:, :, gate_cols:gate_cols + 3 * ATT_DIM].astype(BF16)
    wfl = jnp.pad(w_in[:, :, gate_cols + 3 * ATT_DIM:], ((0, 0), (0, 0), (0, LANES - N_HEADS))).astype(BF16)
    bfl = jnp.pad(b_forget, ((0, 0), (0, LANES - N_HEADS)))[:, None, :]
    wglu = w_ssm_glu.astype(BF16)
    wout = w_out.astype(BF16)
    wgate = w_ffn_gate.astype(BF16)
    wup = w_ffn_up.astype(BF16)
    wdown = w_ffn_down.astype(BF16)
    wpg = w_ple_gate.astype(BF16)
    wpp = w_ple_proj.astype(BF16)
    g_pre = norm_mix_pre[:, None, :]
    g_post = norm_mix_post[:, None, :]
    g_fpre = norm_ffn_pre[:, None, :]
    g_fpost = norm_ffn_post[:, None, :]
    d_row = ssm_d[:, None, :]

    w_g, w_ct, bd, b1, c1t, lam = _ssm_prep(ssm_a_re, ssm_a_im, ssm_log_dt, ssm_b_re, ssm_b_im,
                                            ssm_c_re, ssm_c_im)

    cache_kt = jnp.transpose(cache_k, (0, 1, 3, 4, 2))
    cache_vt = jnp.transpose(cache_v, (0, 1, 3, 4, 2))
    cache_lft = jnp.transpose(cache_logf, (0, 1, 3, 2))

    xp = x_prompt.reshape(n_p, D_MODEL)
    xs = x_sample.reshape(bs, D_MODEL)
    pe_prompt = p_prompt.reshape(depth, n_p, PLE_DIM)
    pe_sample = p_sample.reshape(depth, bs, PLE_DIM)
    outs_p = [[] for _ in range(6)]
    outs_s = [[] for _ in range(6)]
    for i in range(depth):
        (convo, uflat_f, uflat_b, q, k, v, kb, vb, lf, ct, crow, conv_new) = _inproj_prompt(
            xp, g_pre, wg, wqkv, wfl, bfl, conv_w, i, seq_len=seq_len, tm=tm)
        gstate = _ssm_chunk_state(uflat_b, w_g, i, tm=tmc, tn=1024)
        hin, hlast = _ssm_scan(gstate, lam, i, chunks_per_seq=chunks_per_seq)
        ssmy = _ssm_output(uflat_b, hin, uflat_f, bd, w_ct, d_row, i, tm=tmc)
        att = _fox_prompt(q, kb, vb, ct, crow, seq_len=seq_len, tq=tq)
        outs_p[0].append(k.reshape(bp, seq_len, N_HEADS, HEAD_DIM))
        outs_p[1].append(v.reshape(bp, seq_len, N_HEADS, HEAD_DIM))
        outs_p[2].append(lf.reshape(bp, seq_len, N_HEADS))
        outs_p[3].append(conv_new)
        hl = hlast.reshape(bp, 2, SSM_GROUPS, SSM_STATE)
        outs_p[4].append(hl[:, 0])
        outs_p[5].append(hl[:, 1])

        (convo_s, su_s, u_s, q_s, k_s, v_s, lf_s, sn_s) = _inproj_sample(
            xs, g_pre, wg, wqkv, wfl, bfl, conv_w, state_conv[i, :, 0], state_conv[i, :, 1], i)
        ssmy_s, hre_s, him_s = _ssm_step(
            su_s, state_ssm_re[i].reshape(bs, STATE_W), state_ssm_im[i].reshape(bs, STATE_W),
            b1, lam, c1t, d_row, i)
        post_w = (wglu, wout, g_post, g_fpre, wgate, wup, wdown, g_fpost, wpg, wpp)
        if fuse_decode:
            xp, att_s = _post_decode(xp, convo, ssmy.reshape(n_p, SSM_DIM), att, pe_prompt, post_w,
                                     page_table, q_s, v_s, sn_s, lf_s, cache_kt, cache_vt, cache_lft, i,
                                     tm=tm_fused)
        else:
            xp = _post(xp, convo, ssmy.reshape(n_p, SSM_DIM), att, pe_prompt, *post_w, i, tm=tm)
            att_s = _fox_sample(page_table, q_s, v_s, sn_s, lf_s, cache_kt, cache_vt, cache_lft, layer=i)
        xs = _post(xs, convo_s, ssmy_s, att_s, pe_sample,
                   wglu, wout, g_post, g_fpre, wgate, wup, wdown, g_fpost, wpg, wpp, i, tm=bs)
        outs_s[0].append(k_s.reshape(bs, 1, N_HEADS, HEAD_DIM))
        outs_s[1].append(v_s.reshape(bs, 1, N_HEADS, HEAD_DIM))
        outs_s[2].append(lf_s[:, 0:N_HEADS].reshape(bs, 1, N_HEADS))
        outs_s[3].append(jnp.stack([state_conv[i, :, 1], u_s], axis=1))
        outs_s[4].append(hre_s.reshape(bs, SSM_GROUPS, SSM_STATE))
        outs_s[5].append(him_s.reshape(bs, SSM_GROUPS, SSM_STATE))

    k_p, v_p, lf_p, conv_p, re_p, im_p = [jnp.stack(a) for a in outs_p]
    k_s, v_s, lf_s, conv_s, re_s, im_s = [jnp.stack(a) for a in outs_s]
    return (xp.reshape(bp, seq_len, D_MODEL), xs.reshape(bs, 1, D_MODEL),
            k_p, v_p, lf_p, conv_p, re_p, im_p, k_s, v_s, lf_s, conv_s, re_s, im_s)
```

```python
import functools
import math

import jax
import jax.numpy as jnp
from jax import lax
from jax.experimental import pallas as pl
from jax.experimental.pallas import tpu as pltpu

F32 = jnp.float32
BF16 = jnp.bfloat16
HIGHEST = lax.Precision.HIGHEST

D_MODEL = 1024
CONV_DIM = 256
SSM_DIM = 256
ATT_DIM = 512
N_HEADS = 8
HEAD_DIM = 64
SSM_GROUPS = 16
SSM_GROUP = 16
SSM_STATE = 64
D_FF = 2816
PLE_DIM = 256
CONV_K = 3
EPS = 1e-6
LOG2E = math.log2(math.e)
LANES = 128
SSM_CHUNK = 16
FF_CHUNK = 256
STATE_W = SSM_GROUPS * SSM_STATE
CHUNK_W = SSM_CHUNK * SSM_DIM
VMEM_LIMIT = 60 * 1024 * 1024


def _params(sem, vmem=VMEM_LIMIT):
    return pltpu.CompilerParams(dimension_semantics=sem, vmem_limit_bytes=vmem)


def _rms(x, g):
    return x * lax.rsqrt(jnp.mean(x * x, axis=-1, keepdims=True) + EPS) * g


def _sigmoid(x):
    return 1.0 / (1.0 + jnp.exp(-x))


def _log_sigmoid(x):
    return jnp.minimum(x, 0.0) - jnp.log1p(jnp.exp(-jnp.abs(x)))


def _gelu_tanh(x):
    return 0.5 * x * (1.0 + jnp.tanh(math.sqrt(2.0 / math.pi) * (x + 0.044715 * (x * x * x))))


def _dot(a, b, **kw):
    return jnp.dot(a, b, preferred_element_type=F32, **kw)


def _dot_nt(a, b, **kw):
    return lax.dot_general(a, b, (((1,), (1,)), ((), ())), preferred_element_type=F32, **kw)


def _split3(x):
    hi = x.astype(BF16)
    r1 = x - hi.astype(F32)
    mid = r1.astype(BF16)
    lo = (r1 - mid.astype(F32)).astype(BF16)
    return hi, mid, lo


def _layer_spec(arr, layer):
    zeros = (0,) * (arr.ndim - 1)
    return pl.BlockSpec((1,) + arr.shape[1:], lambda *_: (layer,) + zeros, pipeline_mode=pl.Buffered(1))


def _ssm_prep_kernel(are_ref, aim_ref, ldt_ref, btr_ref, bti_ref, ctr_ref, cti_ref,
                     wg_ref, wct_ref, bd_ref, b1_ref, c1t_ref, lam_ref, pw_scr):
    t = pl.program_id(1)
    a_re = are_ref[0]
    a_im = aim_ref[0]
    dt = jnp.exp(ldt_ref[0])
    mag = jnp.exp(a_re * dt)
    lr = mag * jnp.cos(a_im * dt)
    li = mag * jnp.sin(a_im * dt)
    den = a_re * a_re + a_im * a_im
    xr = lr - 1.0
    cfr = (xr * a_re + li * a_im) / den
    cfi = (li * a_re - xr * a_im) / den
    btr = btr_ref[0]
    bti = bti_ref[0]
    bbr = cfr * btr - cfi * bti
    bbi = cfr * bti + cfi * btr
    ctr = ctr_ref[0]
    cti = cti_ref[0]

    rows = SSM_GROUPS * SSM_GROUP
    grp_r = lax.broadcasted_iota(jnp.int32, (rows, STATE_W), 0) // SSM_GROUP
    grp_c = lax.broadcasted_iota(jnp.int32, (rows, STATE_W), 1) // SSM_STATE
    mask = (grp_r == grp_c).astype(F32)

    def blockdiag(x):
        return jnp.concatenate([x] * SSM_GROUPS, axis=0) * mask

    @pl.when(t == 0)
    def _():
        pw_scr[0:1, :] = jnp.ones((1, STATE_W), F32)
        pw_scr[1:2, :] = jnp.zeros((1, STATE_W), F32)
        lam_ref[0] = jnp.zeros((8, STATE_W), F32)
        lam_ref[0, 0:1, :] = lr
        lam_ref[0, 1:2, :] = li
        b1_ref[0, :, 0:STATE_W] = blockdiag(bbr)
        b1_ref[0, :, STATE_W:] = blockdiag(bbi)
        c1t_ref[0, :, 0:STATE_W] = blockdiag(ctr)
        c1t_ref[0, :, STATE_W:] = blockdiag(-cti)

    pr = pw_scr[0:1, :]
    pi = pw_scr[1:2, :]
    nr = pr * lr - pi * li
    ni = pr * li + pi * lr
    pw_scr[0:1, :] = nr
    pw_scr[1:2, :] = ni

    @pl.when(t == SSM_CHUNK - 1)
    def _():
        lam_ref[0, 2:3, :] = nr
        lam_ref[0, 3:4, :] = ni

    wg_ref[0, :, 0:STATE_W] = blockdiag(bbr * pr - bbi * pi).astype(BF16)
    wg_ref[0, :, STATE_W:] = blockdiag(bbr * pi + bbi * pr).astype(BF16)
    wct_ref[0, :, 0:STATE_W] = blockdiag(ctr * nr - cti * ni).astype(BF16)
    wct_ref[0, :, STATE_W:] = blockdiag(-(ctr * ni + cti * nr)).astype(BF16)
    pb = jnp.concatenate([blockdiag(bbr), blockdiag(bbi)], axis=1)
    pa = jnp.concatenate([blockdiag(ctr * pr - cti * pi), blockdiag(-(ctr * pi + cti * pr))], axis=1)
    pb_hi = pb.astype(BF16)
    pb_lo = (pb - pb_hi.astype(F32)).astype(BF16)
    pa_hi = pa.astype(BF16)
    pa_lo = (pa - pa_hi.astype(F32)).astype(BF16)
    bd_ref[0, 0] = (_dot_nt(pb_hi, pa_hi) + _dot_nt(pb_hi, pa_lo) + _dot_nt(pb_lo, pa_hi)).astype(BF16)


def _ssm_prep(a_re, a_im, log_dt, b_re, b_im, c_re, c_im):
    depth = a_re.shape[0]
    g, p, c, l = SSM_GROUPS, SSM_STATE, SSM_GROUP, SSM_CHUNK
    rows = g * c
    row1 = lambda x: x.reshape(depth, 1, g * p)
    chan = lambda x: x.reshape(depth, c, g * p)
    args = (row1(a_re), row1(a_im), row1(jnp.repeat(log_dt, p, axis=-1)),
            chan(jnp.transpose(b_re, (0, 3, 1, 2))), chan(jnp.transpose(b_im, (0, 3, 1, 2))),
            chan(jnp.transpose(c_re, (0, 2, 1, 3))), chan(jnp.transpose(c_im, (0, 2, 1, 3))))
    spec_row = pl.BlockSpec((1, 1, g * p), lambda i, t: (i, 0, 0))
    spec_chan = pl.BlockSpec((1, c, g * p), lambda i, t: (i, 0, 0))
    out_shapes = (
        jax.ShapeDtypeStruct((depth, l * rows, 2 * g * p), BF16),
        jax.ShapeDtypeStruct((depth, l * rows, 2 * g * p), BF16),
        jax.ShapeDtypeStruct((depth, l, rows, rows), BF16),
        jax.ShapeDtypeStruct((depth, rows, 2 * g * p), F32),
        jax.ShapeDtypeStruct((depth, rows, 2 * g * p), F32),
        jax.ShapeDtypeStruct((depth, 8, g * p), F32),
    )
    out_specs = (
        pl.BlockSpec((1, rows, 2 * g * p), lambda i, t: (i, l - 1 - t, 0)),
        pl.BlockSpec((1, rows, 2 * g * p), lambda i, t: (i, t, 0)),
        pl.BlockSpec((1, 1, rows, rows), lambda i, t: (i, t, 0, 0)),
        pl.BlockSpec((1, rows, 2 * g * p), lambda i, t: (i, 0, 0)),
        pl.BlockSpec((1, rows, 2 * g * p), lambda i, t: (i, 0, 0)),
        pl.BlockSpec((1, 8, g * p), lambda i, t: (i, 0, 0)),
    )
    return pl.pallas_call(
        _ssm_prep_kernel,
        grid=(depth, l),
        in_specs=[spec_row, spec_row, spec_row, spec_chan, spec_chan, spec_chan, spec_chan],
        out_specs=out_specs,
        out_shape=out_shapes,
        scratch_shapes=[pltpu.VMEM((2, g * p), F32)],
        compiler_params=_params(("arbitrary", "arbitrary")),
        name="ssm_prep",
    )(*args)


def _inproj_prompt_kernel(x_ref, g_ref, wg_ref, wqkv_ref, wfl_ref, bfl_ref, cw_ref,
                          convo_ref, su_ref, sub_ref, q_ref, k_ref, v_ref, kb_ref, vb_ref,
                          lf_ref, ct_ref, crow_ref, convnew_ref, ubuf, ccar, su_lo, su_hi, *, tiles_per_seq, tm):
    i = pl.program_id(0)

    @pl.when(i % tiles_per_seq == 0)
    def _():
        ubuf[0:8, :] = jnp.zeros((8, CONV_DIM), F32)
        ccar[...] = jnp.zeros_like(ccar)

    h = _rms(x_ref[...], g_ref[0]).astype(BF16)
    z = _dot(h, wg_ref[0])
    cb = z[:, 0:CONV_DIM]
    cc = z[:, CONV_DIM:2 * CONV_DIM]
    cv = z[:, 2 * CONV_DIM:3 * CONV_DIM]
    for half, scr in enumerate((su_lo, su_hi)):
        scr[...] = z[:, 3 * CONV_DIM + half * LANES:3 * CONV_DIM + (half + 1) * LANES]
        for s in range(SSM_CHUNK):
            piece = scr[pl.ds(s, tm // SSM_CHUNK, stride=SSM_CHUNK), :]
            cols = slice(s * SSM_DIM + half * LANES, s * SSM_DIM + (half + 1) * LANES)
            su_ref[:, cols] = piece
            sub_ref[:, cols] = piece.astype(BF16)
    u = cc * cv
    ubuf[8:8 + tm, :] = u
    u1 = ubuf[7:7 + tm, :]
    u2 = ubuf[6:6 + tm, :]
    cw = cw_ref[0]
    y = cw[0:1] * u2 + cw[1:2] * u1 + cw[2:3] * u
    convo_ref[...] = (cb * y).astype(BF16)
    convnew_ref[0] = u[tm - 2:tm, :]
    ubuf[0:8, :] = u[tm - 8:tm, :]

    zz = _dot(h, wqkv_ref[0])
    q_ref[...] = (zz[:, 0:ATT_DIM] * (HEAD_DIM ** -0.5 * LOG2E)).astype(BF16)
    k = zz[:, ATT_DIM:2 * ATT_DIM]
    v = zz[:, 2 * ATT_DIM:]
    k_ref[...] = k
    v_ref[...] = v
    kb_ref[...] = k.astype(BF16)
    vb_ref[...] = v.astype(BF16)

    fl = _dot(h, wfl_ref[0]) + bfl_ref[0]
    lane = lax.broadcasted_iota(jnp.int32, fl.shape, 1)
    lf = jnp.where(lane < N_HEADS, _log_sigmoid(fl), 0.0)
    lf_ref[...] = lf[:, 0:N_HEADS]
    row = lax.broadcasted_iota(jnp.int32, (LANES, LANES), 0)
    col = lax.broadcasted_iota(jnp.int32, (LANES, LANES), 1)
    tri = (row >= col).astype(F32).astype(BF16)
    carry = ccar[...]
    for r0 in range(0, tm, LANES):
        hi, mid, lo = _split3(lf[r0:r0 + LANES, :])
        c = _dot(tri, hi) + _dot(tri, mid) + _dot(tri, lo) + carry
        carry = c[LANES - 1:LANES, :]
        crow_ref[r0:r0 + LANES, :] = c
        ct_ref[:, r0:r0 + LANES] = c.T[0:N_HEADS, :]
    ccar[...] = carry


def _inproj_prompt(x, g_pre, wg, wqkv, wfl, bfl, conv_w, layer, *, seq_len, tm):
    n = x.shape[0]
    nt = n // tm
    tiles_per_seq = seq_len // tm
    nseq = n // seq_len
    tc = tm // SSM_CHUNK
    row = lambda w: pl.BlockSpec((tm, w), lambda i: (i, 0))
    chunk_rows = pl.BlockSpec((tc, CHUNK_W), lambda i: (i, 0))
    out_shapes = (
        jax.ShapeDtypeStruct((n, CONV_DIM), BF16),
        jax.ShapeDtypeStruct((n // SSM_CHUNK, CHUNK_W), F32),
        jax.ShapeDtypeStruct((n // SSM_CHUNK, CHUNK_W), BF16),
        jax.ShapeDtypeStruct((n, ATT_DIM), BF16),
        jax.ShapeDtypeStruct((n, ATT_DIM), F32),
        jax.ShapeDtypeStruct((n, ATT_DIM), F32),
        jax.ShapeDtypeStruct((n, ATT_DIM), BF16),
        jax.ShapeDtypeStruct((n, ATT_DIM), BF16),
        jax.ShapeDtypeStruct((n, N_HEADS), F32),
        jax.ShapeDtypeStruct((N_HEADS, n), F32),
        jax.ShapeDtypeStruct((n, LANES), F32),
        jax.ShapeDtypeStruct((nseq, CONV_K - 1, CONV_DIM), F32),
    )
    out_specs = (
        row(CONV_DIM), chunk_rows, chunk_rows, row(ATT_DIM), row(ATT_DIM), row(ATT_DIM),
        row(ATT_DIM), row(ATT_DIM), row(N_HEADS),
        pl.BlockSpec((N_HEADS, tm), lambda i: (0, i)),
        row(LANES),
        pl.BlockSpec((1, CONV_K - 1, CONV_DIM), lambda i: (i // tiles_per_seq, 0, 0)),
    )
    weights = (g_pre, wg, wqkv, wfl, bfl, conv_w)
    return pl.pallas_call(
        functools.partial(_inproj_prompt_kernel, tiles_per_seq=tiles_per_seq, tm=tm),
        grid=(nt,),
        in_specs=[row(D_MODEL)] + [_layer_spec(w, layer) for w in weights],
        out_specs=out_specs,
        out_shape=out_shapes,
        scratch_shapes=[pltpu.VMEM((tm + 8, CONV_DIM), F32), pltpu.VMEM((1, LANES), F32),
                        pltpu.VMEM((tm, LANES), F32), pltpu.VMEM((tm, LANES), F32)],
        compiler_params=_params(("arbitrary",)),
        name="inproj_prompt",
    )(x, *weights)


def _inproj_sample_kernel(x_ref, g_ref, wg_ref, wqkv_ref, wfl_ref, bfl_ref, cw_ref, b0_ref, b1_ref,
                          convo_ref, su_ref, u_ref, q_ref, k_ref, v_ref, lf_ref, sn_ref):
    h = _rms(x_ref[...], g_ref[0]).astype(BF16)
    z = _dot(h, wg_ref[0])
    cb = z[:, 0:CONV_DIM]
    u = z[:, CONV_DIM:2 * CONV_DIM] * z[:, 2 * CONV_DIM:3 * CONV_DIM]
    su_ref[...] = z[:, 3 * CONV_DIM:]
    u_ref[...] = u
    cw = cw_ref[0]
    y = cw[0:1] * b0_ref[...] + cw[1:2] * b1_ref[...] + cw[2:3] * u
    convo_ref[...] = (cb * y).astype(BF16)
    zz = _dot(h, wqkv_ref[0])
    q = zz[:, 0:ATT_DIM] * (HEAD_DIM ** -0.5)
    k = zz[:, ATT_DIM:2 * ATT_DIM]
    q_ref[...] = q
    k_ref[...] = k
    v_ref[...] = zz[:, 2 * ATT_DIM:]
    fl = _dot(h, wfl_ref[0]) + bfl_ref[0]
    lane = lax.broadcasted_iota(jnp.int32, fl.shape, 1)
    lf_ref[...] = jnp.where(lane < N_HEADS, _log_sigmoid(fl), 0.0)
    hd_row = lax.broadcasted_iota(jnp.int32, (ATT_DIM, LANES), 0) // HEAD_DIM
    hd_col = lax.broadcasted_iota(jnp.int32, (ATT_DIM, LANES), 1)
    sn_ref[...] = _dot(q * k, (hd_row == hd_col).astype(F32), precision=HIGHEST)


def _inproj_sample(x, g_pre, wg, wqkv, wfl, bfl, conv_w, buf0, buf1, layer):
    n = x.shape[0]
    full = lambda s: pl.BlockSpec(s, lambda i: (0,) * len(s))
    weights = (g_pre, wg, wqkv, wfl, bfl, conv_w)
    out_shapes = (
        jax.ShapeDtypeStruct((n, CONV_DIM), BF16),
        jax.ShapeDtypeStruct((n, SSM_DIM), F32),
        jax.ShapeDtypeStruct((n, CONV_DIM), F32),
        jax.ShapeDtypeStruct((n, ATT_DIM), F32),
        jax.ShapeDtypeStruct((n, ATT_DIM), F32),
        jax.ShapeDtypeStruct((n, ATT_DIM), F32),
        jax.ShapeDtypeStruct((n, LANES), F32),
        jax.ShapeDtypeStruct((n, LANES), F32),
    )
    args = (x,) + weights + (buf0, buf1)
    return pl.pallas_call(
        _inproj_sample_kernel,
        grid=(1,),
        in_specs=[full(x.shape)] + [_layer_spec(w, layer) for w in weights] + [full(buf0.shape), full(buf1.shape)],
        out_specs=tuple(full(s.shape) for s in out_shapes),
        out_shape=out_shapes,
        compiler_params=_params(("arbitrary",)),
        name="inproj_sample",
    )(*args)


def _ssm_chunk_state_kernel(u_ref, w_ref, o_ref):
    o_ref[...] = _dot(u_ref[...], w_ref[0])


def _ssm_chunk_state(uflat, w_g, layer, *, tm, tn):
    m, k = uflat.shape
    n = w_g.shape[2]
    return pl.pallas_call(
        _ssm_chunk_state_kernel,
        grid=(n // tn, m // tm),
        in_specs=[pl.BlockSpec((tm, k), lambda j, i: (i, 0)),
                  pl.BlockSpec((1, k, tn), lambda j, i: (layer, 0, j))],
        out_specs=pl.BlockSpec((tm, tn), lambda j, i: (i, j)),
        out_shape=jax.ShapeDtypeStruct((m, n), F32),
        compiler_params=_params(("arbitrary", "arbitrary")),
        name="ssm_chunk_state",
    )(uflat, w_g)


def _ssm_scan_kernel(g_ref, lam_ref, hin_ref, hlast_ref, *, n_chunks):
    lr = lam_ref[0, 2:3, :]
    li = lam_ref[0, 3:4, :]

    def body(c, carry):
        h_re, h_im = carry
        hin_ref[pl.ds(c, 1), 0:STATE_W] = h_re
        hin_ref[pl.ds(c, 1), STATE_W:] = h_im
        g_re = g_ref[pl.ds(c, 1), 0:STATE_W]
        g_im = g_ref[pl.ds(c, 1), STATE_W:]
        return lr * h_re - li * h_im + g_re, lr * h_im + li * h_re + g_im

    zero = jnp.zeros((1, STATE_W), F32)
    h_re, h_im = lax.fori_loop(0, n_chunks, body, (zero, zero))
    hlast_ref[0, :, 0:STATE_W] = h_re
    hlast_ref[0, :, STATE_W:] = h_im


def _ssm_scan(gstate, lam, layer, *, chunks_per_seq):
    m, w = gstate.shape
    nseq = m // chunks_per_seq
    return pl.pallas_call(
        functools.partial(_ssm_scan_kernel, n_chunks=chunks_per_seq),
        grid=(nseq,),
        in_specs=[pl.BlockSpec((chunks_per_seq, w), lambda b: (b, 0)),
                  pl.BlockSpec((1, 8, STATE_W), lambda b: (layer, 0, 0))],
        out_specs=(pl.BlockSpec((chunks_per_seq, w), lambda b: (b, 0)),
                   pl.BlockSpec((1, 1, w), lambda b: (b, 0, 0))),
        out_shape=(jax.ShapeDtypeStruct((m, w), F32), jax.ShapeDtypeStruct((nseq, 1, w), F32)),
        compiler_params=_params(("arbitrary",)),
        name="ssm_scan",
    )(gstate, lam)


def _ssm_output_kernel(ub_ref, hin_ref, uf_ref, bd_ref, wct_ref, d_ref, y_ref, hb_scr):
    j = pl.program_id(1)
    cw = SSM_DIM

    @pl.when(j == 0)
    def _():
        hb_scr[...] = hin_ref[...].astype(BF16)

    y_ref[...] = _dot_nt(hb_scr[...], wct_ref[0]) + d_ref[0] * uf_ref[...]
    for s in range(SSM_CHUNK):
        @pl.when(s <= j)
        def _():
            y_ref[...] += _dot(ub_ref[:, s * cw:(s + 1) * cw], bd_ref[0, j - s])


def _ssm_output(uflat_b, hin, uflat_f, bd, wct, d_row, layer, *, tm):
    m, k = uflat_b.shape
    ks = hin.shape[1]
    cw = SSM_DIM
    return pl.pallas_call(
        _ssm_output_kernel,
        grid=(m // tm, SSM_CHUNK),
        in_specs=[pl.BlockSpec((tm, k), lambda i, j: (i, 0)),
                  pl.BlockSpec((tm, ks), lambda i, j: (i, 0)),
                  pl.BlockSpec((tm, cw), lambda i, j: (i, j)),
                  pl.BlockSpec((1, SSM_CHUNK, cw, cw), lambda i, j: (layer, 0, 0, 0)),
                  pl.BlockSpec((1, cw, ks), lambda i, j: (layer, j, 0)),
                  pl.BlockSpec((1, 1, cw), lambda i, j: (layer, 0, 0))],
        out_specs=pl.BlockSpec((tm, cw), lambda i, j: (i, j)),
        out_shape=jax.ShapeDtypeStruct((m, k), F32),
        scratch_shapes=[pltpu.VMEM((tm, ks), BF16)],
        compiler_params=_params(("arbitrary", "arbitrary")),
        name="ssm_output",
    )(uflat_b, hin, uflat_f, bd, wct, d_row)


def _ssm_step_kernel(u_ref, hre_ref, him_ref, b1_ref, lam_ref, c1t_ref, d_ref,
                     y_ref, ore_ref, oim_ref):
    u = u_ref[...]
    lr = lam_ref[0, 0:1, :]
    li = lam_ref[0, 1:2, :]
    h_re = hre_ref[...]
    h_im = him_ref[...]
    bu = _dot(u, b1_ref[0], precision=HIGHEST)
    n_re = lr * h_re - li * h_im + bu[:, 0:STATE_W]
    n_im = lr * h_im + li * h_re + bu[:, STATE_W:]
    ore_ref[...] = n_re
    oim_ref[...] = n_im
    y = (_dot_nt(n_re, c1t_ref[0, :, 0:STATE_W], precision=HIGHEST)
         + _dot_nt(n_im, c1t_ref[0, :, STATE_W:], precision=HIGHEST))
    y_ref[...] = y + d_ref[0] * u


def _ssm_step(u, h_re, h_im, b1, lam, c1t, d, layer):
    n = u.shape[0]
    full = lambda s: pl.BlockSpec(s, lambda i: (0,) * len(s))
    lay = lambda s: pl.BlockSpec((1,) + s[1:], lambda i: (layer,) + (0,) * (len(s) - 1))
    out_shapes = (jax.ShapeDtypeStruct((n, SSM_DIM), F32), jax.ShapeDtypeStruct((n, STATE_W), F32),
                  jax.ShapeDtypeStruct((n, STATE_W), F32))
    return pl.pallas_call(
        _ssm_step_kernel,
        grid=(1,),
        in_specs=[full(u.shape), full(h_re.shape), full(h_im.shape), lay(b1.shape), lay(lam.shape),
                  lay(c1t.shape), lay(d.shape)],
        out_specs=tuple(full(s.shape) for s in out_shapes),
        out_shape=out_shapes,
        compiler_params=_params(("arbitrary",)),
        name="ssm_step",
    )(u, h_re, h_im, b1, lam, c1t, d)


def _fox_prompt_kernel(qi_ref, ki_ref, q_ref, k_ref, v_ref, cq_ref, ck_ref, o_ref,
                       m_scr, l_scr, acc_scr, cq_scr, *, tq, tk, rb):
    qi = qi_ref[pl.program_id(2)]
    ki = ki_ref[pl.program_id(2)]
    reps = tk // LANES

    @pl.when(ki == 0)
    def _():
        m_scr[...] = jnp.full(m_scr.shape, -jnp.inf, F32)
        l_scr[...] = jnp.zeros_like(l_scr)
        acc_scr[...] = jnp.zeros_like(acc_scr)
        src_lane = lax.broadcasted_iota(jnp.int32, (LANES, 2 * LANES), 0)
        dst_head = lax.broadcasted_iota(jnp.int32, (LANES, 2 * LANES), 1) // LANES
        pick = (src_lane == 2 * pl.program_id(1) + dst_head).astype(F32).astype(BF16)
        hi, mid, lo = _split3(cq_ref[...])
        spread = (_dot(hi, pick) + _dot(mid, pick) + _dot(lo, pick)) * LOG2E
        for hh in range(2):
            cq_scr[hh] = spread[:, hh * LANES:(hh + 1) * LANES]

    def step(on_diagonal):
        k2 = k_ref[...]
        v2 = v_ref[...]
        lane = lax.broadcasted_iota(jnp.int32, (rb, LANES), 1)
        keep = [(lane < HEAD_DIM).astype(F32).astype(BF16), (lane >= HEAD_DIM).astype(F32).astype(BF16)]
        lane_o = lane < HEAD_DIM
        ck2 = [ck_ref[0, hh:hh + 1, :] * LOG2E for hh in range(2)]
        nkeys = [min(tk, r0 + rb) if on_diagonal else tk for r0 in range(0, tq, rb)]
        scores = [[_dot_nt(q_ref[r0:r0 + rb, :] * keep[hh], k2[0:nkeys[bi], :]) for hh in range(2)]
                  for bi, r0 in enumerate(range(0, tq, rb))]
        for bi, r0 in enumerate(range(0, tq, rb)):
            rows = slice(r0, r0 + rb)
            reps = nkeys[bi] // LANES
            if on_diagonal:
                rel = (lax.broadcasted_iota(jnp.int32, (rb, LANES), 1)
                       - lax.broadcasted_iota(jnp.int32, (rb, LANES), 0))
            alphas = []
            pvs = []
            for hh in range(2):
                sc = []
                for c in range(reps):
                    cols = slice(c * LANES, (c + 1) * LANES)
                    s_c = scores[bi][hh][:, cols] - ck2[hh][:, cols]
                    if on_diagonal:
                        s_c = jnp.where(rel <= r0 - c * LANES, s_c, -jnp.inf)
                    sc.append(s_c)
                mx = sc[0]
                for c in range(1, reps):
                    mx = jnp.maximum(mx, sc[c])
                cq = cq_scr[hh, rows, :]
                m_prev = m_scr[hh, rows, :]
                m_new = jnp.maximum(m_prev, jnp.max(mx, axis=-1, keepdims=True) + cq)
                t = m_new - cq
                ps = [jnp.exp2(s_c - t) for s_c in sc]
                tot = ps[0]
                for c in range(1, reps):
                    tot = tot + ps[c]
                alpha = jnp.exp2(m_prev - m_new)
                l_scr[hh, rows, :] = alpha * l_scr[hh, rows, :] + jnp.sum(tot, axis=-1, keepdims=True)
                m_scr[hh, rows, :] = m_new
                alphas.append(alpha)
                p = jnp.concatenate([p_c.astype(BF16) for p_c in ps], axis=1)
                pvs.append(_dot(p, v2[0:nkeys[bi], :]))
            acc_scr[rows, :] = (jnp.where(lane_o, alphas[0], alphas[1]) * acc_scr[rows, :]
                                + jnp.where(lane_o, pvs[0], pvs[1]))

    @pl.when(ki < qi)
    def _():
        step(False)

    @pl.when(ki == qi)
    def _():
        step(True)
        lane_o = lax.broadcasted_iota(jnp.int32, (tq, LANES), 1) < HEAD_DIM
        o_ref[...] = (acc_scr[...] / jnp.where(lane_o, l_scr[0], l_scr[1])).astype(o_ref.dtype)


def _fox_prompt(q, kb, vb, ct, cq, *, seq_len, tq):
    n = q.shape[0]
    nseq = n // seq_len
    nq = seq_len // tq
    hp = N_HEADS // 2
    ck = ct.reshape(hp, 2, n)
    pairs = [(i, j) for i in range(nq) for j in range(i + 1)]
    qi_tab = jnp.asarray([p[0] for p in pairs], jnp.int32)
    ki_tab = jnp.asarray([p[1] for p in pairs], jnp.int32)
    q_idx = lambda b, h, t, qt, kt: (b * nq + qt[t], h)
    kv_idx = lambda b, h, t, qt, kt: (b * nq + kt[t], h)
    grid_spec = pltpu.PrefetchScalarGridSpec(
        num_scalar_prefetch=2,
        grid=(nseq, hp, len(pairs)),
        in_specs=[pl.BlockSpec((tq, LANES), q_idx),
                  pl.BlockSpec((tq, LANES), kv_idx),
                  pl.BlockSpec((tq, LANES), kv_idx),
                  pl.BlockSpec((tq, LANES), lambda b, h, t, qt, kt: (b * nq + qt[t], 0)),
                  pl.BlockSpec((1, 2, tq), lambda b, h, t, qt, kt: (h, 0, b * nq + kt[t]))],
        out_specs=pl.BlockSpec((tq, LANES), q_idx),
        scratch_shapes=[pltpu.VMEM((2, tq, LANES), F32), pltpu.VMEM((2, tq, LANES), F32),
                        pltpu.VMEM((tq, LANES), F32), pltpu.VMEM((2, tq, LANES), F32)],
    )
    return pl.pallas_call(
        functools.partial(_fox_prompt_kernel, tq=tq, tk=tq, rb=min(256, tq)),
        grid_spec=grid_spec,
        out_shape=jax.ShapeDtypeStruct((n, ATT_DIM), BF16),
        compiler_params=_params(("arbitrary", "arbitrary", "arbitrary")),
        name="fox_prompt",
    )(qi_tab, ki_tab, q, kb, vb, cq, ck)


PAGES_PER_STEP = 8


def _decode_init(vn_ref, sn_ref, lfn_ref, m_scr, l_scr, car_scr, acc_scr, page_size):
    eye = (lax.broadcasted_iota(jnp.int32, (N_HEADS, LANES), 0)
           == lax.broadcasted_iota(jnp.int32, (N_HEADS, LANES), 1)).astype(F32)

    def to_col(row):
        return jnp.sum(eye * row, axis=-1, keepdims=True)

    m_scr[...] = to_col(sn_ref[0])
    l_scr[...] = jnp.ones_like(l_scr)
    car_scr[...] = to_col(lfn_ref[0])
    lane0 = lax.broadcasted_iota(jnp.int32, (HEAD_DIM, page_size), 1) == 0
    for h in range(N_HEADS):
        vcol = vn_ref[0, h * HEAD_DIM:(h + 1) * HEAD_DIM, :]
        acc_scr[h] = jnp.where(lane0, vcol, 0.0)


def _decode_finish(o_ref, l_scr, acc_scr):
    inv = 1.0 / l_scr[...]
    for h in range(N_HEADS):
        o_ref[0, h] = jnp.sum(acc_scr[h], axis=-1, keepdims=True) * inv[h:h + 1, :]


def _decode_update(q_ref, k_refs, v_refs, lf_refs, m_scr, l_scr, car_scr, acc_scr, page_size):
    np_ = PAGES_PER_STEP
    later = (lax.broadcasted_iota(jnp.int32, (page_size, page_size), 0)
             > lax.broadcasted_iota(jnp.int32, (page_size, page_size), 1)).astype(F32)
    fold = (lax.broadcasted_iota(jnp.int32, (N_HEADS, N_HEADS * 8), 1) // 8
            == lax.broadcasted_iota(jnp.int32, (N_HEADS, N_HEADS * 8), 0)).astype(F32)
    partial_sums = []
    for r in range(np_):
        parts = []
        for h in range(N_HEADS):
            prod = k_refs[r][0, 0, h] * q_ref[0, h * HEAD_DIM:(h + 1) * HEAD_DIM, :]
            t = prod[0:8]
            for a in range(1, HEAD_DIM // 8):
                t = t + prod[a * 8:(a + 1) * 8]
            parts.append(t)
        partial_sums.append(jnp.concatenate(parts, axis=0))
    s_qk = _dot(fold, jnp.concatenate(partial_sums, axis=1), precision=HIGHEST)
    lf_all = jnp.concatenate([lf_refs[r][0, 0] for r in range(np_)], axis=0)
    within = _dot(lf_all, later, precision=HIGHEST)
    page_tot = jnp.sum(lf_all, axis=-1, keepdims=True)
    carry = car_scr[...]
    bias = []
    for r in range(np_):
        bias.append(within[r * N_HEADS:(r + 1) * N_HEADS] + carry)
        carry = carry + page_tot[r * N_HEADS:(r + 1) * N_HEADS]
    car_scr[...] = carry
    s_all = s_qk + jnp.concatenate(bias, axis=1)
    m_prev = m_scr[...]
    m_new = jnp.maximum(m_prev, jnp.max(s_all, axis=-1, keepdims=True))
    alpha = jnp.exp(m_prev - m_new)
    p_all = jnp.exp(s_all - m_new)
    l_scr[...] = alpha * l_scr[...] + jnp.sum(p_all, axis=-1, keepdims=True)
    m_scr[...] = m_new
    for h in range(N_HEADS):
        a = acc_scr[h] * alpha[h:h + 1, :]
        for r in range(np_):
            a = a + p_all[h:h + 1, r * page_size:(r + 1) * page_size] * v_refs[r][0, 0, h]
        acc_scr[h] = a


def _fox_sample_kernel(pt_ref, q_ref, vn_ref, sn_ref, lfn_ref, *refs, page_size):
    np_ = PAGES_PER_STEP
    k_refs = refs[0:np_]
    v_refs = refs[np_:2 * np_]
    lf_refs = refs[2 * np_:3 * np_]
    o_ref = refs[3 * np_]
    m_scr, l_scr, car_scr, acc_scr = refs[3 * np_ + 1:]
    j = pl.program_id(1)

    @pl.when(j == 0)
    def _():
        _decode_init(vn_ref, sn_ref, lfn_ref, m_scr, l_scr, car_scr, acc_scr, page_size)

    _decode_update(q_ref, k_refs, v_refs, lf_refs, m_scr, l_scr, car_scr, acc_scr, page_size)

    @pl.when(j == pl.num_programs(1) - 1)
    def _():
        _decode_finish(o_ref, l_scr, acc_scr)


def _decode_operands(page_table, q, v_new, s_new, lf_new, cache_kt, cache_vt, cache_lft, layer):
    nseq, n_pages = page_table.shape
    page_size = cache_kt.shape[-1]
    np_ = PAGES_PER_STEP

    def page_idx(r, nd):
        def f(b, j, pt):
            return (layer, pt[b * n_pages + (n_pages - 1 - (j * np_ + r))]) + (0,) * nd
        return f

    col = pl.BlockSpec((1, ATT_DIM, 1), lambda b, j, pt: (b, 0, 0))
    tok = pl.BlockSpec((1, 1, LANES), lambda b, j, pt: (b, 0, 0))
    kv_block = (1, 1, N_HEADS, HEAD_DIM, page_size)
    in_specs = [col, col, tok, tok]
    in_specs += [pl.BlockSpec(kv_block, page_idx(r, 3)) for r in range(np_)]
    in_specs += [pl.BlockSpec(kv_block, page_idx(r, 3)) for r in range(np_)]
    in_specs += [pl.BlockSpec((1, 1, N_HEADS, page_size), page_idx(r, 2)) for r in range(np_)]
    args = [q.reshape(nseq, ATT_DIM, 1), v_new.reshape(nseq, ATT_DIM, 1),
            s_new.reshape(nseq, 1, LANES), lf_new.reshape(nseq, 1, LANES)]
    args += [cache_kt] * np_ + [cache_vt] * np_ + [cache_lft] * np_
    out_spec = pl.BlockSpec((1, N_HEADS, HEAD_DIM, 1), lambda b, j, pt: (b, 0, 0, 0))
    out_shape = jax.ShapeDtypeStruct((nseq, N_HEADS, HEAD_DIM, 1), F32)
    scratch = [pltpu.VMEM((N_HEADS, 1), F32), pltpu.VMEM((N_HEADS, 1), F32),
               pltpu.VMEM((N_HEADS, 1), F32), pltpu.VMEM((N_HEADS, HEAD_DIM, page_size), F32)]
    return in_specs, args, out_spec, out_shape, scratch, n_pages // np_, page_size


def _fox_sample(page_table, q, v_new, s_new, lf_new, cache_kt, cache_vt, cache_lft, *, layer):
    nseq = page_table.shape[0]
    in_specs, args, out_spec, out_shape, scratch, nj, page_size = _decode_operands(
        page_table, q, v_new, s_new, lf_new, cache_kt, cache_vt, cache_lft, layer)
    grid_spec = pltpu.PrefetchScalarGridSpec(
        num_scalar_prefetch=1, grid=(nseq, nj), in_specs=in_specs, out_specs=out_spec, scratch_shapes=scratch)
    out = pl.pallas_call(
        functools.partial(_fox_sample_kernel, page_size=page_size),
        grid_spec=grid_spec,
        out_shape=out_shape,
        compiler_params=_params(("arbitrary", "arbitrary")),
        name="fox_sample",
    )(page_table.reshape(-1), *args)
    return out.reshape(nseq, ATT_DIM).astype(BF16)


def _post_kernel(x_ref, convo_ref, ssmy_ref, att_ref, p_ref, wglu_ref, wout_ref, gpost_ref, gfpre_ref,
                 wgate_ref, wup_ref, wdown_ref, gfpost_ref, wpg_ref, wpp_ref, o_ref):
    y = _gelu_tanh(ssmy_ref[...])
    ssm_out = y * _sigmoid(_dot(y.astype(BF16), wglu_ref[0]))
    mix = (_dot(convo_ref[...], wout_ref[0, 0:CONV_DIM, :])
           + _dot(ssm_out.astype(BF16), wout_ref[0, CONV_DIM:CONV_DIM + SSM_DIM, :])
           + _dot(att_ref[...], wout_ref[0, CONV_DIM + SSM_DIM:, :]))
    x1 = x_ref[...] + _rms(mix, gpost_ref[0])
    h2 = _rms(x1, gfpre_ref[0]).astype(BF16)
    ffn = jnp.zeros(x1.shape, F32)
    for c in range(D_FF // FF_CHUNK):
        cols = slice(c * FF_CHUNK, (c + 1) * FF_CHUNK)
        gate = _dot(h2, wgate_ref[0, :, cols])
        up = _dot(h2, wup_ref[0, :, cols])
        act = (gate * _sigmoid(gate) * up).astype(BF16)
        ffn = ffn + _dot(act, wdown_ref[0, cols, :])
    x2 = x1 + _rms(ffn, gfpost_ref[0])
    pgate = _sigmoid(_dot(x2.astype(BF16), wpg_ref[0]))
    o_ref[...] = x2 + pgate * _dot(p_ref[0].astype(BF16), wpp_ref[0])


def _post(x, convo, ssmy, att, pemb, wglu, wout, gpost, gfpre, wgate, wup, wdown, gfpost, wpg, wpp, layer, *, tm):
    n = x.shape[0]
    row = lambda w: pl.BlockSpec((tm, w), lambda i: (i, 0))
    weights = (wglu, wout, gpost, gfpre, wgate, wup, wdown, gfpost, wpg, wpp)
    return pl.pallas_call(
        _post_kernel,
        grid=(n // tm,),
        in_specs=[row(D_MODEL), row(CONV_DIM), row(SSM_DIM), row(ATT_DIM),
                  pl.BlockSpec((1, tm, PLE_DIM), lambda i: (layer, i, 0))]
                 + [_layer_spec(w, layer) for w in weights],
        out_specs=row(D_MODEL),
        out_shape=jax.ShapeDtypeStruct((n, D_MODEL), F32),
        compiler_params=_params(("arbitrary",)),
        name="post_mixer",
    )(x, convo, ssmy, att, pemb, *weights)


N_POST_WEIGHTS = 10


def _post_schedule(nsub):
    n_ff = D_FF // FF_CHUNK
    cost = ([("head", SSM_DIM * SSM_DIM + D_MODEL * D_MODEL)]
            + [("ffn", 3 * D_MODEL * FF_CHUNK)] * n_ff
            + [("gate", D_MODEL * D_MODEL), ("tail", PLE_DIM * D_MODEL)])
    total = sum(c for _, c in cost)
    groups = [[] for _ in range(nsub)]
    done = 0
    ff = 0
    for name, c in cost:
        g = min(nsub - 1, int((done + c / 2) * nsub / total))
        groups[g].append((name, ff))
        ff += name == "ffn"
        done += c
    return groups


def _post_decode_kernel(pt_ref, x_ref, convo_ref, ssmy_ref, att_ref, p_ref, *refs, page_size, nsub):
    (wglu_ref, wout_ref, gpost_ref, gfpre_ref, wgate_ref, wup_ref, wdown_ref, gfpost_ref,
     wpg_ref, wpp_ref) = refs[0:N_POST_WEIGHTS]
    q_ref, vn_ref, sn_ref, lfn_ref = refs[N_POST_WEIGHTS:N_POST_WEIGHTS + 4]
    np_ = PAGES_PER_STEP
    base = N_POST_WEIGHTS + 4
    k_refs = refs[base:base + np_]
    v_refs = refs[base + np_:base + 2 * np_]
    lf_refs = refs[base + 2 * np_:base + 3 * np_]
    o_ref, od_ref = refs[base + 3 * np_:base + 3 * np_ + 2]
    x1_scr, h2_scr, ffn_scr, ylo_scr, yhi_scr, m_scr, l_scr, car_scr, acc_scr = refs[base + 3 * np_ + 2:]
    step = pl.program_id(1)

    def head():
        tc = ssmy_ref.shape[0]
        for half, scr in enumerate((ylo_scr, yhi_scr)):
            for t in range(SSM_CHUNK):
                c0 = t * SSM_DIM + half * LANES
                scr[pl.ds(t, tc, stride=SSM_CHUNK), :] = ssmy_ref[:, c0:c0 + LANES]
        y = _gelu_tanh(jnp.concatenate([ylo_scr[...], yhi_scr[...]], axis=1))
        ssm_out = y * _sigmoid(_dot(y.astype(BF16), wglu_ref[0]))
        mix = (_dot(convo_ref[...], wout_ref[0, 0:CONV_DIM, :])
               + _dot(ssm_out.astype(BF16), wout_ref[0, CONV_DIM:CONV_DIM + SSM_DIM, :])
               + _dot(att_ref[...], wout_ref[0, CONV_DIM + SSM_DIM:, :]))
        x1 = x_ref[...] + _rms(mix, gpost_ref[0])
        x1_scr[...] = x1
        h2_scr[...] = _rms(x1, gfpre_ref[0]).astype(BF16)

    def ffn(c):
        cols = slice(c * FF_CHUNK, (c + 1) * FF_CHUNK)
        h2 = h2_scr[...]
        gate = _dot(h2, wgate_ref[0, :, cols])
        up = _dot(h2, wup_ref[0, :, cols])
        part = _dot((gate * _sigmoid(gate) * up).astype(BF16), wdown_ref[0, cols, :])
        if c == 0:
            ffn_scr[...] = part
        else:
            ffn_scr[...] += part

    def gate():
        x2 = x1_scr[...] + _rms(ffn_scr[...], gfpost_ref[0])
        x1_scr[...] = x2
        ffn_scr[...] = _sigmoid(_dot(x2.astype(BF16), wpg_ref[0]))

    def tail():
        o_ref[...] = x1_scr[...] + ffn_scr[...] * _dot(p_ref[0].astype(BF16), wpp_ref[0])

    work = {"head": lambda c: head(), "ffn": ffn, "gate": lambda c: gate(), "tail": lambda c: tail()}
    for k, items in enumerate(_post_schedule(nsub)):
        @pl.when(step == k)
        def _():
            lead = 1 if (len(items) >= 2 and all(name == "ffn" for name, _ in items)) else 0
            for name, c in items[:lead]:
                work[name](c)
            if k == 0:
                _decode_init(vn_ref, sn_ref, lfn_ref, m_scr, l_scr, car_scr, acc_scr, page_size)
            _decode_update(q_ref, k_refs, v_refs, lf_refs, m_scr, l_scr, car_scr, acc_scr, page_size)
            if k == nsub - 1:
                _decode_finish(od_ref, l_scr, acc_scr)
            for name, c in items[lead:]:
                work[name](c)


def _post_decode(x, convo, ssmy, att, pemb, weights, page_table, q, v_new, s_new, lf_new,
                 cache_kt, cache_vt, cache_lft, layer, *, tm):
    n = x.shape[0]
    nseq = page_table.shape[0]
    dec_specs, dec_args, dec_out_spec, dec_out_shape, dec_scratch, nsub, page_size = _decode_operands(
        page_table, q, v_new, s_new, lf_new, cache_kt, cache_vt, cache_lft, layer)
    row = lambda w: pl.BlockSpec((tm, w), lambda i, s, pt: (i, 0))
    grid_spec = pltpu.PrefetchScalarGridSpec(
        num_scalar_prefetch=1,
        grid=(nseq, nsub),
        in_specs=[row(D_MODEL), row(CONV_DIM),
                  pl.BlockSpec((tm // SSM_CHUNK, CHUNK_W), lambda i, s, pt: (i, 0)), row(ATT_DIM),
                  pl.BlockSpec((1, tm, PLE_DIM), lambda i, s, pt: (layer, i, 0))]
                 + [_layer_spec(w, layer) for w in weights] + dec_specs,
        out_specs=(row(D_MODEL), dec_out_spec),
        scratch_shapes=[pltpu.VMEM((tm, D_MODEL), F32), pltpu.VMEM((tm, D_MODEL), BF16),
                        pltpu.VMEM((tm, D_MODEL), F32), pltpu.VMEM((tm, LANES), F32),
                        pltpu.VMEM((tm, LANES), F32)] + dec_scratch,
    )
    xo, att_s = pl.pallas_call(
        functools.partial(_post_decode_kernel, page_size=page_size, nsub=nsub),
        grid_spec=grid_spec,
        out_shape=(jax.ShapeDtypeStruct((n, D_MODEL), F32), dec_out_shape),
        compiler_params=_params(("arbitrary", "arbitrary")),
        name="post_mixer_decode",
    )(page_table.reshape(-1), x, convo, ssmy, att, pemb, *weights, *dec_args)
    return xo, att_s.reshape(nseq, ATT_DIM).astype(BF16)


def _kv_gather_kernel(*refs, depth, nseq):
    in_refs = refs[0:depth]
    o_ref = refs[depth]
    layer = pl.program_id(0)
    for l in range(depth):
        @pl.when(layer == l)
        def _():
            for b in range(nseq):
                o_ref[0, b] = in_refs[l][b].T.reshape(N_HEADS, HEAD_DIM, LANES)


def _kv_gather(per_layer, *, nseq, seq_len):
    depth = len(per_layer)
    nt = seq_len // LANES
    views = [a.reshape(nseq, seq_len, ATT_DIM) for a in per_layer]

    def in_spec(l):
        return pl.BlockSpec((nseq, LANES, ATT_DIM),
                            lambda d, t: (0, jnp.where(d == l, t, jnp.where(d < l, 0, nt - 1)), 0))

    out = pl.pallas_call(
        functools.partial(_kv_gather_kernel, depth=depth, nseq=nseq),
        grid=(depth, nt),
        in_specs=[in_spec(l) for l in range(depth)],
        out_specs=pl.BlockSpec((1, nseq, N_HEADS, HEAD_DIM, LANES), lambda d, t: (d, 0, 0, 0, t)),
        out_shape=jax.ShapeDtypeStruct((depth, nseq, N_HEADS, HEAD_DIM, seq_len), F32),
        compiler_params=_params(("arbitrary", "arbitrary")),
        name="kv_gather",
    )(*views)
    return jnp.transpose(out, (0, 1, 4, 2, 3))


def kernel(x_prompt, x_sample, cache_k, cache_v, cache_logf, state_conv, state_ssm_re, state_ssm_im, page_table, p_prompt, p_sample, norm_mix_pre, norm_mix_post, norm_ffn_pre, norm_ffn_post, w_in, b_forget, conv_w, ssm_a_re, ssm_a_im, ssm_log_dt, ssm_b_re, ssm_b_im, ssm_c_re, ssm_c_im, ssm_d, w_ssm_glu, w_out, w_ffn_gate, w_ffn_up, w_ffn_down, w_ple_gate, w_ple_proj):
    depth = w_in.shape[0]
    bp, seq_len, _ = x_prompt.shape
    bs = x_sample.shape[0]
    n_p = bp * seq_len
    tm = min(512, seq_len)
    tq = min(512, seq_len)
    chunks_per_seq = seq_len // SSM_CHUNK
    n_chunks = n_p // SSM_CHUNK
    tmc = min(512, n_chunks)
    tm_fused = n_p // bs
    fuse_decode = (n_p % bs == 0 and tm_fused % 16 == 0 and tm_fused <= 512
                   and page_table.shape[1] % PAGES_PER_STEP == 0)

    gate_cols = 3 * CONV_DIM + SSM_DIM
    wg = w_in[:, :, 0:gate_cols].astype(BF16)
    wqkv = w_in[:, :, gate_cols:gate_cols + 3 * ATT_DIM].astype(BF16)
    wfl = jnp.pad(w_in[:, :, gate_cols + 3 * ATT_DIM:], ((0, 0), (0, 0), (0, LANES - N_HEADS))).astype(BF16)
    bfl = jnp.pad(b_forget, ((0, 0), (0, LANES - N_HEADS)))[:, None, :]
    wglu = w_ssm_glu.astype(BF16)
    wout = w_out.astype(BF16)
    wgate = w_ffn_gate.astype(BF16)
    wup = w_ffn_up.astype(BF16)
    wdown = w_ffn_down.astype(BF16)
    wpg = w_ple_gate.astype(BF16)
    wpp = w_ple_proj.astype(BF16)
    g_pre = norm_mix_pre[:, None, :]
    g_post = norm_mix_post[:, None, :]
    g_fpre = norm_ffn_pre[:, None, :]
    g_fpost = norm_ffn_post[:, None, :]
    d_row = ssm_d[:, None, :]

    w_g, w_ct, bd, b1, c1t, lam = _ssm_prep(ssm_a_re, ssm_a_im, ssm_log_dt, ssm_b_re, ssm_b_im,
                                            ssm_c_re, ssm_c_im)

    cache_kt = jnp.transpose(cache_k, (0, 1, 3, 4, 2))
    cache_vt = jnp.transpose(cache_v, (0, 1, 3, 4, 2))
    cache_lft = jnp.transpose(cache_logf, (0, 1, 3, 2))

    xp = x_prompt.reshape(n_p, D_MODEL)
    xs = x_sample.reshape(bs, D_MODEL)
    pe_prompt = p_prompt.reshape(depth, n_p, PLE_DIM)
    pe_sample = p_sample.reshape(depth, bs, PLE_DIM)
    outs_p = [[] for _ in range(6)]
    outs_s = [[] for _ in range(6)]
    for i in range(depth):
        (convo, uflat_f, uflat_b, q, k, v, kb, vb, lf, ct, crow, conv_new) = _inproj_prompt(
            xp, g_pre, wg, wqkv, wfl, bfl, conv_w, i, seq_len=seq_len, tm=tm)
        gstate = _ssm_chunk_state(uflat_b, w_g, i, tm=tmc, tn=1024)
        hin, hlast = _ssm_scan(gstate, lam, i, chunks_per_seq=chunks_per_seq)
        ssmy = _ssm_output(uflat_b, hin, uflat_f, bd, w_ct, d_row, i, tm=tmc)
        att = _fox_prompt(q, kb, vb, ct, crow, seq_len=seq_len, tq=tq)
        outs_p[0].append(k)
        outs_p[1].append(v)
        outs_p[2].append(lf.reshape(bp, seq_len, N_HEADS))
        outs_p[3].append(conv_new)
        hl = hlast.reshape(bp, 2, SSM_GROUPS, SSM_STATE)
        outs_p[4].append(hl[:, 0])
        outs_p[5].append(hl[:, 1])

        (convo_s, su_s, u_s, q_s, k_s, v_s, lf_s, sn_s) = _inproj_sample(
            xs, g_pre, wg, wqkv, wfl, bfl, conv_w, state_conv[i, :, 0], state_conv[i, :, 1], i)
        ssmy_s, hre_s, him_s = _ssm_step(
            su_s, state_ssm_re[i].reshape(bs, STATE_W), state_ssm_im[i].reshape(bs, STATE_W),
            b1, lam, c1t, d_row, i)
        post_w = (wglu, wout, g_post, g_fpre, wgate, wup, wdown, g_fpost, wpg, wpp)
        if fuse_decode:
            xp, att_s = _post_decode(xp, convo, ssmy, att, pe_prompt, post_w,
                                     page_table, q_s, v_s, sn_s, lf_s, cache_kt, cache_vt, cache_lft, i,
                                     tm=tm_fused)
        else:
            xp = _post(xp, convo, ssmy.reshape(n_p, SSM_DIM), att, pe_prompt, *post_w, i, tm=tm)
            att_s = _fox_sample(page_table, q_s, v_s, sn_s, lf_s, cache_kt, cache_vt, cache_lft, layer=i)
        xs = _post(xs, convo_s, ssmy_s, att_s, pe_sample,
                   wglu, wout, g_post, g_fpre, wgate, wup, wdown, g_fpost, wpg, wpp, i, tm=bs)
        outs_s[0].append(k_s.reshape(bs, 1, N_HEADS, HEAD_DIM))
        outs_s[1].append(v_s.reshape(bs, 1, N_HEADS, HEAD_DIM))
        outs_s[2].append(lf_s[:, 0:N_HEADS].reshape(bs, 1, N_HEADS))
        outs_s[3].append(jnp.stack([state_conv[i, :, 1], u_s], axis=1))
        outs_s[4].append(hre_s.reshape(bs, SSM_GROUPS, SSM_STATE))
        outs_s[5].append(him_s.reshape(bs, SSM_GROUPS, SSM_STATE))

    k_p = _kv_gather(outs_p[0], nseq=bp, seq_len=seq_len)
    v_p = _kv_gather(outs_p[1], nseq=bp, seq_len=seq_len)
    lf_p, conv_p, re_p, im_p = [jnp.stack(a) for a in outs_p[2:]]
    k_s, v_s, lf_s, conv_s, re_s, im_s = [jnp.stack(a) for a in outs_s]
    return (xp.reshape(bp, seq_len, D_MODEL), xs.reshape(bs, 1, D_MODEL),
            k_p, v_p, lf_p, conv_p, re_p, im_p, k_s, v_s, lf_s, conv_s, re_s, im_s)
```

```python
import functools
import math

import jax
import jax.numpy as jnp
from jax import lax
from jax.experimental import pallas as pl
from jax.experimental.pallas import tpu as pltpu

F32 = jnp.float32
BF16 = jnp.bfloat16
HIGHEST = lax.Precision.HIGHEST

D_MODEL = 1024
CONV_DIM = 256
SSM_DIM = 256
ATT_DIM = 512
N_HEADS = 8
HEAD_DIM = 64
SSM_GROUPS = 16
SSM_GROUP = 16
SSM_STATE = 64
D_FF = 2816
PLE_DIM = 256
CONV_K = 3
EPS = 1e-6
LOG2E = math.log2(math.e)
LANES = 128
SSM_CHUNK = 16
FF_CHUNK = 256
STATE_W = SSM_GROUPS * SSM_STATE
CHUNK_W = SSM_CHUNK * SSM_DIM
VMEM_LIMIT = 60 * 1024 * 1024


def _params(sem, vmem=VMEM_LIMIT):
    return pltpu.CompilerParams(dimension_semantics=sem, vmem_limit_bytes=vmem)


def _rms(x, g):
    return x * lax.rsqrt(jnp.mean(x * x, axis=-1, keepdims=True) + EPS) * g


def _sigmoid(x):
    return 1.0 / (1.0 + jnp.exp(-x))


def _log_sigmoid(x):
    return jnp.minimum(x, 0.0) - jnp.log1p(jnp.exp(-jnp.abs(x)))


def _gelu_tanh(x):
    return 0.5 * x * (1.0 + jnp.tanh(math.sqrt(2.0 / math.pi) * (x + 0.044715 * (x * x * x))))


def _dot(a, b, **kw):
    return jnp.dot(a, b, preferred_element_type=F32, **kw)


def _dot_nt(a, b, **kw):
    return lax.dot_general(a, b, (((1,), (1,)), ((), ())), preferred_element_type=F32, **kw)


def _split3(x):
    hi = x.astype(BF16)
    r1 = x - hi.astype(F32)
    mid = r1.astype(BF16)
    lo = (r1 - mid.astype(F32)).astype(BF16)
    return hi, mid, lo


def _layer_spec(arr, layer):
    zeros = (0,) * (arr.ndim - 1)
    return pl.BlockSpec((1,) + arr.shape[1:], lambda *_: (layer,) + zeros, pipeline_mode=pl.Buffered(1))


def _ssm_prep_kernel(are_ref, aim_ref, ldt_ref, btr_ref, bti_ref, ctr_ref, cti_ref,
                     wg_ref, wct_ref, bd_ref, b1_ref, c1t_ref, lam_ref, pw_scr):
    t = pl.program_id(1)
    a_re = are_ref[0]
    a_im = aim_ref[0]
    dt = jnp.exp(ldt_ref[0])
    mag = jnp.exp(a_re * dt)
    lr = mag * jnp.cos(a_im * dt)
    li = mag * jnp.sin(a_im * dt)
    den = a_re * a_re + a_im * a_im
    xr = lr - 1.0
    cfr = (xr * a_re + li * a_im) / den
    cfi = (li * a_re - xr * a_im) / den
    btr = btr_ref[0]
    bti = bti_ref[0]
    bbr = cfr * btr - cfi * bti
    bbi = cfr * bti + cfi * btr
    ctr = ctr_ref[0]
    cti = cti_ref[0]

    rows = SSM_GROUPS * SSM_GROUP
    grp_r = lax.broadcasted_iota(jnp.int32, (rows, STATE_W), 0) // SSM_GROUP
    grp_c = lax.broadcasted_iota(jnp.int32, (rows, STATE_W), 1) // SSM_STATE
    mask = (grp_r == grp_c).astype(F32)

    def blockdiag(x):
        return jnp.concatenate([x] * SSM_GROUPS, axis=0) * mask

    @pl.when(t == 0)
    def _():
        pw_scr[0:1, :] = jnp.ones((1, STATE_W), F32)
        pw_scr[1:2, :] = jnp.zeros((1, STATE_W), F32)
        lam_ref[0] = jnp.zeros((8, STATE_W), F32)
        lam_ref[0, 0:1, :] = lr
        lam_ref[0, 1:2, :] = li
        b1_ref[0, :, 0:STATE_W] = blockdiag(bbr)
        b1_ref[0, :, STATE_W:] = blockdiag(bbi)
        c1t_ref[0, :, 0:STATE_W] = blockdiag(ctr)
        c1t_ref[0, :, STATE_W:] = blockdiag(-cti)

    pr = pw_scr[0:1, :]
    pi = pw_scr[1:2, :]
    nr = pr * lr - pi * li
    ni = pr * li + pi * lr
    pw_scr[0:1, :] = nr
    pw_scr[1:2, :] = ni

    @pl.when(t == SSM_CHUNK - 1)
    def _():
        lam_ref[0, 2:3, :] = nr
        lam_ref[0, 3:4, :] = ni

    wg_ref[0, :, 0:STATE_W] = blockdiag(bbr * pr - bbi * pi).astype(BF16)
    wg_ref[0, :, STATE_W:] = blockdiag(bbr * pi + bbi * pr).astype(BF16)
    wct_ref[0, :, 0:STATE_W] = blockdiag(ctr * nr - cti * ni).astype(BF16)
    wct_ref[0, :, STATE_W:] = blockdiag(-(ctr * ni + cti * nr)).astype(BF16)
    pb = jnp.concatenate([blockdiag(bbr), blockdiag(bbi)], axis=1)
    pa = jnp.concatenate([blockdiag(ctr * pr - cti * pi), blockdiag(-(ctr * pi + cti * pr))], axis=1)
    pb_hi = pb.astype(BF16)
    pb_lo = (pb - pb_hi.astype(F32)).astype(BF16)
    pa_hi = pa.astype(BF16)
    pa_lo = (pa - pa_hi.astype(F32)).astype(BF16)
    bd_ref[0, 0] = (_dot_nt(pb_hi, pa_hi) + _dot_nt(pb_hi, pa_lo) + _dot_nt(pb_lo, pa_hi)).astype(BF16)


def _ssm_prep(a_re, a_im, log_dt, b_re, b_im, c_re, c_im):
    depth = a_re.shape[0]
    g, p, c, l = SSM_GROUPS, SSM_STATE, SSM_GROUP, SSM_CHUNK
    rows = g * c
    row1 = lambda x: x.reshape(depth, 1, g * p)
    chan = lambda x: x.reshape(depth, c, g * p)
    args = (row1(a_re), row1(a_im), row1(jnp.repeat(log_dt, p, axis=-1)),
            chan(jnp.transpose(b_re, (0, 3, 1, 2))), chan(jnp.transpose(b_im, (0, 3, 1, 2))),
            chan(jnp.transpose(c_re, (0, 2, 1, 3))), chan(jnp.transpose(c_im, (0, 2, 1, 3))))
    spec_row = pl.BlockSpec((1, 1, g * p), lambda i, t: (i, 0, 0))
    spec_chan = pl.BlockSpec((1, c, g * p), lambda i, t: (i, 0, 0))
    out_shapes = (
        jax.ShapeDtypeStruct((depth, l * rows, 2 * g * p), BF16),
        jax.ShapeDtypeStruct((depth, l * rows, 2 * g * p), BF16),
        jax.ShapeDtypeStruct((depth, l, rows, rows), BF16),
        jax.ShapeDtypeStruct((depth, rows, 2 * g * p), F32),
        jax.ShapeDtypeStruct((depth, rows, 2 * g * p), F32),
        jax.ShapeDtypeStruct((depth, 8, g * p), F32),
    )
    out_specs = (
        pl.BlockSpec((1, rows, 2 * g * p), lambda i, t: (i, l - 1 - t, 0)),
        pl.BlockSpec((1, rows, 2 * g * p), lambda i, t: (i, t, 0)),
        pl.BlockSpec((1, 1, rows, rows), lambda i, t: (i, t, 0, 0)),
        pl.BlockSpec((1, rows, 2 * g * p), lambda i, t: (i, 0, 0)),
        pl.BlockSpec((1, rows, 2 * g * p), lambda i, t: (i, 0, 0)),
        pl.BlockSpec((1, 8, g * p), lambda i, t: (i, 0, 0)),
    )
    return pl.pallas_call(
        _ssm_prep_kernel,
        grid=(depth, l),
        in_specs=[spec_row, spec_row, spec_row, spec_chan, spec_chan, spec_chan, spec_chan],
        out_specs=out_specs,
        out_shape=out_shapes,
        scratch_shapes=[pltpu.VMEM((2, g * p), F32)],
        compiler_params=_params(("arbitrary", "arbitrary")),
        name="ssm_prep",
    )(*args)


def _inproj_prompt_kernel(x_ref, g_ref, wg_ref, wqkv_ref, wfl_ref, bfl_ref, cw_ref,
                          convo_ref, su_ref, sub_ref, q_ref, k_ref, v_ref, kb_ref, vb_ref,
                          lf_ref, ct_ref, crow_ref, convnew_ref, ubuf, ccar, su_lo, su_hi, *, tiles_per_seq, tm):
    i = pl.program_id(0)

    @pl.when(i % tiles_per_seq == 0)
    def _():
        ubuf[0:8, :] = jnp.zeros((8, CONV_DIM), F32)
        ccar[...] = jnp.zeros_like(ccar)

    h = _rms(x_ref[...], g_ref[0]).astype(BF16)
    z = _dot(h, wg_ref[0])
    cb = z[:, 0:CONV_DIM]
    cc = z[:, CONV_DIM:2 * CONV_DIM]
    cv = z[:, 2 * CONV_DIM:3 * CONV_DIM]
    for half, scr in enumerate((su_lo, su_hi)):
        scr[...] = z[:, 3 * CONV_DIM + half * LANES:3 * CONV_DIM + (half + 1) * LANES]
        for s in range(SSM_CHUNK):
            piece = scr[pl.ds(s, tm // SSM_CHUNK, stride=SSM_CHUNK), :]
            cols = slice(s * SSM_DIM + half * LANES, s * SSM_DIM + (half + 1) * LANES)
            su_ref[:, cols] = piece
            sub_ref[:, cols] = piece.astype(BF16)
    u = cc * cv
    ubuf[8:8 + tm, :] = u
    u1 = ubuf[7:7 + tm, :]
    u2 = ubuf[6:6 + tm, :]
    cw = cw_ref[0]
    y = cw[0:1] * u2 + cw[1:2] * u1 + cw[2:3] * u
    convo_ref[...] = (cb * y).astype(BF16)
    convnew_ref[0] = u[tm - 2:tm, :]
    ubuf[0:8, :] = u[tm - 8:tm, :]

    zz = _dot(h, wqkv_ref[0])
    q_ref[...] = (zz[:, 0:ATT_DIM] * (HEAD_DIM ** -0.5 * LOG2E)).astype(BF16)
    k = zz[:, ATT_DIM:2 * ATT_DIM]
    v = zz[:, 2 * ATT_DIM:]
    k_ref[...] = k
    v_ref[...] = v
    kb_ref[...] = k.astype(BF16)
    vb_ref[...] = v.astype(BF16)

    fl = _dot(h, wfl_ref[0]) + bfl_ref[0]
    lane = lax.broadcasted_iota(jnp.int32, fl.shape, 1)
    lf = jnp.where(lane < N_HEADS, _log_sigmoid(fl), 0.0)
    lf_ref[...] = lf[:, 0:N_HEADS]
    row = lax.broadcasted_iota(jnp.int32, (LANES, LANES), 0)
    col = lax.broadcasted_iota(jnp.int32, (LANES, LANES), 1)
    tri = (row >= col).astype(F32).astype(BF16)
    carry = ccar[...]
    for r0 in range(0, tm, LANES):
        hi, mid, lo = _split3(lf[r0:r0 + LANES, :])
        c = _dot(tri, hi) + _dot(tri, mid) + _dot(tri, lo) + carry
        carry = c[LANES - 1:LANES, :]
        crow_ref[r0:r0 + LANES, :] = c
        ct_ref[:, r0:r0 + LANES] = c.T[0:N_HEADS, :]
    ccar[...] = carry


def _inproj_prompt(x, g_pre, wg, wqkv, wfl, bfl, conv_w, layer, *, seq_len, tm):
    n = x.shape[0]
    nt = n // tm
    tiles_per_seq = seq_len // tm
    nseq = n // seq_len
    tc = tm // SSM_CHUNK
    row = lambda w: pl.BlockSpec((tm, w), lambda i: (i, 0))
    chunk_rows = pl.BlockSpec((tc, CHUNK_W), lambda i: (i, 0))
    out_shapes = (
        jax.ShapeDtypeStruct((n, CONV_DIM), BF16),
        jax.ShapeDtypeStruct((n // SSM_CHUNK, CHUNK_W), F32),
        jax.ShapeDtypeStruct((n // SSM_CHUNK, CHUNK_W), BF16),
        jax.ShapeDtypeStruct((n, ATT_DIM), BF16),
        jax.ShapeDtypeStruct((n, ATT_DIM), F32),
        jax.ShapeDtypeStruct((n, ATT_DIM), F32),
        jax.ShapeDtypeStruct((n, ATT_DIM), BF16),
        jax.ShapeDtypeStruct((n, ATT_DIM), BF16),
        jax.ShapeDtypeStruct((n, N_HEADS), F32),
        jax.ShapeDtypeStruct((N_HEADS, n), F32),
        jax.ShapeDtypeStruct((n, LANES), F32),
        jax.ShapeDtypeStruct((nseq, CONV_K - 1, CONV_DIM), F32),
    )
    out_specs = (
        row(CONV_DIM), chunk_rows, chunk_rows, row(ATT_DIM), row(ATT_DIM), row(ATT_DIM),
        row(ATT_DIM), row(ATT_DIM), row(N_HEADS),
        pl.BlockSpec((N_HEADS, tm), lambda i: (0, i)),
        row(LANES),
        pl.BlockSpec((1, CONV_K - 1, CONV_DIM), lambda i: (i // tiles_per_seq, 0, 0)),
    )
    weights = (g_pre, wg, wqkv, wfl, bfl, conv_w)
    return pl.pallas_call(
        functools.partial(_inproj_prompt_kernel, tiles_per_seq=tiles_per_seq, tm=tm),
        grid=(nt,),
        in_specs=[row(D_MODEL)] + [_layer_spec(w, layer) for w in weights],
        out_specs=out_specs,
        out_shape=out_shapes,
        scratch_shapes=[pltpu.VMEM((tm + 8, CONV_DIM), F32), pltpu.VMEM((1, LANES), F32),
                        pltpu.VMEM((tm, LANES), F32), pltpu.VMEM((tm, LANES), F32)],
        compiler_params=_params(("arbitrary",)),
        name="inproj_prompt",
    )(x, *weights)


def _inproj_sample_kernel(x_ref, g_ref, wg_ref, wqkv_ref, wfl_ref, bfl_ref, cw_ref, b0_ref, b1_ref,
                          convo_ref, su_ref, u_ref, q_ref, k_ref, v_ref, lf_ref, sn_ref):
    h = _rms(x_ref[...], g_ref[0]).astype(BF16)
    z = _dot(h, wg_ref[0])
    cb = z[:, 0:CONV_DIM]
    u = z[:, CONV_DIM:2 * CONV_DIM] * z[:, 2 * CONV_DIM:3 * CONV_DIM]
    su_ref[...] = z[:, 3 * CONV_DIM:]
    u_ref[...] = u
    cw = cw_ref[0]
    y = cw[0:1] * b0_ref[...] + cw[1:2] * b1_ref[...] + cw[2:3] * u
    convo_ref[...] = (cb * y).astype(BF16)
    zz = _dot(h, wqkv_ref[0])
    q = zz[:, 0:ATT_DIM] * (HEAD_DIM ** -0.5)
    k = zz[:, ATT_DIM:2 * ATT_DIM]
    q_ref[...] = q
    k_ref[...] = k
    v_ref[...] = zz[:, 2 * ATT_DIM:]
    fl = _dot(h, wfl_ref[0]) + bfl_ref[0]
    lane = lax.broadcasted_iota(jnp.int32, fl.shape, 1)
    lf_ref[...] = jnp.where(lane < N_HEADS, _log_sigmoid(fl), 0.0)
    hd_row = lax.broadcasted_iota(jnp.int32, (ATT_DIM, LANES), 0) // HEAD_DIM
    hd_col = lax.broadcasted_iota(jnp.int32, (ATT_DIM, LANES), 1)
    sn_ref[...] = _dot(q * k, (hd_row == hd_col).astype(F32), precision=HIGHEST)


def _inproj_sample(x, g_pre, wg, wqkv, wfl, bfl, conv_w, buf0, buf1, layer):
    n = x.shape[0]
    full = lambda s: pl.BlockSpec(s, lambda i: (0,) * len(s))
    weights = (g_pre, wg, wqkv, wfl, bfl, conv_w)
    out_shapes = (
        jax.ShapeDtypeStruct((n, CONV_DIM), BF16),
        jax.ShapeDtypeStruct((n, SSM_DIM), F32),
        jax.ShapeDtypeStruct((n, CONV_DIM), F32),
        jax.ShapeDtypeStruct((n, ATT_DIM), F32),
        jax.ShapeDtypeStruct((n, ATT_DIM), F32),
        jax.ShapeDtypeStruct((n, ATT_DIM), F32),
        jax.ShapeDtypeStruct((n, LANES), F32),
        jax.ShapeDtypeStruct((n, LANES), F32),
    )
    args = (x,) + weights + (buf0, buf1)
    return pl.pallas_call(
        _inproj_sample_kernel,
        grid=(1,),
        in_specs=[full(x.shape)] + [_layer_spec(w, layer) for w in weights] + [full(buf0.shape), full(buf1.shape)],
        out_specs=tuple(full(s.shape) for s in out_shapes),
        out_shape=out_shapes,
        compiler_params=_params(("arbitrary",)),
        name="inproj_sample",
    )(*args)


def _ssm_chunk_state_kernel(u_ref, w_ref, o_ref):
    o_ref[...] = _dot(u_ref[...], w_ref[0])


def _ssm_chunk_state(uflat, w_g, layer, *, tm, tn):
    m, k = uflat.shape
    n = w_g.shape[2]
    return pl.pallas_call(
        _ssm_chunk_state_kernel,
        grid=(n // tn, m // tm),
        in_specs=[pl.BlockSpec((tm, k), lambda j, i: (i, 0)),
                  pl.BlockSpec((1, k, tn), lambda j, i: (layer, 0, j))],
        out_specs=pl.BlockSpec((tm, tn), lambda j, i: (i, j)),
        out_shape=jax.ShapeDtypeStruct((m, n), F32),
        compiler_params=_params(("arbitrary", "arbitrary")),
        name="ssm_chunk_state",
    )(uflat, w_g)


def _ssm_scan_kernel(g_ref, lam_ref, hin_ref, hlast_ref, *, n_chunks):
    lr = lam_ref[0, 2:3, :]
    li = lam_ref[0, 3:4, :]

    def body(c, carry):
        h_re, h_im = carry
        hin_ref[pl.ds(c, 1), 0:STATE_W] = h_re
        hin_ref[pl.ds(c, 1), STATE_W:] = h_im
        g_re = g_ref[pl.ds(c, 1), 0:STATE_W]
        g_im = g_ref[pl.ds(c, 1), STATE_W:]
        return lr * h_re - li * h_im + g_re, lr * h_im + li * h_re + g_im

    zero = jnp.zeros((1, STATE_W), F32)
    h_re, h_im = lax.fori_loop(0, n_chunks, body, (zero, zero))
    hlast_ref[0, :, 0:STATE_W] = h_re
    hlast_ref[0, :, STATE_W:] = h_im


def _ssm_scan(gstate, lam, layer, *, chunks_per_seq):
    m, w = gstate.shape
    nseq = m // chunks_per_seq
    return pl.pallas_call(
        functools.partial(_ssm_scan_kernel, n_chunks=chunks_per_seq),
        grid=(nseq,),
        in_specs=[pl.BlockSpec((chunks_per_seq, w), lambda b: (b, 0)),
                  pl.BlockSpec((1, 8, STATE_W), lambda b: (layer, 0, 0))],
        out_specs=(pl.BlockSpec((chunks_per_seq, w), lambda b: (b, 0)),
                   pl.BlockSpec((1, 1, w), lambda b: (b, 0, 0))),
        out_shape=(jax.ShapeDtypeStruct((m, w), F32), jax.ShapeDtypeStruct((nseq, 1, w), F32)),
        compiler_params=_params(("arbitrary",)),
        name="ssm_scan",
    )(gstate, lam)


def _ssm_output_kernel(ub_ref, hin_ref, uf_ref, bd_ref, wct_ref, d_ref, y_ref, hb_scr):
    j = pl.program_id(1)
    cw = SSM_DIM

    @pl.when(j == 0)
    def _():
        hb_scr[...] = hin_ref[...].astype(BF16)

    y_ref[...] = _dot_nt(hb_scr[...], wct_ref[0]) + d_ref[0] * uf_ref[...]
    for s in range(SSM_CHUNK):
        @pl.when(s <= j)
        def _():
            y_ref[...] += _dot(ub_ref[:, s * cw:(s + 1) * cw], bd_ref[0, j - s])


def _ssm_output(uflat_b, hin, uflat_f, bd, wct, d_row, layer, *, tm):
    m, k = uflat_b.shape
    ks = hin.shape[1]
    cw = SSM_DIM
    return pl.pallas_call(
        _ssm_output_kernel,
        grid=(m // tm, SSM_CHUNK),
        in_specs=[pl.BlockSpec((tm, k), lambda i, j: (i, 0)),
                  pl.BlockSpec((tm, ks), lambda i, j: (i, 0)),
                  pl.BlockSpec((tm, cw), lambda i, j: (i, j)),
                  pl.BlockSpec((1, SSM_CHUNK, cw, cw), lambda i, j: (layer, 0, 0, 0)),
                  pl.BlockSpec((1, cw, ks), lambda i, j: (layer, j, 0)),
                  pl.BlockSpec((1, 1, cw), lambda i, j: (layer, 0, 0))],
        out_specs=pl.BlockSpec((tm, cw), lambda i, j: (i, j)),
        out_shape=jax.ShapeDtypeStruct((m, k), F32),
        scratch_shapes=[pltpu.VMEM((tm, ks), BF16)],
        compiler_params=_params(("arbitrary", "arbitrary")),
        name="ssm_output",
    )(uflat_b, hin, uflat_f, bd, wct, d_row)


def _ssm_step_kernel(u_ref, hre_ref, him_ref, b1_ref, lam_ref, c1t_ref, d_ref,
                     y_ref, ore_ref, oim_ref):
    u = u_ref[...]
    lr = lam_ref[0, 0:1, :]
    li = lam_ref[0, 1:2, :]
    h_re = hre_ref[...]
    h_im = him_ref[...]
    bu = _dot(u, b1_ref[0], precision=HIGHEST)
    n_re = lr * h_re - li * h_im + bu[:, 0:STATE_W]
    n_im = lr * h_im + li * h_re + bu[:, STATE_W:]
    ore_ref[...] = n_re
    oim_ref[...] = n_im
    y = (_dot_nt(n_re, c1t_ref[0, :, 0:STATE_W], precision=HIGHEST)
         + _dot_nt(n_im, c1t_ref[0, :, STATE_W:], precision=HIGHEST))
    y_ref[...] = y + d_ref[0] * u


def _ssm_step(u, h_re, h_im, b1, lam, c1t, d, layer):
    n = u.shape[0]
    full = lambda s: pl.BlockSpec(s, lambda i: (0,) * len(s))
    lay = lambda s: pl.BlockSpec((1,) + s[1:], lambda i: (layer,) + (0,) * (len(s) - 1))
    out_shapes = (jax.ShapeDtypeStruct((n, SSM_DIM), F32), jax.ShapeDtypeStruct((n, STATE_W), F32),
                  jax.ShapeDtypeStruct((n, STATE_W), F32))
    return pl.pallas_call(
        _ssm_step_kernel,
        grid=(1,),
        in_specs=[full(u.shape), full(h_re.shape), full(h_im.shape), lay(b1.shape), lay(lam.shape),
                  lay(c1t.shape), lay(d.shape)],
        out_specs=tuple(full(s.shape) for s in out_shapes),
        out_shape=out_shapes,
        compiler_params=_params(("arbitrary",)),
        name="ssm_step",
    )(u, h_re, h_im, b1, lam, c1t, d)


def _fox_prompt_kernel(qi_ref, ki_ref, q_ref, k_ref, v_ref, cq_ref, ck_ref, o_ref,
                       m_scr, l_scr, acc_scr, cq_scr, *, tq, tk, rb, kpt):
    qi = qi_ref[pl.program_id(2)]
    kg = ki_ref[pl.program_id(2)]

    @pl.when(kg == 0)
    def _():
        m_scr[...] = jnp.full(m_scr.shape, -jnp.inf, F32)
        l_scr[...] = jnp.zeros_like(l_scr)
        acc_scr[...] = jnp.zeros_like(acc_scr)
        src_lane = lax.broadcasted_iota(jnp.int32, (LANES, 2 * LANES), 0)
        dst_head = lax.broadcasted_iota(jnp.int32, (LANES, 2 * LANES), 1) // LANES
        pick = (src_lane == 2 * pl.program_id(1) + dst_head).astype(F32).astype(BF16)
        hi, mid, lo = _split3(cq_ref[...])
        spread = (_dot(hi, pick) + _dot(mid, pick) + _dot(lo, pick)) * LOG2E
        for hh in range(2):
            cq_scr[hh] = spread[:, hh * LANES:(hh + 1) * LANES]

    def step(on_diagonal, sub):
        keys = slice(sub * tk, (sub + 1) * tk)
        k2 = k_ref[keys, :]
        v2 = v_ref[keys, :]
        lane = lax.broadcasted_iota(jnp.int32, (rb, LANES), 1)
        keep = [(lane < HEAD_DIM).astype(F32).astype(BF16), (lane >= HEAD_DIM).astype(F32).astype(BF16)]
        lane_o = lane < HEAD_DIM
        ck2 = [ck_ref[0, hh:hh + 1, keys] * LOG2E for hh in range(2)]
        nkeys = [min(tk, r0 + rb) if on_diagonal else tk for r0 in range(0, tq, rb)]
        scores = [[_dot_nt(q_ref[r0:r0 + rb, :] * keep[hh], k2[0:nkeys[bi], :]) for hh in range(2)]
                  for bi, r0 in enumerate(range(0, tq, rb))]
        for bi, r0 in enumerate(range(0, tq, rb)):
            rows = slice(r0, r0 + rb)
            reps = nkeys[bi] // LANES
            if on_diagonal:
                rel = (lax.broadcasted_iota(jnp.int32, (rb, LANES), 1)
                       - lax.broadcasted_iota(jnp.int32, (rb, LANES), 0))
            alphas = []
            pvs = []
            for hh in range(2):
                sc = []
                for c in range(reps):
                    cols = slice(c * LANES, (c + 1) * LANES)
                    s_c = scores[bi][hh][:, cols] - ck2[hh][:, cols]
                    if on_diagonal:
                        s_c = jnp.where(rel <= r0 - c * LANES, s_c, -jnp.inf)
                    sc.append(s_c)
                mx = sc[0]
                for c in range(1, reps):
                    mx = jnp.maximum(mx, sc[c])
                cq = cq_scr[hh, rows, :]
                m_prev = m_scr[hh, rows, :]
                m_new = jnp.maximum(m_prev, jnp.max(mx, axis=-1, keepdims=True) + cq)
                t = m_new - cq
                ps = [jnp.exp2(s_c - t) for s_c in sc]
                tot = ps[0]
                for c in range(1, reps):
                    tot = tot + ps[c]
                alpha = jnp.exp2(m_prev - m_new)
                l_scr[hh, rows, :] = alpha * l_scr[hh, rows, :] + jnp.sum(tot, axis=-1, keepdims=True)
                m_scr[hh, rows, :] = m_new
                alphas.append(alpha)
                p = jnp.concatenate([p_c.astype(BF16) for p_c in ps], axis=1)
                pvs.append(_dot(p, v2[0:nkeys[bi], :]))
            acc_scr[rows, :] = (jnp.where(lane_o, alphas[0], alphas[1]) * acc_scr[rows, :]
                                + jnp.where(lane_o, pvs[0], pvs[1]))

    def finish():
        lane_o = lax.broadcasted_iota(jnp.int32, (tq, LANES), 1) < HEAD_DIM
        o_ref[...] = (acc_scr[...] / jnp.where(lane_o, l_scr[0], l_scr[1])).astype(o_ref.dtype)

    @pl.when(kg * kpt + kpt - 1 < qi)
    def _():
        for sub in range(kpt):
            step(False, sub)

    for diag in range(kpt):
        @pl.when(qi == kg * kpt + diag)
        def _():
            for sub in range(diag):
                step(False, sub)
            step(True, diag)
            finish()


def _fox_prompt(q, kb, vb, ct, cq, *, seq_len, tq):
    n = q.shape[0]
    nseq = n // seq_len
    nq = seq_len // tq
    hp = N_HEADS // 2
    ck = ct.reshape(hp, 2, n)
    kpt = 2 if nq % 2 == 0 else 1
    ng = nq // kpt
    pairs = [(i, g) for i in range(nq) for g in range(i // kpt + 1)]
    qi_tab = jnp.asarray([p[0] for p in pairs], jnp.int32)
    ki_tab = jnp.asarray([p[1] for p in pairs], jnp.int32)
    q_idx = lambda b, h, t, qt, kt: (b * nq + qt[t], h)
    kv_idx = lambda b, h, t, qt, kt: (b * ng + kt[t], h)
    grid_spec = pltpu.PrefetchScalarGridSpec(
        num_scalar_prefetch=2,
        grid=(nseq, hp, len(pairs)),
        in_specs=[pl.BlockSpec((tq, LANES), q_idx),
                  pl.BlockSpec((kpt * tq, LANES), kv_idx),
                  pl.BlockSpec((kpt * tq, LANES), kv_idx),
                  pl.BlockSpec((tq, LANES), lambda b, h, t, qt, kt: (b * nq + qt[t], 0)),
                  pl.BlockSpec((1, 2, kpt * tq), lambda b, h, t, qt, kt: (h, 0, b * ng + kt[t]))],
        out_specs=pl.BlockSpec((tq, LANES), q_idx),
        scratch_shapes=[pltpu.VMEM((2, tq, LANES), F32), pltpu.VMEM((2, tq, LANES), F32),
                        pltpu.VMEM((tq, LANES), F32), pltpu.VMEM((2, tq, LANES), F32)],
    )
    return pl.pallas_call(
        functools.partial(_fox_prompt_kernel, tq=tq, tk=tq, rb=min(256, tq), kpt=kpt),
        grid_spec=grid_spec,
        out_shape=jax.ShapeDtypeStruct((n, ATT_DIM), BF16),
        compiler_params=_params(("arbitrary", "arbitrary", "arbitrary")),
        name="fox_prompt",
    )(qi_tab, ki_tab, q, kb, vb, cq, ck)


PAGES_PER_STEP = 8


def _decode_init(vn_ref, sn_ref, lfn_ref, m_scr, l_scr, car_scr, acc_scr, page_size):
    eye = (lax.broadcasted_iota(jnp.int32, (N_HEADS, LANES), 0)
           == lax.broadcasted_iota(jnp.int32, (N_HEADS, LANES), 1)).astype(F32)

    def to_col(row):
        return jnp.sum(eye * row, axis=-1, keepdims=True)

    m_scr[...] = to_col(sn_ref[0])
    l_scr[...] = jnp.ones_like(l_scr)
    car_scr[...] = to_col(lfn_ref[0])
    lane0 = lax.broadcasted_iota(jnp.int32, (HEAD_DIM, page_size), 1) == 0
    for h in range(N_HEADS):
        vcol = vn_ref[0, h * HEAD_DIM:(h + 1) * HEAD_DIM, :]
        acc_scr[h] = jnp.where(lane0, vcol, 0.0)


def _decode_finish(o_ref, l_scr, acc_scr):
    inv = 1.0 / l_scr[...]
    for h in range(N_HEADS):
        o_ref[0, h] = jnp.sum(acc_scr[h], axis=-1, keepdims=True) * inv[h:h + 1, :]


def _decode_update(q_ref, k_refs, v_refs, lf_refs, m_scr, l_scr, car_scr, acc_scr, page_size):
    np_ = PAGES_PER_STEP
    later = (lax.broadcasted_iota(jnp.int32, (page_size, page_size), 0)
             > lax.broadcasted_iota(jnp.int32, (page_size, page_size), 1)).astype(F32)
    fold = (lax.broadcasted_iota(jnp.int32, (N_HEADS, N_HEADS * 8), 1) // 8
            == lax.broadcasted_iota(jnp.int32, (N_HEADS, N_HEADS * 8), 0)).astype(F32)
    partial_sums = []
    for r in range(np_):
        parts = []
        for h in range(N_HEADS):
            prod = k_refs[r][0, 0, h] * q_ref[0, h * HEAD_DIM:(h + 1) * HEAD_DIM, :]
            t = prod[0:8]
            for a in range(1, HEAD_DIM // 8):
                t = t + prod[a * 8:(a + 1) * 8]
            parts.append(t)
        partial_sums.append(jnp.concatenate(parts, axis=0))
    s_qk = _dot(fold, jnp.concatenate(partial_sums, axis=1), precision=HIGHEST)
    lf_all = jnp.concatenate([lf_refs[r][0, 0] for r in range(np_)], axis=0)
    within = _dot(lf_all, later, precision=HIGHEST)
    page_tot = jnp.sum(lf_all, axis=-1, keepdims=True)
    carry = car_scr[...]
    bias = []
    for r in range(np_):
        bias.append(within[r * N_HEADS:(r + 1) * N_HEADS] + carry)
        carry = carry + page_tot[r * N_HEADS:(r + 1) * N_HEADS]
    car_scr[...] = carry
    s_all = s_qk + jnp.concatenate(bias, axis=1)
    m_prev = m_scr[...]
    m_new = jnp.maximum(m_prev, jnp.max(s_all, axis=-1, keepdims=True))
    alpha = jnp.exp(m_prev - m_new)
    p_all = jnp.exp(s_all - m_new)
    l_scr[...] = alpha * l_scr[...] + jnp.sum(p_all, axis=-1, keepdims=True)
    m_scr[...] = m_new
    for h in range(N_HEADS):
        a = acc_scr[h] * alpha[h:h + 1, :]
        for r in range(np_):
            a = a + p_all[h:h + 1, r * page_size:(r + 1) * page_size] * v_refs[r][0, 0, h]
        acc_scr[h] = a


def _fox_sample_kernel(pt_ref, q_ref, vn_ref, sn_ref, lfn_ref, *refs, page_size):
    np_ = PAGES_PER_STEP
    k_refs = refs[0:np_]
    v_refs = refs[np_:2 * np_]
    lf_refs = refs[2 * np_:3 * np_]
    o_ref = refs[3 * np_]
    m_scr, l_scr, car_scr, acc_scr = refs[3 * np_ + 1:]
    j = pl.program_id(1)

    @pl.when(j == 0)
    def _():
        _decode_init(vn_ref, sn_ref, lfn_ref, m_scr, l_scr, car_scr, acc_scr, page_size)

    _decode_update(q_ref, k_refs, v_refs, lf_refs, m_scr, l_scr, car_scr, acc_scr, page_size)

    @pl.when(j == pl.num_programs(1) - 1)
    def _():
        _decode_finish(o_ref, l_scr, acc_scr)


def _decode_operands(page_table, q, v_new, s_new, lf_new, cache_kt, cache_vt, cache_lft, layer):
    nseq, n_pages = page_table.shape
    page_size = cache_kt.shape[-1]
    np_ = PAGES_PER_STEP

    def page_idx(r, nd):
        def f(b, j, pt):
            return (layer, pt[b * n_pages + (n_pages - 1 - (j * np_ + r))]) + (0,) * nd
        return f

    col = pl.BlockSpec((1, ATT_DIM, 1), lambda b, j, pt: (b, 0, 0))
    tok = pl.BlockSpec((1, 1, LANES), lambda b, j, pt: (b, 0, 0))
    kv_block = (1, 1, N_HEADS, HEAD_DIM, page_size)
    in_specs = [col, col, tok, tok]
    in_specs += [pl.BlockSpec(kv_block, page_idx(r, 3)) for r in range(np_)]
    in_specs += [pl.BlockSpec(kv_block, page_idx(r, 3)) for r in range(np_)]
    in_specs += [pl.BlockSpec((1, 1, N_HEADS, page_size), page_idx(r, 2)) for r in range(np_)]
    args = [q.reshape(nseq, ATT_DIM, 1), v_new.reshape(nseq, ATT_DIM, 1),
            s_new.reshape(nseq, 1, LANES), lf_new.reshape(nseq, 1, LANES)]
    args += [cache_kt] * np_ + [cache_vt] * np_ + [cache_lft] * np_
    out_spec = pl.BlockSpec((1, N_HEADS, HEAD_DIM, 1), lambda b, j, pt: (b, 0, 0, 0))
    out_shape = jax.ShapeDtypeStruct((nseq, N_HEADS, HEAD_DIM, 1), F32)
    scratch = [pltpu.VMEM((N_HEADS, 1), F32), pltpu.VMEM((N_HEADS, 1), F32),
               pltpu.VMEM((N_HEADS, 1), F32), pltpu.VMEM((N_HEADS, HEAD_DIM, page_size), F32)]
    return in_specs, args, out_spec, out_shape, scratch, n_pages // np_, page_size


def _fox_sample(page_table, q, v_new, s_new, lf_new, cache_kt, cache_vt, cache_lft, *, layer):
    nseq = page_table.shape[0]
    in_specs, args, out_spec, out_shape, scratch, nj, page_size = _decode_operands(
        page_table, q, v_new, s_new, lf_new, cache_kt, cache_vt, cache_lft, layer)
    grid_spec = pltpu.PrefetchScalarGridSpec(
        num_scalar_prefetch=1, grid=(nseq, nj), in_specs=in_specs, out_specs=out_spec, scratch_shapes=scratch)
    out = pl.pallas_call(
        functools.partial(_fox_sample_kernel, page_size=page_size),
        grid_spec=grid_spec,
        out_shape=out_shape,
        compiler_params=_params(("arbitrary", "arbitrary")),
        name="fox_sample",
    )(page_table.reshape(-1), *args)
    return out.reshape(nseq, ATT_DIM).astype(BF16)


def _post_kernel(x_ref, convo_ref, ssmy_ref, att_ref, p_ref, wglu_ref, wout_ref, gpost_ref, gfpre_ref,
                 wgate_ref, wup_ref, wdown_ref, gfpost_ref, wpg_ref, wpp_ref, o_ref):
    y = _gelu_tanh(ssmy_ref[...])
    ssm_out = y * _sigmoid(_dot(y.astype(BF16), wglu_ref[0]))
    mix = (_dot(convo_ref[...], wout_ref[0, 0:CONV_DIM, :])
           + _dot(ssm_out.astype(BF16), wout_ref[0, CONV_DIM:CONV_DIM + SSM_DIM, :])
           + _dot(att_ref[...], wout_ref[0, CONV_DIM + SSM_DIM:, :]))
    x1 = x_ref[...] + _rms(mix, gpost_ref[0])
    h2 = _rms(x1, gfpre_ref[0]).astype(BF16)
    ffn = jnp.zeros(x1.shape, F32)
    for c in range(D_FF // FF_CHUNK):
        cols = slice(c * FF_CHUNK, (c + 1) * FF_CHUNK)
        gate = _dot(h2, wgate_ref[0, :, cols])
        up = _dot(h2, wup_ref[0, :, cols])
        act = (gate * _sigmoid(gate) * up).astype(BF16)
        ffn = ffn + _dot(act, wdown_ref[0, cols, :])
    x2 = x1 + _rms(ffn, gfpost_ref[0])
    pgate = _sigmoid(_dot(x2.astype(BF16), wpg_ref[0]))
    o_ref[...] = x2 + pgate * _dot(p_ref[0].astype(BF16), wpp_ref[0])


def _post(x, convo, ssmy, att, pemb, wglu, wout, gpost, gfpre, wgate, wup, wdown, gfpost, wpg, wpp, layer, *, tm):
    n = x.shape[0]
    row = lambda w: pl.BlockSpec((tm, w), lambda i: (i, 0))
    weights = (wglu, wout, gpost, gfpre, wgate, wup, wdown, gfpost, wpg, wpp)
    return pl.pallas_call(
        _post_kernel,
        grid=(n // tm,),
        in_specs=[row(D_MODEL), row(CONV_DIM), row(SSM_DIM), row(ATT_DIM),
                  pl.BlockSpec((1, tm, PLE_DIM), lambda i: (layer, i, 0))]
                 + [_layer_spec(w, layer) for w in weights],
        out_specs=row(D_MODEL),
        out_shape=jax.ShapeDtypeStruct((n, D_MODEL), F32),
        compiler_params=_params(("arbitrary",)),
        name="post_mixer",
    )(x, convo, ssmy, att, pemb, *weights)


N_POST_WEIGHTS = 10


def _post_schedule(nsub):
    n_ff = D_FF // FF_CHUNK
    cost = ([("head", SSM_DIM * SSM_DIM + D_MODEL * D_MODEL)]
            + [("ffn", 3 * D_MODEL * FF_CHUNK)] * n_ff
            + [("gate", D_MODEL * D_MODEL), ("tail", PLE_DIM * D_MODEL)])
    total = sum(c for _, c in cost)
    groups = [[] for _ in range(nsub)]
    done = 0
    ff = 0
    for name, c in cost:
        g = min(nsub - 1, int((done + c / 2) * nsub / total))
        groups[g].append((name, ff))
        ff += name == "ffn"
        done += c
    return groups


def _post_decode_kernel(pt_ref, x_ref, convo_ref, ssmy_ref, att_ref, p_ref, *refs, page_size, nsub):
    (wglu_ref, wout_ref, gpost_ref, gfpre_ref, wgate_ref, wup_ref, wdown_ref, gfpost_ref,
     wpg_ref, wpp_ref) = refs[0:N_POST_WEIGHTS]
    q_ref, vn_ref, sn_ref, lfn_ref = refs[N_POST_WEIGHTS:N_POST_WEIGHTS + 4]
    np_ = PAGES_PER_STEP
    base = N_POST_WEIGHTS + 4
    k_refs = refs[base:base + np_]
    v_refs = refs[base + np_:base + 2 * np_]
    lf_refs = refs[base + 2 * np_:base + 3 * np_]
    o_ref, od_ref = refs[base + 3 * np_:base + 3 * np_ + 2]
    x1_scr, h2_scr, ffn_scr, ylo_scr, yhi_scr, m_scr, l_scr, car_scr, acc_scr = refs[base + 3 * np_ + 2:]
    step = pl.program_id(1)

    def head():
        tc = ssmy_ref.shape[0]
        for half, scr in enumerate((ylo_scr, yhi_scr)):
            for t in range(SSM_CHUNK):
                c0 = t * SSM_DIM + half * LANES
                scr[pl.ds(t, tc, stride=SSM_CHUNK), :] = ssmy_ref[:, c0:c0 + LANES]
        y = _gelu_tanh(jnp.concatenate([ylo_scr[...], yhi_scr[...]], axis=1))
        ssm_out = y * _sigmoid(_dot(y.astype(BF16), wglu_ref[0]))
        mix = (_dot(convo_ref[...], wout_ref[0, 0:CONV_DIM, :])
               + _dot(ssm_out.astype(BF16), wout_ref[0, CONV_DIM:CONV_DIM + SSM_DIM, :])
               + _dot(att_ref[...], wout_ref[0, CONV_DIM + SSM_DIM:, :]))
        x1 = x_ref[...] + _rms(mix, gpost_ref[0])
        x1_scr[...] = x1
        h2_scr[...] = _rms(x1, gfpre_ref[0]).astype(BF16)

    def ffn(c):
        cols = slice(c * FF_CHUNK, (c + 1) * FF_CHUNK)
        h2 = h2_scr[...]
        gate = _dot(h2, wgate_ref[0, :, cols])
        up = _dot(h2, wup_ref[0, :, cols])
        part = _dot((gate * _sigmoid(gate) * up).astype(BF16), wdown_ref[0, cols, :])
        if c == 0:
            ffn_scr[...] = part
        else:
            ffn_scr[...] += part

    def gate():
        x2 = x1_scr[...] + _rms(ffn_scr[...], gfpost_ref[0])
        x1_scr[...] = x2
        ffn_scr[...] = _sigmoid(_dot(x2.astype(BF16), wpg_ref[0]))

    def tail():
        o_ref[...] = x1_scr[...] + ffn_scr[...] * _dot(p_ref[0].astype(BF16), wpp_ref[0])

    work = {"head": lambda c: head(), "ffn": ffn, "gate": lambda c: gate(), "tail": lambda c: tail()}
    for k, items in enumerate(_post_schedule(nsub)):
        @pl.when(step == k)
        def _():
            lead = 1 if (len(items) >= 2 and all(name == "ffn" for name, _ in items)) else 0
            for name, c in items[:lead]:
                work[name](c)
            if k == 0:
                _decode_init(vn_ref, sn_ref, lfn_ref, m_scr, l_scr, car_scr, acc_scr, page_size)
            _decode_update(q_ref, k_refs, v_refs, lf_refs, m_scr, l_scr, car_scr, acc_scr, page_size)
            if k == nsub - 1:
                _decode_finish(od_ref, l_scr, acc_scr)
            for name, c in items[lead:]:
                work[name](c)


def _post_decode(x, convo, ssmy, att, pemb, weights, page_table, q, v_new, s_new, lf_new,
                 cache_kt, cache_vt, cache_lft, layer, *, tm):
    n = x.shape[0]
    nseq = page_table.shape[0]
    dec_specs, dec_args, dec_out_spec, dec_out_shape, dec_scratch, nsub, page_size = _decode_operands(
        page_table, q, v_new, s_new, lf_new, cache_kt, cache_vt, cache_lft, layer)
    row = lambda w: pl.BlockSpec((tm, w), lambda i, s, pt: (i, 0))
    grid_spec = pltpu.PrefetchScalarGridSpec(
        num_scalar_prefetch=1,
        grid=(nseq, nsub),
        in_specs=[row(D_MODEL), row(CONV_DIM),
                  pl.BlockSpec((tm // SSM_CHUNK, CHUNK_W), lambda i, s, pt: (i, 0)), row(ATT_DIM),
                  pl.BlockSpec((1, tm, PLE_DIM), lambda i, s, pt: (layer, i, 0))]
                 + [_layer_spec(w, layer) for w in weights] + dec_specs,
        out_specs=(row(D_MODEL), dec_out_spec),
        scratch_shapes=[pltpu.VMEM((tm, D_MODEL), F32), pltpu.VMEM((tm, D_MODEL), BF16),
                        pltpu.VMEM((tm, D_MODEL), F32), pltpu.VMEM((tm, LANES), F32),
                        pltpu.VMEM((tm, LANES), F32)] + dec_scratch,
    )
    xo, att_s = pl.pallas_call(
        functools.partial(_post_decode_kernel, page_size=page_size, nsub=nsub),
        grid_spec=grid_spec,
        out_shape=(jax.ShapeDtypeStruct((n, D_MODEL), F32), dec_out_shape),
        compiler_params=_params(("arbitrary", "arbitrary")),
        name="post_mixer_decode",
    )(page_table.reshape(-1), x, convo, ssmy, att, pemb, *weights, *dec_args)
    return xo, att_s.reshape(nseq, ATT_DIM).astype(BF16)


def _kv_gather_kernel(*refs, depth, nseq):
    in_refs = refs[0:depth]
    o_ref = refs[depth]
    layer = pl.program_id(0)
    for l in range(depth):
        @pl.when(layer == l)
        def _():
            for b in range(nseq):
                o_ref[0, b] = in_refs[l][b].T.reshape(N_HEADS, HEAD_DIM, o_ref.shape[-1])


def _kv_gather(per_layer, *, nseq, seq_len):
    depth = len(per_layer)
    tt = min(256, seq_len)
    nt = seq_len // tt
    views = [a.reshape(nseq, seq_len, ATT_DIM) for a in per_layer]

    def in_spec(l):
        return pl.BlockSpec((nseq, tt, ATT_DIM),
                            lambda d, t: (0, jnp.where(d == l, t, jnp.where(d < l, 0, nt - 1)), 0))

    out = pl.pallas_call(
        functools.partial(_kv_gather_kernel, depth=depth, nseq=nseq),
        grid=(depth, nt),
        in_specs=[in_spec(l) for l in range(depth)],
        out_specs=pl.BlockSpec((1, nseq, N_HEADS, HEAD_DIM, tt), lambda d, t: (d, 0, 0, 0, t)),
        out_shape=jax.ShapeDtypeStruct((depth, nseq, N_HEADS, HEAD_DIM, seq_len), F32),
        compiler_params=_params(("arbitrary", "arbitrary")),
        name="kv_gather",
    )(*views)
    return jnp.transpose(out, (0, 1, 4, 2, 3))


def kernel(x_prompt, x_sample, cache_k, cache_v, cache_logf, state_conv, state_ssm_re, state_ssm_im, page_table, p_prompt, p_sample, norm_mix_pre, norm_mix_post, norm_ffn_pre, norm_ffn_post, w_in, b_forget, conv_w, ssm_a_re, ssm_a_im, ssm_log_dt, ssm_b_re, ssm_b_im, ssm_c_re, ssm_c_im, ssm_d, w_ssm_glu, w_out, w_ffn_gate, w_ffn_up, w_ffn_down, w_ple_gate, w_ple_proj):
    depth = w_in.shape[0]
    bp, seq_len, _ = x_prompt.shape
    bs = x_sample.shape[0]
    n_p = bp * seq_len
    tm = min(512, seq_len)
    tq = min(512, seq_len)
    chunks_per_seq = seq_len // SSM_CHUNK
    n_chunks = n_p // SSM_CHUNK
    tmc = min(512, n_chunks)
    tm_fused = n_p // bs
    fuse_decode = (n_p % bs == 0 and tm_fused % 16 == 0 and tm_fused <= 512
                   and page_table.shape[1] % PAGES_PER_STEP == 0)

    gate_cols = 3 * CONV_DIM + SSM_DIM
    wg = w_in[:, :, 0:gate_cols].astype(BF16)
    wqkv = w_in[:, :, gate_cols:gate_cols + 3 * ATT_DIM].astype(BF16)
    wfl = jnp.pad(w_in[:, :, gate_cols + 3 * ATT_DIM:], ((0, 0), (0, 0), (0, LANES - N_HEADS))).astype(BF16)
    bfl = jnp.pad(b_forget, ((0, 0), (0, LANES - N_HEADS)))[:, None, :]
    wglu = w_ssm_glu.astype(BF16)
    wout = w_out.astype(BF16)
    wgate = w_ffn_gate.astype(BF16)
    wup = w_ffn_up.astype(BF16)
    wdown = w_ffn_down.astype(BF16)
    wpg = w_ple_gate.astype(BF16)
    wpp = w_ple_proj.astype(BF16)
    g_pre = norm_mix_pre[:, None, :]
    g_post = norm_mix_post[:, None, :]
    g_fpre = norm_ffn_pre[:, None, :]
    g_fpost = norm_ffn_post[:, None, :]
    d_row = ssm_d[:, None, :]

    w_g, w_ct, bd, b1, c1t, lam = _ssm_prep(ssm_a_re, ssm_a_im, ssm_log_dt, ssm_b_re, ssm_b_im,
                                            ssm_c_re, ssm_c_im)

    cache_kt = jnp.transpose(cache_k, (0, 1, 3, 4, 2))
    cache_vt = jnp.transpose(cache_v, (0, 1, 3, 4, 2))
    cache_lft = jnp.transpose(cache_logf, (0, 1, 3, 2))

    xp = x_prompt.reshape(n_p, D_MODEL)
    xs = x_sample.reshape(bs, D_MODEL)
    pe_prompt = p_prompt.reshape(depth, n_p, PLE_DIM)
    pe_sample = p_sample.reshape(depth, bs, PLE_DIM)
    outs_p = [[] for _ in range(6)]
    outs_s = [[] for _ in range(6)]
    for i in range(depth):
        (convo, uflat_f, uflat_b, q, k, v, kb, vb, lf, ct, crow, conv_new) = _inproj_prompt(
            xp, g_pre, wg, wqkv, wfl, bfl, conv_w, i, seq_len=seq_len, tm=tm)
        gstate = _ssm_chunk_state(uflat_b, w_g, i, tm=tmc, tn=1024)
        hin, hlast = _ssm_scan(gstate, lam, i, chunks_per_seq=chunks_per_seq)
        ssmy = _ssm_output(uflat_b, hin, uflat_f, bd, w_ct, d_row, i, tm=tmc)
        att = _fox_prompt(q, kb, vb, ct, crow, seq_len=seq_len, tq=tq)
        outs_p[0].append(k)
        outs_p[1].append(v)
        outs_p[2].append(lf.reshape(bp, seq_len, N_HEADS))
        outs_p[3].append(conv_new)
        hl = hlast.reshape(bp, 2, SSM_GROUPS, SSM_STATE)
        outs_p[4].append(hl[:, 0])
        outs_p[5].append(hl[:, 1])

        (convo_s, su_s, u_s, q_s, k_s, v_s, lf_s, sn_s) = _inproj_sample(
            xs, g_pre, wg, wqkv, wfl, bfl, conv_w, state_conv[i, :, 0], state_conv[i, :, 1], i)
        ssmy_s, hre_s, him_s = _ssm_step(
            su_s, state_ssm_re[i].reshape(bs, STATE_W), state_ssm_im[i].reshape(bs, STATE_W),
            b1, lam, c1t, d_row, i)
        post_w = (wglu, wout, g_post, g_fpre, wgate, wup, wdown, g_fpost, wpg, wpp)
        if fuse_decode:
            xp, att_s = _post_decode(xp, convo, ssmy, att, pe_prompt, post_w,
                                     page_table, q_s, v_s, sn_s, lf_s, cache_kt, cache_vt, cache_lft, i,
                                     tm=tm_fused)
        else:
            xp = _post(xp, convo, ssmy.reshape(n_p, SSM_DIM), att, pe_prompt, *post_w, i, tm=tm)
            att_s = _fox_sample(page_table, q_s, v_s, sn_s, lf_s, cache_kt, cache_vt, cache_lft, layer=i)
        xs = _post(xs, convo_s, ssmy_s, att_s, pe_sample,
                   wglu, wout, g_post, g_fpre, wgate, wup, wdown, g_fpost, wpg, wpp, i, tm=bs)
        outs_s[0].append(k_s.reshape(bs, 1, N_HEADS, HEAD_DIM))
        outs_s[1].append(v_s.reshape(bs, 1, N_HEADS, HEAD_DIM))
        outs_s[2].append(lf_s[:, 0:N_HEADS].reshape(bs, 1, N_HEADS))
        outs_s[3].append(jnp.stack([state_conv[i, :, 1], u_s], axis=1))
        outs_s[4].append(hre_s.reshape(bs, SSM_GROUPS, SSM_STATE))
        outs_s[5].append(him_s.reshape(bs, SSM_GROUPS, SSM_STATE))

    k_p = _kv_gather(outs_p[0], nseq=bp, seq_len=seq_len)
    v_p = _kv_gather(outs_p[1], nseq=bp, seq_len=seq_len)
    lf_p, conv_p, re_p, im_p = [jnp.stack(a) for a in outs_p[2:]]
    k_s, v_s, lf_s, conv_s, re_s, im_s = [jnp.stack(a) for a in outs_s]
    return (xp.reshape(bp, seq_len, D_MODEL), xs.reshape(bs, 1, D_MODEL),
            k_p, v_p, lf_p, conv_p, re_p, im_p, k_s, v_s, lf_s, conv_s, re_s, im_s)
```

```python
import functools
import math

import jax
import jax.numpy as jnp
from jax import lax
from jax.experimental import pallas as pl
from jax.experimental.pallas import tpu as pltpu

F32 = jnp.float32
BF16 = jnp.bfloat16
HIGHEST = lax.Precision.HIGHEST

D_MODEL = 1024
CONV_DIM = 256
SSM_DIM = 256
ATT_DIM = 512
N_HEADS = 8
HEAD_DIM = 64
SSM_GROUPS = 16
SSM_GROUP = 16
SSM_STATE = 64
D_FF = 2816
PLE_DIM = 256
CONV_K = 3
EPS = 1e-6
LOG2E = math.log2(math.e)
LANES = 128
SSM_CHUNK = 16
FF_CHUNK = 256
STATE_W = SSM_GROUPS * SSM_STATE
CHUNK_W = SSM_CHUNK * SSM_DIM
VMEM_LIMIT = 60 * 1024 * 1024


def _params(sem, vmem=VMEM_LIMIT):
    return pltpu.CompilerParams(dimension_semantics=sem, vmem_limit_bytes=vmem)


def _rms(x, g):
    return x * lax.rsqrt(jnp.mean(x * x, axis=-1, keepdims=True) + EPS) * g


def _sigmoid(x):
    return 1.0 / (1.0 + jnp.exp(-x))


def _log_sigmoid(x):
    return jnp.minimum(x, 0.0) - jnp.log1p(jnp.exp(-jnp.abs(x)))


def _gelu_tanh(x):
    return 0.5 * x * (1.0 + jnp.tanh(math.sqrt(2.0 / math.pi) * (x + 0.044715 * (x * x * x))))


def _dot(a, b, **kw):
    return jnp.dot(a, b, preferred_element_type=F32, **kw)


def _dot_nt(a, b, **kw):
    return lax.dot_general(a, b, (((1,), (1,)), ((), ())), preferred_element_type=F32, **kw)


def _split3(x):
    hi = x.astype(BF16)
    r1 = x - hi.astype(F32)
    mid = r1.astype(BF16)
    lo = (r1 - mid.astype(F32)).astype(BF16)
    return hi, mid, lo


def _layer_spec(arr, layer):
    zeros = (0,) * (arr.ndim - 1)
    return pl.BlockSpec((1,) + arr.shape[1:], lambda *_: (layer,) + zeros, pipeline_mode=pl.Buffered(1))


def _ssm_prep_kernel(are_ref, aim_ref, ldt_ref, btr_ref, bti_ref, ctr_ref, cti_ref,
                     wg_ref, wct_ref, bd_ref, b1_ref, c1t_ref, lam_ref, pw_scr):
    t = pl.program_id(1)
    a_re = are_ref[0]
    a_im = aim_ref[0]
    dt = jnp.exp(ldt_ref[0])
    mag = jnp.exp(a_re * dt)
    lr = mag * jnp.cos(a_im * dt)
    li = mag * jnp.sin(a_im * dt)
    den = a_re * a_re + a_im * a_im
    xr = lr - 1.0
    cfr = (xr * a_re + li * a_im) / den
    cfi = (li * a_re - xr * a_im) / den
    btr = btr_ref[0]
    bti = bti_ref[0]
    bbr = cfr * btr - cfi * bti
    bbi = cfr * bti + cfi * btr
    ctr = ctr_ref[0]
    cti = cti_ref[0]

    rows = SSM_GROUPS * SSM_GROUP
    grp_r = lax.broadcasted_iota(jnp.int32, (rows, STATE_W), 0) // SSM_GROUP
    grp_c = lax.broadcasted_iota(jnp.int32, (rows, STATE_W), 1) // SSM_STATE
    mask = (grp_r == grp_c).astype(F32)

    def blockdiag(x):
        return jnp.concatenate([x] * SSM_GROUPS, axis=0) * mask

    @pl.when(t == 0)
    def _():
        pw_scr[0:1, :] = jnp.ones((1, STATE_W), F32)
        pw_scr[1:2, :] = jnp.zeros((1, STATE_W), F32)
        lam_ref[0] = jnp.zeros((8, STATE_W), F32)
        lam_ref[0, 0:1, :] = lr
        lam_ref[0, 1:2, :] = li
        b1_ref[0, :, 0:STATE_W] = blockdiag(bbr)
        b1_ref[0, :, STATE_W:] = blockdiag(bbi)
        c1t_ref[0, :, 0:STATE_W] = blockdiag(ctr)
        c1t_ref[0, :, STATE_W:] = blockdiag(-cti)

    pr = pw_scr[0:1, :]
    pi = pw_scr[1:2, :]
    nr = pr * lr - pi * li
    ni = pr * li + pi * lr
    pw_scr[0:1, :] = nr
    pw_scr[1:2, :] = ni

    @pl.when(t == SSM_CHUNK - 1)
    def _():
        lam_ref[0, 2:3, :] = nr
        lam_ref[0, 3:4, :] = ni

    wg_ref[0, :, 0:STATE_W] = blockdiag(bbr * pr - bbi * pi).astype(BF16)
    wg_ref[0, :, STATE_W:] = blockdiag(bbr * pi + bbi * pr).astype(BF16)
    wct_ref[0, :, 0:STATE_W] = blockdiag(ctr * nr - cti * ni).astype(BF16)
    wct_ref[0, :, STATE_W:] = blockdiag(-(ctr * ni + cti * nr)).astype(BF16)
    pb = jnp.concatenate([blockdiag(bbr), blockdiag(bbi)], axis=1)
    pa = jnp.concatenate([blockdiag(ctr * pr - cti * pi), blockdiag(-(ctr * pi + cti * pr))], axis=1)
    pb_hi = pb.astype(BF16)
    pb_lo = (pb - pb_hi.astype(F32)).astype(BF16)
    pa_hi = pa.astype(BF16)
    pa_lo = (pa - pa_hi.astype(F32)).astype(BF16)
    bd_ref[0, 0] = (_dot_nt(pb_hi, pa_hi) + _dot_nt(pb_hi, pa_lo) + _dot_nt(pb_lo, pa_hi)).astype(BF16)


def _ssm_prep(a_re, a_im, log_dt, b_re, b_im, c_re, c_im):
    depth = a_re.shape[0]
    g, p, c, l = SSM_GROUPS, SSM_STATE, SSM_GROUP, SSM_CHUNK
    rows = g * c
    row1 = lambda x: x.reshape(depth, 1, g * p)
    chan = lambda x: x.reshape(depth, c, g * p)
    args = (row1(a_re), row1(a_im), row1(jnp.repeat(log_dt, p, axis=-1)),
            chan(jnp.transpose(b_re, (0, 3, 1, 2))), chan(jnp.transpose(b_im, (0, 3, 1, 2))),
            chan(jnp.transpose(c_re, (0, 2, 1, 3))), chan(jnp.transpose(c_im, (0, 2, 1, 3))))
    spec_row = pl.BlockSpec((1, 1, g * p), lambda i, t: (i, 0, 0))
    spec_chan = pl.BlockSpec((1, c, g * p), lambda i, t: (i, 0, 0))
    out_shapes = (
        jax.ShapeDtypeStruct((depth, l * rows, 2 * g * p), BF16),
        jax.ShapeDtypeStruct((depth, l * rows, 2 * g * p), BF16),
        jax.ShapeDtypeStruct((depth, l, rows, rows), BF16),
        jax.ShapeDtypeStruct((depth, rows, 2 * g * p), F32),
        jax.ShapeDtypeStruct((depth, rows, 2 * g * p), F32),
        jax.ShapeDtypeStruct((depth, 8, g * p), F32),
    )
    out_specs = (
        pl.BlockSpec((1, rows, 2 * g * p), lambda i, t: (i, l - 1 - t, 0)),
        pl.BlockSpec((1, rows, 2 * g * p), lambda i, t: (i, t, 0)),
        pl.BlockSpec((1, 1, rows, rows), lambda i, t: (i, t, 0, 0)),
        pl.BlockSpec((1, rows, 2 * g * p), lambda i, t: (i, 0, 0)),
        pl.BlockSpec((1, rows, 2 * g * p), lambda i, t: (i, 0, 0)),
        pl.BlockSpec((1, 8, g * p), lambda i, t: (i, 0, 0)),
    )
    return pl.pallas_call(
        _ssm_prep_kernel,
        grid=(depth, l),
        in_specs=[spec_row, spec_row, spec_row, spec_chan, spec_chan, spec_chan, spec_chan],
        out_specs=out_specs,
        out_shape=out_shapes,
        scratch_shapes=[pltpu.VMEM((2, g * p), F32)],
        compiler_params=_params(("arbitrary", "arbitrary")),
        name="ssm_prep",
    )(*args)


def _inproj_prompt_kernel(x_ref, g_ref, wg_ref, wqkv_ref, wfl_ref, bfl_ref, cw_ref,
                          convo_ref, su_ref, sub_ref, q_ref, k_ref, v_ref, kb_ref, vb_ref,
                          lf_ref, ct_ref, crow_ref, convnew_ref, ubuf, ccar, su_lo, su_hi, *, tiles_per_seq, tm):
    i = pl.program_id(0)

    @pl.when(i % tiles_per_seq == 0)
    def _():
        ubuf[0:8, :] = jnp.zeros((8, CONV_DIM), F32)
        ccar[...] = jnp.zeros_like(ccar)

    h = _rms(x_ref[...], g_ref[0]).astype(BF16)
    z = _dot(h, wg_ref[0])
    cb = z[:, 0:CONV_DIM]
    cc = z[:, CONV_DIM:2 * CONV_DIM]
    cv = z[:, 2 * CONV_DIM:3 * CONV_DIM]
    for half, scr in enumerate((su_lo, su_hi)):
        scr[...] = z[:, 3 * CONV_DIM + half * LANES:3 * CONV_DIM + (half + 1) * LANES]
        for s in range(SSM_CHUNK):
            piece = scr[pl.ds(s, tm // SSM_CHUNK, stride=SSM_CHUNK), :]
            cols = slice(s * SSM_DIM + half * LANES, s * SSM_DIM + (half + 1) * LANES)
            su_ref[:, cols] = piece
            sub_ref[:, cols] = piece.astype(BF16)
    u = cc * cv
    ubuf[8:8 + tm, :] = u
    u1 = ubuf[7:7 + tm, :]
    u2 = ubuf[6:6 + tm, :]
    cw = cw_ref[0]
    y = cw[0:1] * u2 + cw[1:2] * u1 + cw[2:3] * u
    convo_ref[...] = (cb * y).astype(BF16)
    convnew_ref[0] = u[tm - 2:tm, :]
    ubuf[0:8, :] = u[tm - 8:tm, :]

    zz = _dot(h, wqkv_ref[0])
    q_ref[...] = (zz[:, 0:ATT_DIM] * (HEAD_DIM ** -0.5 * LOG2E)).astype(BF16)
    k = zz[:, ATT_DIM:2 * ATT_DIM]
    v = zz[:, 2 * ATT_DIM:]
    k_ref[...] = k
    v_ref[...] = v
    kb_ref[...] = k.astype(BF16)
    vb_ref[...] = v.astype(BF16)

    fl = _dot(h, wfl_ref[0]) + bfl_ref[0]
    lane = lax.broadcasted_iota(jnp.int32, fl.shape, 1)
    lf = jnp.where(lane < N_HEADS, _log_sigmoid(fl), 0.0)
    lf_ref[...] = lf[:, 0:N_HEADS]
    row = lax.broadcasted_iota(jnp.int32, (LANES, LANES), 0)
    col = lax.broadcasted_iota(jnp.int32, (LANES, LANES), 1)
    tri = (row >= col).astype(F32).astype(BF16)
    carry = ccar[...]
    for r0 in range(0, tm, LANES):
        hi, mid, lo = _split3(lf[r0:r0 + LANES, :])
        c = _dot(tri, hi) + _dot(tri, mid) + _dot(tri, lo) + carry
        carry = c[LANES - 1:LANES, :]
        crow_ref[r0:r0 + LANES, :] = c
        ct_ref[:, r0:r0 + LANES] = c.T[0:N_HEADS, :]
    ccar[...] = carry


def _inproj_prompt(x, g_pre, wg, wqkv, wfl, bfl, conv_w, layer, *, seq_len, tm):
    n = x.shape[0]
    nt = n // tm
    tiles_per_seq = seq_len // tm
    nseq = n // seq_len
    tc = tm // SSM_CHUNK
    row = lambda w: pl.BlockSpec((tm, w), lambda i: (i, 0))
    chunk_rows = pl.BlockSpec((tc, CHUNK_W), lambda i: (i, 0))
    out_shapes = (
        jax.ShapeDtypeStruct((n, CONV_DIM), BF16),
        jax.ShapeDtypeStruct((n // SSM_CHUNK, CHUNK_W), F32),
        jax.ShapeDtypeStruct((n // SSM_CHUNK, CHUNK_W), BF16),
        jax.ShapeDtypeStruct((n, ATT_DIM), BF16),
        jax.ShapeDtypeStruct((n, ATT_DIM), F32),
        jax.ShapeDtypeStruct((n, ATT_DIM), F32),
        jax.ShapeDtypeStruct((n, ATT_DIM), BF16),
        jax.ShapeDtypeStruct((n, ATT_DIM), BF16),
        jax.ShapeDtypeStruct((n, N_HEADS), F32),
        jax.ShapeDtypeStruct((N_HEADS, n), F32),
        jax.ShapeDtypeStruct((n, LANES), F32),
        jax.ShapeDtypeStruct((nseq, CONV_K - 1, CONV_DIM), F32),
    )
    out_specs = (
        row(CONV_DIM), chunk_rows, chunk_rows, row(ATT_DIM), row(ATT_DIM), row(ATT_DIM),
        row(ATT_DIM), row(ATT_DIM), row(N_HEADS),
        pl.BlockSpec((N_HEADS, tm), lambda i: (0, i)),
        row(LANES),
        pl.BlockSpec((1, CONV_K - 1, CONV_DIM), lambda i: (i // tiles_per_seq, 0, 0)),
    )
    weights = (g_pre, wg, wqkv, wfl, bfl, conv_w)
    return pl.pallas_call(
        functools.partial(_inproj_prompt_kernel, tiles_per_seq=tiles_per_seq, tm=tm),
        grid=(nt,),
        in_specs=[row(D_MODEL)] + [_layer_spec(w, layer) for w in weights],
        out_specs=out_specs,
        out_shape=out_shapes,
        scratch_shapes=[pltpu.VMEM((tm + 8, CONV_DIM), F32), pltpu.VMEM((1, LANES), F32),
                        pltpu.VMEM((tm, LANES), F32), pltpu.VMEM((tm, LANES), F32)],
        compiler_params=_params(("arbitrary",)),
        name="inproj_prompt",
    )(x, *weights)


def _inproj_sample_kernel(x_ref, g_ref, wg_ref, wqkv_ref, wfl_ref, bfl_ref, cw_ref, b0_ref, b1_ref,
                          convo_ref, su_ref, u_ref, q_ref, k_ref, v_ref, lf_ref, sn_ref):
    h = _rms(x_ref[...], g_ref[0]).astype(BF16)
    z = _dot(h, wg_ref[0])
    cb = z[:, 0:CONV_DIM]
    u = z[:, CONV_DIM:2 * CONV_DIM] * z[:, 2 * CONV_DIM:3 * CONV_DIM]
    su_ref[...] = z[:, 3 * CONV_DIM:]
    u_ref[...] = u
    cw = cw_ref[0]
    y = cw[0:1] * b0_ref[...] + cw[1:2] * b1_ref[...] + cw[2:3] * u
    convo_ref[...] = (cb * y).astype(BF16)
    zz = _dot(h, wqkv_ref[0])
    q = zz[:, 0:ATT_DIM] * (HEAD_DIM ** -0.5)
    k = zz[:, ATT_DIM:2 * ATT_DIM]
    q_ref[...] = q
    k_ref[...] = k
    v_ref[...] = zz[:, 2 * ATT_DIM:]
    fl = _dot(h, wfl_ref[0]) + bfl_ref[0]
    lane = lax.broadcasted_iota(jnp.int32, fl.shape, 1)
    lf_ref[...] = jnp.where(lane < N_HEADS, _log_sigmoid(fl), 0.0)
    hd_row = lax.broadcasted_iota(jnp.int32, (ATT_DIM, LANES), 0) // HEAD_DIM
    hd_col = lax.broadcasted_iota(jnp.int32, (ATT_DIM, LANES), 1)
    sn_ref[...] = _dot(q * k, (hd_row == hd_col).astype(F32), precision=HIGHEST)


def _inproj_sample(x, g_pre, wg, wqkv, wfl, bfl, conv_w, buf0, buf1, layer):
    n = x.shape[0]
    full = lambda s: pl.BlockSpec(s, lambda i: (0,) * len(s))
    weights = (g_pre, wg, wqkv, wfl, bfl, conv_w)
    out_shapes = (
        jax.ShapeDtypeStruct((n, CONV_DIM), BF16),
        jax.ShapeDtypeStruct((n, SSM_DIM), F32),
        jax.ShapeDtypeStruct((n, CONV_DIM), F32),
        jax.ShapeDtypeStruct((n, ATT_DIM), F32),
        jax.ShapeDtypeStruct((n, ATT_DIM), F32),
        jax.ShapeDtypeStruct((n, ATT_DIM), F32),
        jax.ShapeDtypeStruct((n, LANES), F32),
        jax.ShapeDtypeStruct((n, LANES), F32),
    )
    args = (x,) + weights + (buf0, buf1)
    return pl.pallas_call(
        _inproj_sample_kernel,
        grid=(1,),
        in_specs=[full(x.shape)] + [_layer_spec(w, layer) for w in weights] + [full(buf0.shape), full(buf1.shape)],
        out_specs=tuple(full(s.shape) for s in out_shapes),
        out_shape=out_shapes,
        compiler_params=_params(("arbitrary",)),
        name="inproj_sample",
    )(*args)


def _ssm_chunk_state_kernel(u_ref, w_ref, o_ref):
    o_ref[...] = _dot(u_ref[...], w_ref[0])


def _ssm_chunk_state(uflat, w_g, layer, *, tm, tn):
    m, k = uflat.shape
    n = w_g.shape[2]
    return pl.pallas_call(
        _ssm_chunk_state_kernel,
        grid=(n // tn, m // tm),
        in_specs=[pl.BlockSpec((tm, k), lambda j, i: (i, 0)),
                  pl.BlockSpec((1, k, tn), lambda j, i: (layer, 0, j))],
        out_specs=pl.BlockSpec((tm, tn), lambda j, i: (i, j)),
        out_shape=jax.ShapeDtypeStruct((m, n), F32),
        compiler_params=_params(("arbitrary", "arbitrary")),
        name="ssm_chunk_state",
    )(uflat, w_g)


def _ssm_scan_kernel(g_ref, lam_ref, hin_ref, hlast_ref, *, n_chunks):
    lr = lam_ref[0, 2:3, :]
    li = lam_ref[0, 3:4, :]

    def body(c, carry):
        h_re, h_im = carry
        hin_ref[pl.ds(c, 1), 0:STATE_W] = h_re
        hin_ref[pl.ds(c, 1), STATE_W:] = h_im
        g_re = g_ref[pl.ds(c, 1), 0:STATE_W]
        g_im = g_ref[pl.ds(c, 1), STATE_W:]
        return lr * h_re - li * h_im + g_re, lr * h_im + li * h_re + g_im

    zero = jnp.zeros((1, STATE_W), F32)
    h_re, h_im = lax.fori_loop(0, n_chunks, body, (zero, zero))
    hlast_ref[0, :, 0:STATE_W] = h_re
    hlast_ref[0, :, STATE_W:] = h_im


def _ssm_scan(gstate, lam, layer, *, chunks_per_seq):
    m, w = gstate.shape
    nseq = m // chunks_per_seq
    return pl.pallas_call(
        functools.partial(_ssm_scan_kernel, n_chunks=chunks_per_seq),
        grid=(nseq,),
        in_specs=[pl.BlockSpec((chunks_per_seq, w), lambda b: (b, 0)),
                  pl.BlockSpec((1, 8, STATE_W), lambda b: (layer, 0, 0))],
        out_specs=(pl.BlockSpec((chunks_per_seq, w), lambda b: (b, 0)),
                   pl.BlockSpec((1, 1, w), lambda b: (b, 0, 0))),
        out_shape=(jax.ShapeDtypeStruct((m, w), F32), jax.ShapeDtypeStruct((nseq, 1, w), F32)),
        compiler_params=_params(("arbitrary",)),
        name="ssm_scan",
    )(gstate, lam)


def _ssm_output_kernel(ub_ref, hin_ref, uf_ref, bd_ref, wct_ref, d_ref, y_ref, hb_scr):
    j = pl.program_id(1)
    cw = SSM_DIM

    @pl.when(j == 0)
    def _():
        hb_scr[...] = hin_ref[...].astype(BF16)

    y_ref[...] = _dot_nt(hb_scr[...], wct_ref[0]) + d_ref[0] * uf_ref[...]
    for s in range(SSM_CHUNK):
        @pl.when(s <= j)
        def _():
            y_ref[...] += _dot(ub_ref[:, s * cw:(s + 1) * cw], bd_ref[0, j - s])


def _ssm_output(uflat_b, hin, uflat_f, bd, wct, d_row, layer, *, tm):
    m, k = uflat_b.shape
    ks = hin.shape[1]
    cw = SSM_DIM
    return pl.pallas_call(
        _ssm_output_kernel,
        grid=(m // tm, SSM_CHUNK),
        in_specs=[pl.BlockSpec((tm, k), lambda i, j: (i, 0)),
                  pl.BlockSpec((tm, ks), lambda i, j: (i, 0)),
                  pl.BlockSpec((tm, cw), lambda i, j: (i, j)),
                  pl.BlockSpec((1, SSM_CHUNK, cw, cw), lambda i, j: (layer, 0, 0, 0)),
                  pl.BlockSpec((1, cw, ks), lambda i, j: (layer, j, 0)),
                  pl.BlockSpec((1, 1, cw), lambda i, j: (layer, 0, 0))],
        out_specs=pl.BlockSpec((tm, cw), lambda i, j: (i, j)),
        out_shape=jax.ShapeDtypeStruct((m, k), F32),
        scratch_shapes=[pltpu.VMEM((tm, ks), BF16)],
        compiler_params=_params(("arbitrary", "arbitrary")),
        name="ssm_output",
    )(uflat_b, hin, uflat_f, bd, wct, d_row)


def _ssm_step_kernel(u_ref, hre_ref, him_ref, b1_ref, lam_ref, c1t_ref, d_ref,
                     y_ref, ore_ref, oim_ref):
    u = u_ref[...]
    lr = lam_ref[0, 0:1, :]
    li = lam_ref[0, 1:2, :]
    h_re = hre_ref[...]
    h_im = him_ref[...]
    bu = _dot(u, b1_ref[0], precision=HIGHEST)
    n_re = lr * h_re - li * h_im + bu[:, 0:STATE_W]
    n_im = lr * h_im + li * h_re + bu[:, STATE_W:]
    ore_ref[...] = n_re
    oim_ref[...] = n_im
    y = (_dot_nt(n_re, c1t_ref[0, :, 0:STATE_W], precision=HIGHEST)
         + _dot_nt(n_im, c1t_ref[0, :, STATE_W:], precision=HIGHEST))
    y_ref[...] = y + d_ref[0] * u


def _ssm_step(u, h_re, h_im, b1, lam, c1t, d, layer):
    n = u.shape[0]
    full = lambda s: pl.BlockSpec(s, lambda i: (0,) * len(s))
    lay = lambda s: pl.BlockSpec((1,) + s[1:], lambda i: (layer,) + (0,) * (len(s) - 1))
    out_shapes = (jax.ShapeDtypeStruct((n, SSM_DIM), F32), jax.ShapeDtypeStruct((n, STATE_W), F32),
                  jax.ShapeDtypeStruct((n, STATE_W), F32))
    return pl.pallas_call(
        _ssm_step_kernel,
        grid=(1,),
        in_specs=[full(u.shape), full(h_re.shape), full(h_im.shape), lay(b1.shape), lay(lam.shape),
                  lay(c1t.shape), lay(d.shape)],
        out_specs=tuple(full(s.shape) for s in out_shapes),
        out_shape=out_shapes,
        compiler_params=_params(("arbitrary",)),
        name="ssm_step",
    )(u, h_re, h_im, b1, lam, c1t, d)


def _fox_prompt_kernel(qi_ref, ki_ref, q_ref, k_ref, v_ref, cq_ref, ck_ref, o_ref,
                       m_scr, l_scr, acc_scr, cq_scr, *, tq, tk, rb, kpt):
    qi = qi_ref[pl.program_id(2)]
    kg = ki_ref[pl.program_id(2)]

    @pl.when(kg == 0)
    def _():
        m_scr[...] = jnp.full(m_scr.shape, -jnp.inf, F32)
        l_scr[...] = jnp.zeros_like(l_scr)
        acc_scr[...] = jnp.zeros_like(acc_scr)
        src_lane = lax.broadcasted_iota(jnp.int32, (LANES, 2 * LANES), 0)
        dst_head = lax.broadcasted_iota(jnp.int32, (LANES, 2 * LANES), 1) // LANES
        pick = (src_lane == 2 * pl.program_id(1) + dst_head).astype(F32).astype(BF16)
        hi, mid, lo = _split3(cq_ref[...])
        spread = (_dot(hi, pick) + _dot(mid, pick) + _dot(lo, pick)) * LOG2E
        for hh in range(2):
            cq_scr[hh] = spread[:, hh * LANES:(hh + 1) * LANES]

    def step(on_diagonal, sub):
        keys = slice(sub * tk, (sub + 1) * tk)
        k2 = k_ref[keys, :]
        v2 = v_ref[keys, :]
        lane = lax.broadcasted_iota(jnp.int32, (rb, LANES), 1)
        keep = [(lane < HEAD_DIM).astype(F32).astype(BF16), (lane >= HEAD_DIM).astype(F32).astype(BF16)]
        lane_o = lane < HEAD_DIM
        ck2 = [ck_ref[0, hh:hh + 1, keys] * LOG2E for hh in range(2)]
        nkeys = [min(tk, r0 + rb) if on_diagonal else tk for r0 in range(0, tq, rb)]
        scores = [[_dot_nt(q_ref[r0:r0 + rb, :] * keep[hh], k2[0:nkeys[bi], :]) for hh in range(2)]
                  for bi, r0 in enumerate(range(0, tq, rb))]
        for bi, r0 in enumerate(range(0, tq, rb)):
            rows = slice(r0, r0 + rb)
            reps = nkeys[bi] // LANES
            if on_diagonal:
                rel = (lax.broadcasted_iota(jnp.int32, (rb, LANES), 1)
                       - lax.broadcasted_iota(jnp.int32, (rb, LANES), 0))
            alphas = []
            pvs = []
            for hh in range(2):
                sc = []
                for c in range(reps):
                    cols = slice(c * LANES, (c + 1) * LANES)
                    s_c = scores[bi][hh][:, cols] - ck2[hh][:, cols]
                    if on_diagonal:
                        s_c = jnp.where(rel <= r0 - c * LANES, s_c, -jnp.inf)
                    sc.append(s_c)
                mx = sc[0]
                for c in range(1, reps):
                    mx = jnp.maximum(mx, sc[c])
                cq = cq_scr[hh, rows, :]
                m_prev = m_scr[hh, rows, :]
                m_new = jnp.maximum(m_prev, jnp.max(mx, axis=-1, keepdims=True) + cq)
                t = m_new - cq
                ps = [jnp.exp2(s_c - t) for s_c in sc]
                tot = ps[0]
                for c in range(1, reps):
                    tot = tot + ps[c]
                alpha = jnp.exp2(m_prev - m_new)
                l_scr[hh, rows, :] = alpha * l_scr[hh, rows, :] + jnp.sum(tot, axis=-1, keepdims=True)
                m_scr[hh, rows, :] = m_new
                alphas.append(alpha)
                p = jnp.concatenate([p_c.astype(BF16) for p_c in ps], axis=1)
                pvs.append(_dot(p, v2[0:nkeys[bi], :]))
            acc_scr[rows, :] = (jnp.where(lane_o, alphas[0], alphas[1]) * acc_scr[rows, :]
                                + jnp.where(lane_o, pvs[0], pvs[1]))

    def finish():
        lane_o = lax.broadcasted_iota(jnp.int32, (tq, LANES), 1) < HEAD_DIM
        o_ref[...] = (acc_scr[...] / jnp.where(lane_o, l_scr[0], l_scr[1])).astype(o_ref.dtype)

    @pl.when(kg * kpt + kpt - 1 < qi)
    def _():
        for sub in range(kpt):
            step(False, sub)

    for diag in range(kpt):
        @pl.when(qi == kg * kpt + diag)
        def _():
            for sub in range(diag):
                step(False, sub)
            step(True, diag)
            finish()


def _fox_prompt(q, kb, vb, ct, cq, *, seq_len, tq):
    n = q.shape[0]
    nseq = n // seq_len
    nq = seq_len // tq
    hp = N_HEADS // 2
    ck = ct.reshape(hp, 2, n)
    kpt = next(c for c in (4, 2, 1) if nq % c == 0)
    ng = nq // kpt
    pairs = [(i, g) for i in range(nq) for g in range(i // kpt + 1)]
    qi_tab = jnp.asarray([p[0] for p in pairs], jnp.int32)
    ki_tab = jnp.asarray([p[1] for p in pairs], jnp.int32)
    q_idx = lambda b, h, t, qt, kt: (b * nq + qt[t], h)
    kv_idx = lambda b, h, t, qt, kt: (b * ng + kt[t], h)
    grid_spec = pltpu.PrefetchScalarGridSpec(
        num_scalar_prefetch=2,
        grid=(nseq, hp, len(pairs)),
        in_specs=[pl.BlockSpec((tq, LANES), q_idx),
                  pl.BlockSpec((kpt * tq, LANES), kv_idx),
                  pl.BlockSpec((kpt * tq, LANES), kv_idx),
                  pl.BlockSpec((tq, LANES), lambda b, h, t, qt, kt: (b * nq + qt[t], 0)),
                  pl.BlockSpec((1, 2, kpt * tq), lambda b, h, t, qt, kt: (h, 0, b * ng + kt[t]))],
        out_specs=pl.BlockSpec((tq, LANES), q_idx),
        scratch_shapes=[pltpu.VMEM((2, tq, LANES), F32), pltpu.VMEM((2, tq, LANES), F32),
                        pltpu.VMEM((tq, LANES), F32), pltpu.VMEM((2, tq, LANES), F32)],
    )
    return pl.pallas_call(
        functools.partial(_fox_prompt_kernel, tq=tq, tk=tq, rb=min(256, tq), kpt=kpt),
        grid_spec=grid_spec,
        out_shape=jax.ShapeDtypeStruct((n, ATT_DIM), BF16),
        compiler_params=_params(("arbitrary", "arbitrary", "arbitrary")),
        name="fox_prompt",
    )(qi_tab, ki_tab, q, kb, vb, cq, ck)


PAGES_PER_STEP = 8


def _decode_init(vn_ref, sn_ref, lfn_ref, m_scr, l_scr, car_scr, acc_scr, page_size):
    eye = (lax.broadcasted_iota(jnp.int32, (N_HEADS, LANES), 0)
           == lax.broadcasted_iota(jnp.int32, (N_HEADS, LANES), 1)).astype(F32)

    def to_col(row):
        return jnp.sum(eye * row, axis=-1, keepdims=True)

    m_scr[...] = to_col(sn_ref[0])
    l_scr[...] = jnp.ones_like(l_scr)
    car_scr[...] = to_col(lfn_ref[0])
    lane0 = lax.broadcasted_iota(jnp.int32, (HEAD_DIM, page_size), 1) == 0
    for h in range(N_HEADS):
        vcol = vn_ref[0, h * HEAD_DIM:(h + 1) * HEAD_DIM, :]
        acc_scr[h] = jnp.where(lane0, vcol, 0.0)


def _decode_finish(o_ref, l_scr, acc_scr):
    inv = 1.0 / l_scr[...]
    for h in range(N_HEADS):
        o_ref[0, h] = jnp.sum(acc_scr[h], axis=-1, keepdims=True) * inv[h:h + 1, :]


def _decode_update(q_ref, k_refs, v_refs, lf_refs, m_scr, l_scr, car_scr, acc_scr, page_size):
    np_ = PAGES_PER_STEP
    later = (lax.broadcasted_iota(jnp.int32, (page_size, page_size), 0)
             > lax.broadcasted_iota(jnp.int32, (page_size, page_size), 1)).astype(F32)
    fold = (lax.broadcasted_iota(jnp.int32, (N_HEADS, N_HEADS * 8), 1) // 8
            == lax.broadcasted_iota(jnp.int32, (N_HEADS, N_HEADS * 8), 0)).astype(F32)
    partial_sums = []
    for r in range(np_):
        parts = []
        for h in range(N_HEADS):
            prod = k_refs[r][0, 0, h] * q_ref[0, h * HEAD_DIM:(h + 1) * HEAD_DIM, :]
            t = prod[0:8]
            for a in range(1, HEAD_DIM // 8):
                t = t + prod[a * 8:(a + 1) * 8]
            parts.append(t)
        partial_sums.append(jnp.concatenate(parts, axis=0))
    s_qk = _dot(fold, jnp.concatenate(partial_sums, axis=1), precision=HIGHEST)
    lf_all = jnp.concatenate([lf_refs[r][0, 0] for r in range(np_)], axis=0)
    within = _dot(lf_all, later, precision=HIGHEST)
    page_tot = jnp.sum(lf_all, axis=-1, keepdims=True)
    carry = car_scr[...]
    bias = []
    for r in range(np_):
        bias.append(within[r * N_HEADS:(r + 1) * N_HEADS] + carry)
        carry = carry + page_tot[r * N_HEADS:(r + 1) * N_HEADS]
    car_scr[...] = carry
    s_all = s_qk + jnp.concatenate(bias, axis=1)
    m_prev = m_scr[...]
    m_new = jnp.maximum(m_prev, jnp.max(s_all, axis=-1, keepdims=True))
    alpha = jnp.exp(m_prev - m_new)
    p_all = jnp.exp(s_all - m_new)
    l_scr[...] = alpha * l_scr[...] + jnp.sum(p_all, axis=-1, keepdims=True)
    m_scr[...] = m_new
    for h in range(N_HEADS):
        a = acc_scr[h] * alpha[h:h + 1, :]
        for r in range(np_):
            a = a + p_all[h:h + 1, r * page_size:(r + 1) * page_size] * v_refs[r][0, 0, h]
        acc_scr[h] = a


def _fox_sample_kernel(pt_ref, q_ref, vn_ref, sn_ref, lfn_ref, *refs, page_size):
    np_ = PAGES_PER_STEP
    k_refs = refs[0:np_]
    v_refs = refs[np_:2 * np_]
    lf_refs = refs[2 * np_:3 * np_]
    o_ref = refs[3 * np_]
    m_scr, l_scr, car_scr, acc_scr = refs[3 * np_ + 1:]
    j = pl.program_id(1)

    @pl.when(j == 0)
    def _():
        _decode_init(vn_ref, sn_ref, lfn_ref, m_scr, l_scr, car_scr, acc_scr, page_size)

    _decode_update(q_ref, k_refs, v_refs, lf_refs, m_scr, l_scr, car_scr, acc_scr, page_size)

    @pl.when(j == pl.num_programs(1) - 1)
    def _():
        _decode_finish(o_ref, l_scr, acc_scr)


def _decode_operands(page_table, q, v_new, s_new, lf_new, cache_kt, cache_vt, cache_lft, layer):
    nseq, n_pages = page_table.shape
    page_size = cache_kt.shape[-1]
    np_ = PAGES_PER_STEP

    def page_idx(r, nd):
        def f(b, j, pt):
            return (layer, pt[b * n_pages + (n_pages - 1 - (j * np_ + r))]) + (0,) * nd
        return f

    col = pl.BlockSpec((1, ATT_DIM, 1), lambda b, j, pt: (b, 0, 0))
    tok = pl.BlockSpec((1, 1, LANES), lambda b, j, pt: (b, 0, 0))
    kv_block = (1, 1, N_HEADS, HEAD_DIM, page_size)
    in_specs = [col, col, tok, tok]
    in_specs += [pl.BlockSpec(kv_block, page_idx(r, 3)) for r in range(np_)]
    in_specs += [pl.BlockSpec(kv_block, page_idx(r, 3)) for r in range(np_)]
    in_specs += [pl.BlockSpec((1, 1, N_HEADS, page_size), page_idx(r, 2)) for r in range(np_)]
    args = [q.reshape(nseq, ATT_DIM, 1), v_new.reshape(nseq, ATT_DIM, 1),
            s_new.reshape(nseq, 1, LANES), lf_new.reshape(nseq, 1, LANES)]
    args += [cache_kt] * np_ + [cache_vt] * np_ + [cache_lft] * np_
    out_spec = pl.BlockSpec((1, N_HEADS, HEAD_DIM, 1), lambda b, j, pt: (b, 0, 0, 0))
    out_shape = jax.ShapeDtypeStruct((nseq, N_HEADS, HEAD_DIM, 1), F32)
    scratch = [pltpu.VMEM((N_HEADS, 1), F32), pltpu.VMEM((N_HEADS, 1), F32),
               pltpu.VMEM((N_HEADS, 1), F32), pltpu.VMEM((N_HEADS, HEAD_DIM, page_size), F32)]
    return in_specs, args, out_spec, out_shape, scratch, n_pages // np_, page_size


def _fox_sample(page_table, q, v_new, s_new, lf_new, cache_kt, cache_vt, cache_lft, *, layer):
    nseq = page_table.shape[0]
    in_specs, args, out_spec, out_shape, scratch, nj, page_size = _decode_operands(
        page_table, q, v_new, s_new, lf_new, cache_kt, cache_vt, cache_lft, layer)
    grid_spec = pltpu.PrefetchScalarGridSpec(
        num_scalar_prefetch=1, grid=(nseq, nj), in_specs=in_specs, out_specs=out_spec, scratch_shapes=scratch)
    out = pl.pallas_call(
        functools.partial(_fox_sample_kernel, page_size=page_size),
        grid_spec=grid_spec,
        out_shape=out_shape,
        compiler_params=_params(("arbitrary", "arbitrary")),
        name="fox_sample",
    )(page_table.reshape(-1), *args)
    return out.reshape(nseq, ATT_DIM).astype(BF16)


def _post_kernel(x_ref, convo_ref, ssmy_ref, att_ref, p_ref, wglu_ref, wout_ref, gpost_ref, gfpre_ref,
                 wgate_ref, wup_ref, wdown_ref, gfpost_ref, wpg_ref, wpp_ref, o_ref):
    y = _gelu_tanh(ssmy_ref[...])
    ssm_out = y * _sigmoid(_dot(y.astype(BF16), wglu_ref[0]))
    mix = (_dot(convo_ref[...], wout_ref[0, 0:CONV_DIM, :])
           + _dot(ssm_out.astype(BF16), wout_ref[0, CONV_DIM:CONV_DIM + SSM_DIM, :])
           + _dot(att_ref[...], wout_ref[0, CONV_DIM + SSM_DIM:, :]))
    x1 = x_ref[...] + _rms(mix, gpost_ref[0])
    h2 = _rms(x1, gfpre_ref[0]).astype(BF16)
    ffn = jnp.zeros(x1.shape, F32)
    for c in range(D_FF // FF_CHUNK):
        cols = slice(c * FF_CHUNK, (c + 1) * FF_CHUNK)
        gate = _dot(h2, wgate_ref[0, :, cols])
        up = _dot(h2, wup_ref[0, :, cols])
        act = (gate * _sigmoid(gate) * up).astype(BF16)
        ffn = ffn + _dot(act, wdown_ref[0, cols, :])
    x2 = x1 + _rms(ffn, gfpost_ref[0])
    pgate = _sigmoid(_dot(x2.astype(BF16), wpg_ref[0]))
    o_ref[...] = x2 + pgate * _dot(p_ref[0].astype(BF16), wpp_ref[0])


def _post(x, convo, ssmy, att, pemb, wglu, wout, gpost, gfpre, wgate, wup, wdown, gfpost, wpg, wpp, layer, *, tm):
    n = x.shape[0]
    row = lambda w: pl.BlockSpec((tm, w), lambda i: (i, 0))
    weights = (wglu, wout, gpost, gfpre, wgate, wup, wdown, gfpost, wpg, wpp)
    return pl.pallas_call(
        _post_kernel,
        grid=(n // tm,),
        in_specs=[row(D_MODEL), row(CONV_DIM), row(SSM_DIM), row(ATT_DIM),
                  pl.BlockSpec((1, tm, PLE_DIM), lambda i: (layer, i, 0))]
                 + [_layer_spec(w, layer) for w in weights],
        out_specs=row(D_MODEL),
        out_shape=jax.ShapeDtypeStruct((n, D_MODEL), F32),
        compiler_params=_params(("arbitrary",)),
        name="post_mixer",
    )(x, convo, ssmy, att, pemb, *weights)


N_POST_WEIGHTS = 10


def _post_schedule(nsub):
    n_ff = D_FF // FF_CHUNK
    cost = ([("head", SSM_DIM * SSM_DIM + D_MODEL * D_MODEL)]
            + [("ffn", 3 * D_MODEL * FF_CHUNK)] * n_ff
            + [("gate", D_MODEL * D_MODEL), ("tail", PLE_DIM * D_MODEL)])
    total = sum(c for _, c in cost)
    groups = [[] for _ in range(nsub)]
    done = 0
    ff = 0
    for name, c in cost:
        g = min(nsub - 1, int((done + c / 2) * nsub / total))
        groups[g].append((name, ff))
        ff += name == "ffn"
        done += c
    return groups


def _post_decode_kernel(pt_ref, x_ref, convo_ref, ssmy_ref, att_ref, p_ref, *refs, page_size, nsub):
    (wglu_ref, wout_ref, gpost_ref, gfpre_ref, wgate_ref, wup_ref, wdown_ref, gfpost_ref,
     wpg_ref, wpp_ref) = refs[0:N_POST_WEIGHTS]
    q_ref, vn_ref, sn_ref, lfn_ref = refs[N_POST_WEIGHTS:N_POST_WEIGHTS + 4]
    np_ = PAGES_PER_STEP
    base = N_POST_WEIGHTS + 4
    k_refs = refs[base:base + np_]
    v_refs = refs[base + np_:base + 2 * np_]
    lf_refs = refs[base + 2 * np_:base + 3 * np_]
    o_ref, od_ref = refs[base + 3 * np_:base + 3 * np_ + 2]
    x1_scr, h2_scr, ffn_scr, ylo_scr, yhi_scr, m_scr, l_scr, car_scr, acc_scr = refs[base + 3 * np_ + 2:]
    step = pl.program_id(1)

    def head():
        tc = ssmy_ref.shape[0]
        for half, scr in enumerate((ylo_scr, yhi_scr)):
            for t in range(SSM_CHUNK):
                c0 = t * SSM_DIM + half * LANES
                scr[pl.ds(t, tc, stride=SSM_CHUNK), :] = ssmy_ref[:, c0:c0 + LANES]
        y = _gelu_tanh(jnp.concatenate([ylo_scr[...], yhi_scr[...]], axis=1))
        ssm_out = y * _sigmoid(_dot(y.astype(BF16), wglu_ref[0]))
        mix = (_dot(convo_ref[...], wout_ref[0, 0:CONV_DIM, :])
               + _dot(ssm_out.astype(BF16), wout_ref[0, CONV_DIM:CONV_DIM + SSM_DIM, :])
               + _dot(att_ref[...], wout_ref[0, CONV_DIM + SSM_DIM:, :]))
        x1 = x_ref[...] + _rms(mix, gpost_ref[0])
        x1_scr[...] = x1
        h2_scr[...] = _rms(x1, gfpre_ref[0]).astype(BF16)

    def ffn(c):
        cols = slice(c * FF_CHUNK, (c + 1) * FF_CHUNK)
        h2 = h2_scr[...]
        gate = _dot(h2, wgate_ref[0, :, cols])
        up = _dot(h2, wup_ref[0, :, cols])
        part = _dot((gate * _sigmoid(gate) * up).astype(BF16), wdown_ref[0, cols, :])
        if c == 0:
            ffn_scr[...] = part
        else:
            ffn_scr[...] += part

    def gate():
        x2 = x1_scr[...] + _rms(ffn_scr[...], gfpost_ref[0])
        x1_scr[...] = x2
        ffn_scr[...] = _sigmoid(_dot(x2.astype(BF16), wpg_ref[0]))

    def tail():
        o_ref[...] = x1_scr[...] + ffn_scr[...] * _dot(p_ref[0].astype(BF16), wpp_ref[0])

    work = {"head": lambda c: head(), "ffn": ffn, "gate": lambda c: gate(), "tail": lambda c: tail()}
    for k, items in enumerate(_post_schedule(nsub)):
        @pl.when(step == k)
        def _():
            lead = 1 if (len(items) >= 2 and all(name == "ffn" for name, _ in items)) else 0
            for name, c in items[:lead]:
                work[name](c)
            if k == 0:
                _decode_init(vn_ref, sn_ref, lfn_ref, m_scr, l_scr, car_scr, acc_scr, page_size)
            _decode_update(q_ref, k_refs, v_refs, lf_refs, m_scr, l_scr, car_scr, acc_scr, page_size)
            if k == nsub - 1:
                _decode_finish(od_ref, l_scr, acc_scr)
            for name, c in items[lead:]:
                work[name](c)


def _post_decode(x, convo, ssmy, att, pemb, weights, page_table, q, v_new, s_new, lf_new,
                 cache_kt, cache_vt, cache_lft, layer, *, tm):
    n = x.shape[0]
    nseq = page_table.shape[0]
    dec_specs, dec_args, dec_out_spec, dec_out_shape, dec_scratch, nsub, page_size = _decode_operands(
        page_table, q, v_new, s_new, lf_new, cache_kt, cache_vt, cache_lft, layer)
    row = lambda w: pl.BlockSpec((tm, w), lambda i, s, pt: (i, 0))
    grid_spec = pltpu.PrefetchScalarGridSpec(
        num_scalar_prefetch=1,
        grid=(nseq, nsub),
        in_specs=[row(D_MODEL), row(CONV_DIM),
                  pl.BlockSpec((tm // SSM_CHUNK, CHUNK_W), lambda i, s, pt: (i, 0)), row(ATT_DIM),
                  pl.BlockSpec((1, tm, PLE_DIM), lambda i, s, pt: (layer, i, 0))]
                 + [_layer_spec(w, layer) for w in weights] + dec_specs,
        out_specs=(row(D_MODEL), dec_out_spec),
        scratch_shapes=[pltpu.VMEM((tm, D_MODEL), F32), pltpu.VMEM((tm, D_MODEL), BF16),
                        pltpu.VMEM((tm, D_MODEL), F32), pltpu.VMEM((tm, LANES), F32),
                        pltpu.VMEM((tm, LANES), F32)] + dec_scratch,
    )
    xo, att_s = pl.pallas_call(
        functools.partial(_post_decode_kernel, page_size=page_size, nsub=nsub),
        grid_spec=grid_spec,
        out_shape=(jax.ShapeDtypeStruct((n, D_MODEL), F32), dec_out_shape),
        compiler_params=_params(("arbitrary", "arbitrary")),
        name="post_mixer_decode",
    )(page_table.reshape(-1), x, convo, ssmy, att, pemb, *weights, *dec_args)
    return xo, att_s.reshape(nseq, ATT_DIM).astype(BF16)


def _kv_gather_kernel(*refs, depth, nseq):
    in_refs = refs[0:depth]
    o_ref = refs[depth]
    layer = pl.program_id(0)
    for l in range(depth):
        @pl.when(layer == l)
        def _():
            for b in range(nseq):
                o_ref[0, b] = in_refs[l][b].T.reshape(N_HEADS, HEAD_DIM, o_ref.shape[-1])


def _kv_gather(per_layer, *, nseq, seq_len):
    depth = len(per_layer)
    tt = min(256, seq_len)
    nt = seq_len // tt
    views = [a.reshape(nseq, seq_len, ATT_DIM) for a in per_layer]

    def in_spec(l):
        return pl.BlockSpec((nseq, tt, ATT_DIM),
                            lambda d, t: (0, jnp.where(d == l, t, jnp.where(d < l, 0, nt - 1)), 0))

    out = pl.pallas_call(
        functools.partial(_kv_gather_kernel, depth=depth, nseq=nseq),
        grid=(depth, nt),
        in_specs=[in_spec(l) for l in range(depth)],
        out_specs=pl.BlockSpec((1, nseq, N_HEADS, HEAD_DIM, tt), lambda d, t: (d, 0, 0, 0, t)),
        out_shape=jax.ShapeDtypeStruct((depth, nseq, N_HEADS, HEAD_DIM, seq_len), F32),
        compiler_params=_params(("arbitrary", "arbitrary")),
        name="kv_gather",
    )(*views)
    return jnp.transpose(out, (0, 1, 4, 2, 3))


def kernel(x_prompt, x_sample, cache_k, cache_v, cache_logf, state_conv, state_ssm_re, state_ssm_im, page_table, p_prompt, p_sample, norm_mix_pre, norm_mix_post, norm_ffn_pre, norm_ffn_post, w_in, b_forget, conv_w, ssm_a_re, ssm_a_im, ssm_log_dt, ssm_b_re, ssm_b_im, ssm_c_re, ssm_c_im, ssm_d, w_ssm_glu, w_out, w_ffn_gate, w_ffn_up, w_ffn_down, w_ple_gate, w_ple_proj):
    depth = w_in.shape[0]
    bp, seq_len, _ = x_prompt.shape
    bs = x_sample.shape[0]
    n_p = bp * seq_len
    tm = min(512, seq_len)
    tq = min(512, seq_len)
    chunks_per_seq = seq_len // SSM_CHUNK
    n_chunks = n_p // SSM_CHUNK
    tmc = min(512, n_chunks)
    tm_fused = n_p // bs
    fuse_decode = (n_p % bs == 0 and tm_fused % 16 == 0 and tm_fused <= 512
                   and page_table.shape[1] % PAGES_PER_STEP == 0)

    gate_cols = 3 * CONV_DIM + SSM_DIM
    wg = w_in[:, :, 0:gate_cols].astype(BF16)
    wqkv = w_in[:, :, gate_cols:gate_cols + 3 * ATT_DIM].astype(BF16)
    wfl = jnp.pad(w_in[:, :, gate_cols + 3 * ATT_DIM:], ((0, 0), (0, 0), (0, LANES - N_HEADS))).astype(BF16)
    bfl = jnp.pad(b_forget, ((0, 0), (0, LANES - N_HEADS)))[:, None, :]
    wglu = w_ssm_glu.astype(BF16)
    wout = w_out.astype(BF16)
    wgate = w_ffn_gate.astype(BF16)
    wup = w_ffn_up.astype(BF16)
    wdown = w_ffn_down.astype(BF16)
    wpg = w_ple_gate.astype(BF16)
    wpp = w_ple_proj.astype(BF16)
    g_pre = norm_mix_pre[:, None, :]
    g_post = norm_mix_post[:, None, :]
    g_fpre = norm_ffn_pre[:, None, :]
    g_fpost = norm_ffn_post[:, None, :]
    d_row = ssm_d[:, None, :]

    w_g, w_ct, bd, b1, c1t, lam = _ssm_prep(ssm_a_re, ssm_a_im, ssm_log_dt, ssm_b_re, ssm_b_im,
                                            ssm_c_re, ssm_c_im)

    cache_kt = jnp.transpose(cache_k, (0, 1, 3, 4, 2))
    cache_vt = jnp.transpose(cache_v, (0, 1, 3, 4, 2))
    cache_lft = jnp.transpose(cache_logf, (0, 1, 3, 2))

    xp = x_prompt.reshape(n_p, D_MODEL)
    xs = x_sample.reshape(bs, D_MODEL)
    pe_prompt = p_prompt.reshape(depth, n_p, PLE_DIM)
    pe_sample = p_sample.reshape(depth, bs, PLE_DIM)
    outs_p = [[] for _ in range(6)]
    outs_s = [[] for _ in range(6)]
    for i in range(depth):
        (convo, uflat_f, uflat_b, q, k, v, kb, vb, lf, ct, crow, conv_new) = _inproj_prompt(
            xp, g_pre, wg, wqkv, wfl, bfl, conv_w, i, seq_len=seq_len, tm=tm)
        gstate = _ssm_chunk_state(uflat_b, w_g, i, tm=tmc, tn=1024)
        hin, hlast = _ssm_scan(gstate, lam, i, chunks_per_seq=chunks_per_seq)
        ssmy = _ssm_output(uflat_b, hin, uflat_f, bd, w_ct, d_row, i, tm=tmc)
        att = _fox_prompt(q, kb, vb, ct, crow, seq_len=seq_len, tq=tq)
        outs_p[0].append(k)
        outs_p[1].append(v)
        outs_p[2].append(lf.reshape(bp, seq_len, N_HEADS))
        outs_p[3].append(conv_new)
        hl = hlast.reshape(bp, 2, SSM_GROUPS, SSM_STATE)
        outs_p[4].append(hl[:, 0])
        outs_p[5].append(hl[:, 1])

        (convo_s, su_s, u_s, q_s, k_s, v_s, lf_s, sn_s) = _inproj_sample(
            xs, g_pre, wg, wqkv, wfl, bfl, conv_w, state_conv[i, :, 0], state_conv[i, :, 1], i)
        ssmy_s, hre_s, him_s = _ssm_step(
            su_s, state_ssm_re[i].reshape(bs, STATE_W), state_ssm_im[i].reshape(bs, STATE_W),
            b1, lam, c1t, d_row, i)
        post_w = (wglu, wout, g_post, g_fpre, wgate, wup, wdown, g_fpost, wpg, wpp)
        if fuse_decode:
            xp, att_s = _post_decode(xp, convo, ssmy, att, pe_prompt, post_w,
                                     page_table, q_s, v_s, sn_s, lf_s, cache_kt, cache_vt, cache_lft, i,
                                     tm=tm_fused)
        else:
            xp = _post(xp, convo, ssmy.reshape(n_p, SSM_DIM), att, pe_prompt, *post_w, i, tm=tm)
            att_s = _fox_sample(page_table, q_s, v_s, sn_s, lf_s, cache_kt, cache_vt, cache_lft, layer=i)
        xs = _post(xs, convo_s, ssmy_s, att_s, pe_sample,
                   wglu, wout, g_post, g_fpre, wgate, wup, wdown, g_fpost, wpg, wpp, i, tm=bs)
        outs_s[0].append(k_s.reshape(bs, 1, N_HEADS, HEAD_DIM))
        outs_s[1].append(v_s.reshape(bs, 1, N_HEADS, HEAD_DIM))
        outs_s[2].append(lf_s[:, 0:N_HEADS].reshape(bs, 1, N_HEADS))
        outs_s[3].append(jnp.stack([state_conv[i, :, 1], u_s], axis=1))
        outs_s[4].append(hre_s.reshape(bs, SSM_GROUPS, SSM_STATE))
        outs_s[5].append(him_s.reshape(bs, SSM_GROUPS, SSM_STATE))

    k_p = _kv_gather(outs_p[0], nseq=bp, seq_len=seq_len)
    v_p = _kv_gather(outs_p[1], nseq=bp, seq_len=seq_len)
    lf_p, conv_p, re_p, im_p = [jnp.stack(a) for a in outs_p[2:]]
    k_s, v_s, lf_s, conv_s, re_s, im_s = [jnp.stack(a) for a in outs_s]
    return (xp.reshape(bp, seq_len, D_MODEL), xs.reshape(bs, 1, D_MODEL),
            k_p, v_p, lf_p, conv_p, re_p, im_p, k_s, v_s, lf_s, conv_s, re_s, im_s)
```

```python
import functools
import math

import jax
import jax.numpy as jnp
from jax import lax
from jax.experimental import pallas as pl
from jax.experimental.pallas import tpu as pltpu

F32 = jnp.float32
BF16 = jnp.bfloat16
HIGHEST = lax.Precision.HIGHEST

D_MODEL = 1024
CONV_DIM = 256
SSM_DIM = 256
ATT_DIM = 512
N_HEADS = 8
HEAD_DIM = 64
SSM_GROUPS = 16
SSM_GROUP = 16
SSM_STATE = 64
D_FF = 2816
PLE_DIM = 256
CONV_K = 3
EPS = 1e-6
LOG2E = math.log2(math.e)
LANES = 128
SSM_CHUNK = 16
FF_CHUNK = 256
STATE_W = SSM_GROUPS * SSM_STATE
CHUNK_W = SSM_CHUNK * SSM_DIM
VMEM_LIMIT = 60 * 1024 * 1024


def _params(sem, vmem=VMEM_LIMIT):
    return pltpu.CompilerParams(dimension_semantics=sem, vmem_limit_bytes=vmem)


def _rms(x, g):
    return x * lax.rsqrt(jnp.mean(x * x, axis=-1, keepdims=True) + EPS) * g


def _sigmoid(x):
    return 1.0 / (1.0 + jnp.exp(-x))


def _log_sigmoid(x):
    return jnp.minimum(x, 0.0) - jnp.log1p(jnp.exp(-jnp.abs(x)))


def _gelu_tanh(x):
    return 0.5 * x * (1.0 + jnp.tanh(math.sqrt(2.0 / math.pi) * (x + 0.044715 * (x * x * x))))


def _dot(a, b, **kw):
    return jnp.dot(a, b, preferred_element_type=F32, **kw)


def _dot_nt(a, b, **kw):
    return lax.dot_general(a, b, (((1,), (1,)), ((), ())), preferred_element_type=F32, **kw)


def _split3(x):
    hi = x.astype(BF16)
    r1 = x - hi.astype(F32)
    mid = r1.astype(BF16)
    lo = (r1 - mid.astype(F32)).astype(BF16)
    return hi, mid, lo


def _layer_spec(arr, layer):
    zeros = (0,) * (arr.ndim - 1)
    return pl.BlockSpec((1,) + arr.shape[1:], lambda *_: (layer,) + zeros, pipeline_mode=pl.Buffered(1))


def _ssm_prep_kernel(are_ref, aim_ref, ldt_ref, btr_ref, bti_ref, ctr_ref, cti_ref,
                     wg_ref, wct_ref, bd_ref, b1_ref, c1t_ref, lam_ref, pw_scr):
    t = pl.program_id(1)
    a_re = are_ref[0]
    a_im = aim_ref[0]
    dt = jnp.exp(ldt_ref[0])
    mag = jnp.exp(a_re * dt)
    lr = mag * jnp.cos(a_im * dt)
    li = mag * jnp.sin(a_im * dt)
    den = a_re * a_re + a_im * a_im
    xr = lr - 1.0
    cfr = (xr * a_re + li * a_im) / den
    cfi = (li * a_re - xr * a_im) / den
    btr = btr_ref[0]
    bti = bti_ref[0]
    bbr = cfr * btr - cfi * bti
    bbi = cfr * bti + cfi * btr
    ctr = ctr_ref[0]
    cti = cti_ref[0]

    rows = SSM_GROUPS * SSM_GROUP
    grp_r = lax.broadcasted_iota(jnp.int32, (rows, STATE_W), 0) // SSM_GROUP
    grp_c = lax.broadcasted_iota(jnp.int32, (rows, STATE_W), 1) // SSM_STATE
    mask = (grp_r == grp_c).astype(F32)

    def blockdiag(x):
        return jnp.concatenate([x] * SSM_GROUPS, axis=0) * mask

    @pl.when(t == 0)
    def _():
        pw_scr[0:1, :] = jnp.ones((1, STATE_W), F32)
        pw_scr[1:2, :] = jnp.zeros((1, STATE_W), F32)
        lam_ref[0] = jnp.zeros((8, STATE_W), F32)
        lam_ref[0, 0:1, :] = lr
        lam_ref[0, 1:2, :] = li
        b1_ref[0, :, 0:STATE_W] = blockdiag(bbr)
        b1_ref[0, :, STATE_W:] = blockdiag(bbi)
        c1t_ref[0, :, 0:STATE_W] = blockdiag(ctr)
        c1t_ref[0, :, STATE_W:] = blockdiag(-cti)

    pr = pw_scr[0:1, :]
    pi = pw_scr[1:2, :]
    nr = pr * lr - pi * li
    ni = pr * li + pi * lr
    pw_scr[0:1, :] = nr
    pw_scr[1:2, :] = ni

    @pl.when(t == SSM_CHUNK - 1)
    def _():
        lam_ref[0, 2:3, :] = nr
        lam_ref[0, 3:4, :] = ni

    wg_ref[0, :, 0:STATE_W] = blockdiag(bbr * pr - bbi * pi).astype(BF16)
    wg_ref[0, :, STATE_W:] = blockdiag(bbr * pi + bbi * pr).astype(BF16)
    wct_ref[0, :, 0:STATE_W] = blockdiag(ctr * nr - cti * ni).astype(BF16)
    wct_ref[0, :, STATE_W:] = blockdiag(-(ctr * ni + cti * nr)).astype(BF16)
    pb = jnp.concatenate([blockdiag(bbr), blockdiag(bbi)], axis=1)
    pa = jnp.concatenate([blockdiag(ctr * pr - cti * pi), blockdiag(-(ctr * pi + cti * pr))], axis=1)
    pb_hi = pb.astype(BF16)
    pb_lo = (pb - pb_hi.astype(F32)).astype(BF16)
    pa_hi = pa.astype(BF16)
    pa_lo = (pa - pa_hi.astype(F32)).astype(BF16)
    bd_ref[0, 0] = (_dot_nt(pb_hi, pa_hi) + _dot_nt(pb_hi, pa_lo) + _dot_nt(pb_lo, pa_hi)).astype(BF16)


def _ssm_prep(a_re, a_im, log_dt, b_re, b_im, c_re, c_im):
    depth = a_re.shape[0]
    g, p, c, l = SSM_GROUPS, SSM_STATE, SSM_GROUP, SSM_CHUNK
    rows = g * c
    row1 = lambda x: x.reshape(depth, 1, g * p)
    chan = lambda x: x.reshape(depth, c, g * p)
    args = (row1(a_re), row1(a_im), row1(jnp.repeat(log_dt, p, axis=-1)),
            chan(jnp.transpose(b_re, (0, 3, 1, 2))), chan(jnp.transpose(b_im, (0, 3, 1, 2))),
            chan(jnp.transpose(c_re, (0, 2, 1, 3))), chan(jnp.transpose(c_im, (0, 2, 1, 3))))
    spec_row = pl.BlockSpec((1, 1, g * p), lambda i, t: (i, 0, 0))
    spec_chan = pl.BlockSpec((1, c, g * p), lambda i, t: (i, 0, 0))
    out_shapes = (
        jax.ShapeDtypeStruct((depth, l * rows, 2 * g * p), BF16),
        jax.ShapeDtypeStruct((depth, l * rows, 2 * g * p), BF16),
        jax.ShapeDtypeStruct((depth, l, rows, rows), BF16),
        jax.ShapeDtypeStruct((depth, rows, 2 * g * p), F32),
        jax.ShapeDtypeStruct((depth, rows, 2 * g * p), F32),
        jax.ShapeDtypeStruct((depth, 8, g * p), F32),
    )
    out_specs = (
        pl.BlockSpec((1, rows, 2 * g * p), lambda i, t: (i, l - 1 - t, 0)),
        pl.BlockSpec((1, rows, 2 * g * p), lambda i, t: (i, t, 0)),
        pl.BlockSpec((1, 1, rows, rows), lambda i, t: (i, t, 0, 0)),
        pl.BlockSpec((1, rows, 2 * g * p), lambda i, t: (i, 0, 0)),
        pl.BlockSpec((1, rows, 2 * g * p), lambda i, t: (i, 0, 0)),
        pl.BlockSpec((1, 8, g * p), lambda i, t: (i, 0, 0)),
    )
    return pl.pallas_call(
        _ssm_prep_kernel,
        grid=(depth, l),
        in_specs=[spec_row, spec_row, spec_row, spec_chan, spec_chan, spec_chan, spec_chan],
        out_specs=out_specs,
        out_shape=out_shapes,
        scratch_shapes=[pltpu.VMEM((2, g * p), F32)],
        compiler_params=_params(("arbitrary", "arbitrary")),
        name="ssm_prep",
    )(*args)


def _inproj_prompt_kernel(x_ref, g_ref, wg_ref, wqkv_ref, wfl_ref, bfl_ref, cw_ref,
                          convo_ref, su_ref, sub_ref, q_ref, k_ref, v_ref, kb_ref, vb_ref,
                          lf_ref, ct_ref, crow_ref, convnew_ref, ubuf, ccar, su_lo, su_hi, *, tiles_per_seq, tm):
    i = pl.program_id(0)

    @pl.when(i % tiles_per_seq == 0)
    def _():
        ubuf[0:8, :] = jnp.zeros((8, CONV_DIM), F32)
        ccar[...] = jnp.zeros_like(ccar)

    h = _rms(x_ref[...], g_ref[0]).astype(BF16)
    z = _dot(h, wg_ref[0])
    cb = z[:, 0:CONV_DIM]
    cc = z[:, CONV_DIM:2 * CONV_DIM]
    cv = z[:, 2 * CONV_DIM:3 * CONV_DIM]
    for half, scr in enumerate((su_lo, su_hi)):
        scr[...] = z[:, 3 * CONV_DIM + half * LANES:3 * CONV_DIM + (half + 1) * LANES]
        for s in range(SSM_CHUNK):
            piece = scr[pl.ds(s, tm // SSM_CHUNK, stride=SSM_CHUNK), :]
            cols = slice(s * SSM_DIM + half * LANES, s * SSM_DIM + (half + 1) * LANES)
            su_ref[:, cols] = piece
            sub_ref[:, cols] = piece.astype(BF16)
    u = cc * cv
    ubuf[8:8 + tm, :] = u
    u1 = ubuf[7:7 + tm, :]
    u2 = ubuf[6:6 + tm, :]
    cw = cw_ref[0]
    y = cw[0:1] * u2 + cw[1:2] * u1 + cw[2:3] * u
    convo_ref[...] = (cb * y).astype(BF16)
    convnew_ref[0] = u[tm - 2:tm, :]
    ubuf[0:8, :] = u[tm - 8:tm, :]

    zz = _dot(h, wqkv_ref[0])
    q_ref[...] = (zz[:, 0:ATT_DIM] * (HEAD_DIM ** -0.5 * LOG2E)).astype(BF16)
    k = zz[:, ATT_DIM:2 * ATT_DIM]
    v = zz[:, 2 * ATT_DIM:]
    k_ref[...] = k
    v_ref[...] = v
    kb_ref[...] = k.astype(BF16)
    vb_ref[...] = v.astype(BF16)

    fl = _dot(h, wfl_ref[0]) + bfl_ref[0]
    lane = lax.broadcasted_iota(jnp.int32, fl.shape, 1)
    lf = jnp.where(lane < N_HEADS, _log_sigmoid(fl), 0.0)
    lf_ref[...] = lf[:, 0:N_HEADS]
    row = lax.broadcasted_iota(jnp.int32, (LANES, LANES), 0)
    col = lax.broadcasted_iota(jnp.int32, (LANES, LANES), 1)
    tri = (row >= col).astype(F32).astype(BF16)
    carry = ccar[...]
    for r0 in range(0, tm, LANES):
        hi, mid, lo = _split3(lf[r0:r0 + LANES, :])
        c = _dot(tri, hi) + _dot(tri, mid) + _dot(tri, lo) + carry
        carry = c[LANES - 1:LANES, :]
        crow_ref[r0:r0 + LANES, :] = c
        ct_ref[:, r0:r0 + LANES] = c.T[0:N_HEADS, :]
    ccar[...] = carry


def _inproj_prompt(x, g_pre, wg, wqkv, wfl, bfl, conv_w, layer, *, seq_len, tm):
    n = x.shape[0]
    nt = n // tm
    tiles_per_seq = seq_len // tm
    nseq = n // seq_len
    tc = tm // SSM_CHUNK
    row = lambda w: pl.BlockSpec((tm, w), lambda i: (i, 0))
    chunk_rows = pl.BlockSpec((tc, CHUNK_W), lambda i: (i, 0))
    out_shapes = (
        jax.ShapeDtypeStruct((n, CONV_DIM), BF16),
        jax.ShapeDtypeStruct((n // SSM_CHUNK, CHUNK_W), F32),
        jax.ShapeDtypeStruct((n // SSM_CHUNK, CHUNK_W), BF16),
        jax.ShapeDtypeStruct((n, ATT_DIM), BF16),
        jax.ShapeDtypeStruct((n, ATT_DIM), F32),
        jax.ShapeDtypeStruct((n, ATT_DIM), F32),
        jax.ShapeDtypeStruct((n, ATT_DIM), BF16),
        jax.ShapeDtypeStruct((n, ATT_DIM), BF16),
        jax.ShapeDtypeStruct((n, N_HEADS), F32),
        jax.ShapeDtypeStruct((N_HEADS, n), F32),
        jax.ShapeDtypeStruct((n, LANES), F32),
        jax.ShapeDtypeStruct((nseq, CONV_K - 1, CONV_DIM), F32),
    )
    out_specs = (
        row(CONV_DIM), chunk_rows, chunk_rows, row(ATT_DIM), row(ATT_DIM), row(ATT_DIM),
        row(ATT_DIM), row(ATT_DIM), row(N_HEADS),
        pl.BlockSpec((N_HEADS, tm), lambda i: (0, i)),
        row(LANES),
        pl.BlockSpec((1, CONV_K - 1, CONV_DIM), lambda i: (i // tiles_per_seq, 0, 0)),
    )
    weights = (g_pre, wg, wqkv, wfl, bfl, conv_w)
    return pl.pallas_call(
        functools.partial(_inproj_prompt_kernel, tiles_per_seq=tiles_per_seq, tm=tm),
        grid=(nt,),
        in_specs=[row(D_MODEL)] + [_layer_spec(w, layer) for w in weights],
        out_specs=out_specs,
        out_shape=out_shapes,
        scratch_shapes=[pltpu.VMEM((tm + 8, CONV_DIM), F32), pltpu.VMEM((1, LANES), F32),
                        pltpu.VMEM((tm, LANES), F32), pltpu.VMEM((tm, LANES), F32)],
        compiler_params=_params(("arbitrary",)),
        name="inproj_prompt",
    )(x, *weights)


def _inproj_sample_kernel(x_ref, g_ref, wg_ref, wqkv_ref, wfl_ref, bfl_ref, cw_ref, b0_ref, b1_ref,
                          convo_ref, su_ref, u_ref, q_ref, k_ref, v_ref, lf_ref, sn_ref):
    h = _rms(x_ref[...], g_ref[0]).astype(BF16)
    z = _dot(h, wg_ref[0])
    cb = z[:, 0:CONV_DIM]
    u = z[:, CONV_DIM:2 * CONV_DIM] * z[:, 2 * CONV_DIM:3 * CONV_DIM]
    su_ref[...] = z[:, 3 * CONV_DIM:]
    u_ref[...] = u
    cw = cw_ref[0]
    y = cw[0:1] * b0_ref[...] + cw[1:2] * b1_ref[...] + cw[2:3] * u
    convo_ref[...] = (cb * y).astype(BF16)
    zz = _dot(h, wqkv_ref[0])
    q = zz[:, 0:ATT_DIM] * (HEAD_DIM ** -0.5)
    k = zz[:, ATT_DIM:2 * ATT_DIM]
    q_ref[...] = q
    k_ref[...] = k
    v_ref[...] = zz[:, 2 * ATT_DIM:]
    fl = _dot(h, wfl_ref[0]) + bfl_ref[0]
    lane = lax.broadcasted_iota(jnp.int32, fl.shape, 1)
    lf_ref[...] = jnp.where(lane < N_HEADS, _log_sigmoid(fl), 0.0)
    hd_row = lax.broadcasted_iota(jnp.int32, (ATT_DIM, LANES), 0) // HEAD_DIM
    hd_col = lax.broadcasted_iota(jnp.int32, (ATT_DIM, LANES), 1)
    sn_ref[...] = _dot(q * k, (hd_row == hd_col).astype(F32), precision=HIGHEST)


def _inproj_sample(x, g_pre, wg, wqkv, wfl, bfl, conv_w, buf0, buf1, layer):
    n = x.shape[0]
    full = lambda s: pl.BlockSpec(s, lambda i: (0,) * len(s))
    weights = (g_pre, wg, wqkv, wfl, bfl, conv_w)
    out_shapes = (
        jax.ShapeDtypeStruct((n, CONV_DIM), BF16),
        jax.ShapeDtypeStruct((n, SSM_DIM), F32),
        jax.ShapeDtypeStruct((n, CONV_DIM), F32),
        jax.ShapeDtypeStruct((n, ATT_DIM), F32),
        jax.ShapeDtypeStruct((n, ATT_DIM), F32),
        jax.ShapeDtypeStruct((n, ATT_DIM), F32),
        jax.ShapeDtypeStruct((n, LANES), F32),
        jax.ShapeDtypeStruct((n, LANES), F32),
    )
    args = (x,) + weights + (buf0, buf1)
    return pl.pallas_call(
        _inproj_sample_kernel,
        grid=(1,),
        in_specs=[full(x.shape)] + [_layer_spec(w, layer) for w in weights] + [full(buf0.shape), full(buf1.shape)],
        out_specs=tuple(full(s.shape) for s in out_shapes),
        out_shape=out_shapes,
        compiler_params=_params(("arbitrary",)),
        name="inproj_sample",
    )(*args)


def _ssm_chunk_state_kernel(u_ref, w_ref, o_ref):
    o_ref[...] = _dot(u_ref[...], w_ref[0])


def _ssm_chunk_state(uflat, w_g, layer, *, tm, tn):
    m, k = uflat.shape
    n = w_g.shape[2]
    return pl.pallas_call(
        _ssm_chunk_state_kernel,
        grid=(n // tn, m // tm),
        in_specs=[pl.BlockSpec((tm, k), lambda j, i: (i, 0)),
                  pl.BlockSpec((1, k, tn), lambda j, i: (layer, 0, j))],
        out_specs=pl.BlockSpec((tm, tn), lambda j, i: (i, j)),
        out_shape=jax.ShapeDtypeStruct((m, n), F32),
        compiler_params=_params(("arbitrary", "arbitrary")),
        name="ssm_chunk_state",
    )(uflat, w_g)


def _ssm_scan_kernel(g_ref, lam_ref, hin_ref, hlast_ref, *, n_chunks):
    lr = lam_ref[0, 2:3, :]
    li = lam_ref[0, 3:4, :]

    def body(c, carry):
        h_re, h_im = carry
        hin_ref[pl.ds(c, 1), 0:STATE_W] = h_re
        hin_ref[pl.ds(c, 1), STATE_W:] = h_im
        g_re = g_ref[pl.ds(c, 1), 0:STATE_W]
        g_im = g_ref[pl.ds(c, 1), STATE_W:]
        return lr * h_re - li * h_im + g_re, lr * h_im + li * h_re + g_im

    zero = jnp.zeros((1, STATE_W), F32)
    h_re, h_im = lax.fori_loop(0, n_chunks, body, (zero, zero))
    hlast_ref[0, :, 0:STATE_W] = h_re
    hlast_ref[0, :, STATE_W:] = h_im


def _ssm_scan(gstate, lam, layer, *, chunks_per_seq):
    m, w = gstate.shape
    nseq = m // chunks_per_seq
    return pl.pallas_call(
        functools.partial(_ssm_scan_kernel, n_chunks=chunks_per_seq),
        grid=(nseq,),
        in_specs=[pl.BlockSpec((chunks_per_seq, w), lambda b: (b, 0)),
                  pl.BlockSpec((1, 8, STATE_W), lambda b: (layer, 0, 0))],
        out_specs=(pl.BlockSpec((chunks_per_seq, w), lambda b: (b, 0)),
                   pl.BlockSpec((1, 1, w), lambda b: (b, 0, 0))),
        out_shape=(jax.ShapeDtypeStruct((m, w), F32), jax.ShapeDtypeStruct((nseq, 1, w), F32)),
        compiler_params=_params(("arbitrary",)),
        name="ssm_scan",
    )(gstate, lam)


def _ssm_output_kernel(ub_ref, hin_ref, uf_ref, bd_ref, wct_ref, d_ref, y_ref, hb_scr):
    j = pl.program_id(1)
    cw = SSM_DIM

    @pl.when(j == 0)
    def _():
        hb_scr[...] = hin_ref[...].astype(BF16)

    y_ref[...] = _dot_nt(hb_scr[...], wct_ref[0]) + d_ref[0] * uf_ref[...]
    for s in range(SSM_CHUNK):
        @pl.when(s <= j)
        def _():
            y_ref[...] += _dot(ub_ref[:, s * cw:(s + 1) * cw], bd_ref[0, j - s])


def _ssm_output(uflat_b, hin, uflat_f, bd, wct, d_row, layer, *, tm):
    m, k = uflat_b.shape
    ks = hin.shape[1]
    cw = SSM_DIM
    return pl.pallas_call(
        _ssm_output_kernel,
        grid=(m // tm, SSM_CHUNK),
        in_specs=[pl.BlockSpec((tm, k), lambda i, j: (i, 0)),
                  pl.BlockSpec((tm, ks), lambda i, j: (i, 0)),
                  pl.BlockSpec((tm, cw), lambda i, j: (i, j)),
                  pl.BlockSpec((1, SSM_CHUNK, cw, cw), lambda i, j: (layer, 0, 0, 0)),
                  pl.BlockSpec((1, cw, ks), lambda i, j: (layer, j, 0)),
                  pl.BlockSpec((1, 1, cw), lambda i, j: (layer, 0, 0))],
        out_specs=pl.BlockSpec((tm, cw), lambda i, j: (i, j)),
        out_shape=jax.ShapeDtypeStruct((m, k), F32),
        scratch_shapes=[pltpu.VMEM((tm, ks), BF16)],
        compiler_params=_params(("arbitrary", "arbitrary")),
        name="ssm_output",
    )(uflat_b, hin, uflat_f, bd, wct, d_row)


def _ssm_step_kernel(u_ref, hre_ref, him_ref, b1_ref, lam_ref, c1t_ref, d_ref,
                     y_ref, ore_ref, oim_ref):
    u = u_ref[...]
    lr = lam_ref[0, 0:1, :]
    li = lam_ref[0, 1:2, :]
    h_re = hre_ref[...]
    h_im = him_ref[...]
    bu = _dot(u, b1_ref[0], precision=HIGHEST)
    n_re = lr * h_re - li * h_im + bu[:, 0:STATE_W]
    n_im = lr * h_im + li * h_re + bu[:, STATE_W:]
    ore_ref[...] = n_re
    oim_ref[...] = n_im
    y = (_dot_nt(n_re, c1t_ref[0, :, 0:STATE_W], precision=HIGHEST)
         + _dot_nt(n_im, c1t_ref[0, :, STATE_W:], precision=HIGHEST))
    y_ref[...] = y + d_ref[0] * u


def _ssm_step(u, h_re, h_im, b1, lam, c1t, d, layer):
    n = u.shape[0]
    full = lambda s: pl.BlockSpec(s, lambda i: (0,) * len(s))
    lay = lambda s: pl.BlockSpec((1,) + s[1:], lambda i: (layer,) + (0,) * (len(s) - 1))
    out_shapes = (jax.ShapeDtypeStruct((n, SSM_DIM), F32), jax.ShapeDtypeStruct((n, STATE_W), F32),
                  jax.ShapeDtypeStruct((n, STATE_W), F32))
    return pl.pallas_call(
        _ssm_step_kernel,
        grid=(1,),
        in_specs=[full(u.shape), full(h_re.shape), full(h_im.shape), lay(b1.shape), lay(lam.shape),
                  lay(c1t.shape), lay(d.shape)],
        out_specs=tuple(full(s.shape) for s in out_shapes),
        out_shape=out_shapes,
        compiler_params=_params(("arbitrary",)),
        name="ssm_step",
    )(u, h_re, h_im, b1, lam, c1t, d)


def _fox_prompt_kernel(qi_ref, ki_ref, q_ref, k_ref, v_ref, cq_ref, ck_ref, o_ref,
                       m_scr, l_scr, acc_scr, cq_scr, *, tq, tk, rb, kpt):
    qi = qi_ref[pl.program_id(2)]
    kg = ki_ref[pl.program_id(2)]

    @pl.when(kg == 0)
    def _():
        m_scr[...] = jnp.full(m_scr.shape, -jnp.inf, F32)
        l_scr[...] = jnp.zeros_like(l_scr)
        acc_scr[...] = jnp.zeros_like(acc_scr)
        src_lane = lax.broadcasted_iota(jnp.int32, (LANES, 2 * LANES), 0)
        dst_head = lax.broadcasted_iota(jnp.int32, (LANES, 2 * LANES), 1) // LANES
        pick = (src_lane == 2 * pl.program_id(1) + dst_head).astype(F32).astype(BF16)
        hi, mid, lo = _split3(cq_ref[...])
        spread = (_dot(hi, pick) + _dot(mid, pick) + _dot(lo, pick)) * LOG2E
        for hh in range(2):
            cq_scr[hh] = spread[:, hh * LANES:(hh + 1) * LANES]

    def step(on_diagonal, sub):
        keys = slice(sub * tk, (sub + 1) * tk)
        k2 = k_ref[keys, :]
        v2 = v_ref[keys, :]
        lane = lax.broadcasted_iota(jnp.int32, (rb, LANES), 1)
        keep = [(lane < HEAD_DIM).astype(F32).astype(BF16), (lane >= HEAD_DIM).astype(F32).astype(BF16)]
        lane_o = lane < HEAD_DIM
        ck2 = [ck_ref[0, hh:hh + 1, keys] * LOG2E for hh in range(2)]
        nkeys = [min(tk, r0 + rb) if on_diagonal else tk for r0 in range(0, tq, rb)]
        scores = [[_dot_nt(q_ref[r0:r0 + rb, :] * keep[hh], k2[0:nkeys[bi], :]) for hh in range(2)]
                  for bi, r0 in enumerate(range(0, tq, rb))]
        for bi, r0 in enumerate(range(0, tq, rb)):
            rows = slice(r0, r0 + rb)
            reps = nkeys[bi] // LANES
            if on_diagonal:
                rel = (lax.broadcasted_iota(jnp.int32, (rb, LANES), 1)
                       - lax.broadcasted_iota(jnp.int32, (rb, LANES), 0))
            alphas = []
            pvs = []
            for hh in range(2):
                sc = []
                for c in range(reps):
                    cols = slice(c * LANES, (c + 1) * LANES)
                    s_c = scores[bi][hh][:, cols] - ck2[hh][:, cols]
                    if on_diagonal:
                        s_c = jnp.where(rel <= r0 - c * LANES, s_c, -jnp.inf)
                    sc.append(s_c)
                mx = sc[0]
                for c in range(1, reps):
                    mx = jnp.maximum(mx, sc[c])
                cq = cq_scr[hh, rows, :]
                m_prev = m_scr[hh, rows, :]
                m_new = jnp.maximum(m_prev, jnp.max(mx, axis=-1, keepdims=True) + cq)
                t = m_new - cq
                ps = [jnp.exp2(s_c - t) for s_c in sc]
                tot = ps[0]
                for c in range(1, reps):
                    tot = tot + ps[c]
                alpha = jnp.exp2(m_prev - m_new)
                l_scr[hh, rows, :] = alpha * l_scr[hh, rows, :] + jnp.sum(tot, axis=-1, keepdims=True)
                m_scr[hh, rows, :] = m_new
                alphas.append(alpha)
                p = jnp.concatenate([p_c.astype(BF16) for p_c in ps], axis=1)
                pvs.append(_dot(p, v2[0:nkeys[bi], :]))
            acc_scr[rows, :] = (jnp.where(lane_o, alphas[0], alphas[1]) * acc_scr[rows, :]
                                + jnp.where(lane_o, pvs[0], pvs[1]))

    def finish():
        lane_o = lax.broadcasted_iota(jnp.int32, (tq, LANES), 1) < HEAD_DIM
        o_ref[...] = (acc_scr[...] / jnp.where(lane_o, l_scr[0], l_scr[1])).astype(o_ref.dtype)

    @pl.when(kg * kpt + kpt - 1 < qi)
    def _():
        for sub in range(kpt):
            step(False, sub)

    for diag in range(kpt):
        @pl.when(qi == kg * kpt + diag)
        def _():
            for sub in range(diag):
                step(False, sub)
            step(True, diag)
            finish()


def _fox_prompt(q, kb, vb, ct, cq, *, seq_len, tq):
    n = q.shape[0]
    nseq = n // seq_len
    nq = seq_len // tq
    hp = N_HEADS // 2
    ck = ct.reshape(hp, 2, n)
    kpt = next(c for c in (8, 4, 2, 1) if nq % c == 0)
    ng = nq // kpt
    pairs = [(i, g) for i in range(nq) for g in range(i // kpt + 1)]
    qi_tab = jnp.asarray([p[0] for p in pairs], jnp.int32)
    ki_tab = jnp.asarray([p[1] for p in pairs], jnp.int32)
    q_idx = lambda b, h, t, qt, kt: (b * nq + qt[t], h)
    kv_idx = lambda b, h, t, qt, kt: (b * ng + kt[t], h)
    grid_spec = pltpu.PrefetchScalarGridSpec(
        num_scalar_prefetch=2,
        grid=(nseq, hp, len(pairs)),
        in_specs=[pl.BlockSpec((tq, LANES), q_idx),
                  pl.BlockSpec((kpt * tq, LANES), kv_idx),
                  pl.BlockSpec((kpt * tq, LANES), kv_idx),
                  pl.BlockSpec((tq, LANES), lambda b, h, t, qt, kt: (b * nq + qt[t], 0)),
                  pl.BlockSpec((1, 2, kpt * tq), lambda b, h, t, qt, kt: (h, 0, b * ng + kt[t]))],
        out_specs=pl.BlockSpec((tq, LANES), q_idx),
        scratch_shapes=[pltpu.VMEM((2, tq, LANES), F32), pltpu.VMEM((2, tq, LANES), F32),
                        pltpu.VMEM((tq, LANES), F32), pltpu.VMEM((2, tq, LANES), F32)],
    )
    return pl.pallas_call(
        functools.partial(_fox_prompt_kernel, tq=tq, tk=tq, rb=min(256, tq), kpt=kpt),
        grid_spec=grid_spec,
        out_shape=jax.ShapeDtypeStruct((n, ATT_DIM), BF16),
        compiler_params=_params(("arbitrary", "arbitrary", "arbitrary")),
        name="fox_prompt",
    )(qi_tab, ki_tab, q, kb, vb, cq, ck)


PAGES_PER_STEP = 8


def _decode_init(vn_ref, sn_ref, lfn_ref, m_scr, l_scr, car_scr, acc_scr, page_size):
    eye = (lax.broadcasted_iota(jnp.int32, (N_HEADS, LANES), 0)
           == lax.broadcasted_iota(jnp.int32, (N_HEADS, LANES), 1)).astype(F32)

    def to_col(row):
        return jnp.sum(eye * row, axis=-1, keepdims=True)

    m_scr[...] = to_col(sn_ref[0])
    l_scr[...] = jnp.ones_like(l_scr)
    car_scr[...] = to_col(lfn_ref[0])
    lane0 = lax.broadcasted_iota(jnp.int32, (HEAD_DIM, page_size), 1) == 0
    for h in range(N_HEADS):
        vcol = vn_ref[0, h * HEAD_DIM:(h + 1) * HEAD_DIM, :]
        acc_scr[h] = jnp.where(lane0, vcol, 0.0)


def _decode_finish(o_ref, l_scr, acc_scr):
    inv = 1.0 / l_scr[...]
    for h in range(N_HEADS):
        o_ref[0, h] = jnp.sum(acc_scr[h], axis=-1, keepdims=True) * inv[h:h + 1, :]


def _decode_update(q_ref, k_refs, v_refs, lf_refs, m_scr, l_scr, car_scr, acc_scr, page_size):
    np_ = PAGES_PER_STEP
    later = (lax.broadcasted_iota(jnp.int32, (page_size, page_size), 0)
             > lax.broadcasted_iota(jnp.int32, (page_size, page_size), 1)).astype(F32)
    fold = (lax.broadcasted_iota(jnp.int32, (N_HEADS, N_HEADS * 8), 1) // 8
            == lax.broadcasted_iota(jnp.int32, (N_HEADS, N_HEADS * 8), 0)).astype(F32)
    partial_sums = []
    for r in range(np_):
        parts = []
        for h in range(N_HEADS):
            prod = k_refs[r][0, 0, h] * q_ref[0, h * HEAD_DIM:(h + 1) * HEAD_DIM, :]
            t = prod[0:8]
            for a in range(1, HEAD_DIM // 8):
                t = t + prod[a * 8:(a + 1) * 8]
            parts.append(t)
        partial_sums.append(jnp.concatenate(parts, axis=0))
    s_qk = _dot(fold, jnp.concatenate(partial_sums, axis=1), precision=HIGHEST)
    lf_all = jnp.concatenate([lf_refs[r][0, 0] for r in range(np_)], axis=0)
    within = _dot(lf_all, later, precision=HIGHEST)
    page_tot = jnp.sum(lf_all, axis=-1, keepdims=True)
    carry = car_scr[...]
    bias = []
    for r in range(np_):
        bias.append(within[r * N_HEADS:(r + 1) * N_HEADS] + carry)
        carry = carry + page_tot[r * N_HEADS:(r + 1) * N_HEADS]
    car_scr[...] = carry
    s_all = s_qk + jnp.concatenate(bias, axis=1)
    m_prev = m_scr[...]
    m_new = jnp.maximum(m_prev, jnp.max(s_all, axis=-1, keepdims=True))
    alpha = jnp.exp(m_prev - m_new)
    p_all = jnp.exp(s_all - m_new)
    l_scr[...] = alpha * l_scr[...] + jnp.sum(p_all, axis=-1, keepdims=True)
    m_scr[...] = m_new
    for h in range(N_HEADS):
        a = acc_scr[h] * alpha[h:h + 1, :]
        for r in range(np_):
            a = a + p_all[h:h + 1, r * page_size:(r + 1) * page_size] * v_refs[r][0, 0, h]
        acc_scr[h] = a


def _fox_sample_kernel(pt_ref, q_ref, vn_ref, sn_ref, lfn_ref, *refs, page_size):
    np_ = PAGES_PER_STEP
    k_refs = refs[0:np_]
    v_refs = refs[np_:2 * np_]
    lf_refs = refs[2 * np_:3 * np_]
    o_ref = refs[3 * np_]
    m_scr, l_scr, car_scr, acc_scr = refs[3 * np_ + 1:]
    j = pl.program_id(1)

    @pl.when(j == 0)
    def _():
        _decode_init(vn_ref, sn_ref, lfn_ref, m_scr, l_scr, car_scr, acc_scr, page_size)

    _decode_update(q_ref, k_refs, v_refs, lf_refs, m_scr, l_scr, car_scr, acc_scr, page_size)

    @pl.when(j == pl.num_programs(1) - 1)
    def _():
        _decode_finish(o_ref, l_scr, acc_scr)


def _decode_operands(page_table, q, v_new, s_new, lf_new, cache_kt, cache_vt, cache_lft, layer):
    nseq, n_pages = page_table.shape
    page_size = cache_kt.shape[-1]
    np_ = PAGES_PER_STEP

    def page_idx(r, nd):
        def f(b, j, pt):
            return (layer, pt[b * n_pages + (n_pages - 1 - (j * np_ + r))]) + (0,) * nd
        return f

    col = pl.BlockSpec((1, ATT_DIM, 1), lambda b, j, pt: (b, 0, 0))
    tok = pl.BlockSpec((1, 1, LANES), lambda b, j, pt: (b, 0, 0))
    kv_block = (1, 1, N_HEADS, HEAD_DIM, page_size)
    in_specs = [col, col, tok, tok]
    in_specs += [pl.BlockSpec(kv_block, page_idx(r, 3)) for r in range(np_)]
    in_specs += [pl.BlockSpec(kv_block, page_idx(r, 3)) for r in range(np_)]
    in_specs += [pl.BlockSpec((1, 1, N_HEADS, page_size), page_idx(r, 2)) for r in range(np_)]
    args = [q.reshape(nseq, ATT_DIM, 1), v_new.reshape(nseq, ATT_DIM, 1),
            s_new.reshape(nseq, 1, LANES), lf_new.reshape(nseq, 1, LANES)]
    args += [cache_kt] * np_ + [cache_vt] * np_ + [cache_lft] * np_
    out_spec = pl.BlockSpec((1, N_HEADS, HEAD_DIM, 1), lambda b, j, pt: (b, 0, 0, 0))
    out_shape = jax.ShapeDtypeStruct((nseq, N_HEADS, HEAD_DIM, 1), F32)
    scratch = [pltpu.VMEM((N_HEADS, 1), F32), pltpu.VMEM((N_HEADS, 1), F32),
               pltpu.VMEM((N_HEADS, 1), F32), pltpu.VMEM((N_HEADS, HEAD_DIM, page_size), F32)]
    return in_specs, args, out_spec, out_shape, scratch, n_pages // np_, page_size


def _fox_sample(page_table, q, v_new, s_new, lf_new, cache_kt, cache_vt, cache_lft, *, layer):
    nseq = page_table.shape[0]
    in_specs, args, out_spec, out_shape, scratch, nj, page_size = _decode_operands(
        page_table, q, v_new, s_new, lf_new, cache_kt, cache_vt, cache_lft, layer)
    grid_spec = pltpu.PrefetchScalarGridSpec(
        num_scalar_prefetch=1, grid=(nseq, nj), in_specs=in_specs, out_specs=out_spec, scratch_shapes=scratch)
    out = pl.pallas_call(
        functools.partial(_fox_sample_kernel, page_size=page_size),
        grid_spec=grid_spec,
        out_shape=out_shape,
        compiler_params=_params(("arbitrary", "arbitrary")),
        name="fox_sample",
    )(page_table.reshape(-1), *args)
    return out.reshape(nseq, ATT_DIM).astype(BF16)


def _post_kernel(x_ref, convo_ref, ssmy_ref, att_ref, p_ref, wglu_ref, wout_ref, gpost_ref, gfpre_ref,
                 wgate_ref, wup_ref, wdown_ref, gfpost_ref, wpg_ref, wpp_ref, o_ref):
    y = _gelu_tanh(ssmy_ref[...])
    ssm_out = y * _sigmoid(_dot(y.astype(BF16), wglu_ref[0]))
    mix = (_dot(convo_ref[...], wout_ref[0, 0:CONV_DIM, :])
           + _dot(ssm_out.astype(BF16), wout_ref[0, CONV_DIM:CONV_DIM + SSM_DIM, :])
           + _dot(att_ref[...], wout_ref[0, CONV_DIM + SSM_DIM:, :]))
    x1 = x_ref[...] + _rms(mix, gpost_ref[0])
    h2 = _rms(x1, gfpre_ref[0]).astype(BF16)
    ffn = jnp.zeros(x1.shape, F32)
    for c in range(D_FF // FF_CHUNK):
        cols = slice(c * FF_CHUNK, (c + 1) * FF_CHUNK)
        gate = _dot(h2, wgate_ref[0, :, cols])
        up = _dot(h2, wup_ref[0, :, cols])
        act = (gate * _sigmoid(gate) * up).astype(BF16)
        ffn = ffn + _dot(act, wdown_ref[0, cols, :])
    x2 = x1 + _rms(ffn, gfpost_ref[0])
    pgate = _sigmoid(_dot(x2.astype(BF16), wpg_ref[0]))
    o_ref[...] = x2 + pgate * _dot(p_ref[0].astype(BF16), wpp_ref[0])


def _post(x, convo, ssmy, att, pemb, wglu, wout, gpost, gfpre, wgate, wup, wdown, gfpost, wpg, wpp, layer, *, tm):
    n = x.shape[0]
    row = lambda w: pl.BlockSpec((tm, w), lambda i: (i, 0))
    weights = (wglu, wout, gpost, gfpre, wgate, wup, wdown, gfpost, wpg, wpp)
    return pl.pallas_call(
        _post_kernel,
        grid=(n // tm,),
        in_specs=[row(D_MODEL), row(CONV_DIM), row(SSM_DIM), row(ATT_DIM),
                  pl.BlockSpec((1, tm, PLE_DIM), lambda i: (layer, i, 0))]
                 + [_layer_spec(w, layer) for w in weights],
        out_specs=row(D_MODEL),
        out_shape=jax.ShapeDtypeStruct((n, D_MODEL), F32),
        compiler_params=_params(("arbitrary",)),
        name="post_mixer",
    )(x, convo, ssmy, att, pemb, *weights)


N_POST_WEIGHTS = 10


def _post_schedule(nsub):
    n_ff = D_FF // FF_CHUNK
    cost = ([("head", SSM_DIM * SSM_DIM + D_MODEL * D_MODEL)]
            + [("ffn", 3 * D_MODEL * FF_CHUNK)] * n_ff
            + [("gate", D_MODEL * D_MODEL), ("tail", PLE_DIM * D_MODEL)])
    total = sum(c for _, c in cost)
    groups = [[] for _ in range(nsub)]
    done = 0
    ff = 0
    for name, c in cost:
        g = min(nsub - 1, int((done + c / 2) * nsub / total))
        groups[g].append((name, ff))
        ff += name == "ffn"
        done += c
    return groups


def _post_decode_kernel(pt_ref, x_ref, convo_ref, ssmy_ref, att_ref, p_ref, *refs, page_size, nsub):
    (wglu_ref, wout_ref, gpost_ref, gfpre_ref, wgate_ref, wup_ref, wdown_ref, gfpost_ref,
     wpg_ref, wpp_ref) = refs[0:N_POST_WEIGHTS]
    q_ref, vn_ref, sn_ref, lfn_ref = refs[N_POST_WEIGHTS:N_POST_WEIGHTS + 4]
    np_ = PAGES_PER_STEP
    base = N_POST_WEIGHTS + 4
    k_refs = refs[base:base + np_]
    v_refs = refs[base + np_:base + 2 * np_]
    lf_refs = refs[base + 2 * np_:base + 3 * np_]
    o_ref, od_ref = refs[base + 3 * np_:base + 3 * np_ + 2]
    x1_scr, h2_scr, ffn_scr, ylo_scr, yhi_scr, m_scr, l_scr, car_scr, acc_scr = refs[base + 3 * np_ + 2:]
    step = pl.program_id(1)

    def head():
        tc = ssmy_ref.shape[0]
        for half, scr in enumerate((ylo_scr, yhi_scr)):
            for t in range(SSM_CHUNK):
                c0 = t * SSM_DIM + half * LANES
                scr[pl.ds(t, tc, stride=SSM_CHUNK), :] = ssmy_ref[:, c0:c0 + LANES]
        y = _gelu_tanh(jnp.concatenate([ylo_scr[...], yhi_scr[...]], axis=1))
        ssm_out = y * _sigmoid(_dot(y.astype(BF16), wglu_ref[0]))
        mix = (_dot(convo_ref[...], wout_ref[0, 0:CONV_DIM, :])
               + _dot(ssm_out.astype(BF16), wout_ref[0, CONV_DIM:CONV_DIM + SSM_DIM, :])
               + _dot(att_ref[...], wout_ref[0, CONV_DIM + SSM_DIM:, :]))
        x1 = x_ref[...] + _rms(mix, gpost_ref[0])
        x1_scr[...] = x1
        h2_scr[...] = _rms(x1, gfpre_ref[0]).astype(BF16)

    def ffn(c):
        cols = slice(c * FF_CHUNK, (c + 1) * FF_CHUNK)
        h2 = h2_scr[...]
        gate = _dot(h2, wgate_ref[0, :, cols])
        up = _dot(h2, wup_ref[0, :, cols])
        part = _dot((gate * _sigmoid(gate) * up).astype(BF16), wdown_ref[0, cols, :])
        if c == 0:
            ffn_scr[...] = part
        else:
            ffn_scr[...] += part

    def gate():
        x2 = x1_scr[...] + _rms(ffn_scr[...], gfpost_ref[0])
        x1_scr[...] = x2
        ffn_scr[...] = _sigmoid(_dot(x2.astype(BF16), wpg_ref[0]))

    def tail():
        o_ref[...] = x1_scr[...] + ffn_scr[...] * _dot(p_ref[0].astype(BF16), wpp_ref[0])

    work = {"head": lambda c: head(), "ffn": ffn, "gate": lambda c: gate(), "tail": lambda c: tail()}
    for k, items in enumerate(_post_schedule(nsub)):
        @pl.when(step == k)
        def _():
            lead = 1 if (len(items) >= 2 and all(name == "ffn" for name, _ in items)) else 0
            for name, c in items[:lead]:
                work[name](c)
            if k == 0:
                _decode_init(vn_ref, sn_ref, lfn_ref, m_scr, l_scr, car_scr, acc_scr, page_size)
            _decode_update(q_ref, k_refs, v_refs, lf_refs, m_scr, l_scr, car_scr, acc_scr, page_size)
            if k == nsub - 1:
                _decode_finish(od_ref, l_scr, acc_scr)
            for name, c in items[lead:]:
                work[name](c)


def _post_decode(x, convo, ssmy, att, pemb, weights, page_table, q, v_new, s_new, lf_new,
                 cache_kt, cache_vt, cache_lft, layer, *, tm):
    n = x.shape[0]
    nseq = page_table.shape[0]
    dec_specs, dec_args, dec_out_spec, dec_out_shape, dec_scratch, nsub, page_size = _decode_operands(
        page_table, q, v_new, s_new, lf_new, cache_kt, cache_vt, cache_lft, layer)
    row = lambda w: pl.BlockSpec((tm, w), lambda i, s, pt: (i, 0))
    grid_spec = pltpu.PrefetchScalarGridSpec(
        num_scalar_prefetch=1,
        grid=(nseq, nsub),
        in_specs=[row(D_MODEL), row(CONV_DIM),
                  pl.BlockSpec((tm // SSM_CHUNK, CHUNK_W), lambda i, s, pt: (i, 0)), row(ATT_DIM),
                  pl.BlockSpec((1, tm, PLE_DIM), lambda i, s, pt: (layer, i, 0))]
                 + [_layer_spec(w, layer) for w in weights] + dec_specs,
        out_specs=(row(D_MODEL), dec_out_spec),
        scratch_shapes=[pltpu.VMEM((tm, D_MODEL), F32), pltpu.VMEM((tm, D_MODEL), BF16),
                        pltpu.VMEM((tm, D_MODEL), F32), pltpu.VMEM((tm, LANES), F32),
                        pltpu.VMEM((tm, LANES), F32)] + dec_scratch,
    )
    xo, att_s = pl.pallas_call(
        functools.partial(_post_decode_kernel, page_size=page_size, nsub=nsub),
        grid_spec=grid_spec,
        out_shape=(jax.ShapeDtypeStruct((n, D_MODEL), F32), dec_out_shape),
        compiler_params=_params(("arbitrary", "arbitrary")),
        name="post_mixer_decode",
    )(page_table.reshape(-1), x, convo, ssmy, att, pemb, *weights, *dec_args)
    return xo, att_s.reshape(nseq, ATT_DIM).astype(BF16)


def _kv_gather_kernel(*refs, depth, nseq):
    in_refs = refs[0:depth]
    o_ref = refs[depth]
    layer = pl.program_id(0)
    for l in range(depth):
        @pl.when(layer == l)
        def _():
            for b in range(nseq):
                o_ref[0, b] = in_refs[l][b].T.reshape(N_HEADS, HEAD_DIM, o_ref.shape[-1])


def _kv_gather(per_layer, *, nseq, seq_len):
    depth = len(per_layer)
    tt = min(256, seq_len)
    nt = seq_len // tt
    views = [a.reshape(nseq, seq_len, ATT_DIM) for a in per_layer]

    def in_spec(l):
        return pl.BlockSpec((nseq, tt, ATT_DIM),
                            lambda d, t: (0, jnp.where(d == l, t, jnp.where(d < l, 0, nt - 1)), 0))

    out = pl.pallas_call(
        functools.partial(_kv_gather_kernel, depth=depth, nseq=nseq),
        grid=(depth, nt),
        in_specs=[in_spec(l) for l in range(depth)],
        out_specs=pl.BlockSpec((1, nseq, N_HEADS, HEAD_DIM, tt), lambda d, t: (d, 0, 0, 0, t)),
        out_shape=jax.ShapeDtypeStruct((depth, nseq, N_HEADS, HEAD_DIM, seq_len), F32),
        compiler_params=_params(("arbitrary", "arbitrary")),
        name="kv_gather",
    )(*views)
    return jnp.transpose(out, (0, 1, 4, 2, 3))


def kernel(x_prompt, x_sample, cache_k, cache_v, cache_logf, state_conv, state_ssm_re, state_ssm_im, page_table, p_prompt, p_sample, norm_mix_pre, norm_mix_post, norm_ffn_pre, norm_ffn_post, w_in, b_forget, conv_w, ssm_a_re, ssm_a_im, ssm_log_dt, ssm_b_re, ssm_b_im, ssm_c_re, ssm_c_im, ssm_d, w_ssm_glu, w_out, w_ffn_gate, w_ffn_up, w_ffn_down, w_ple_gate, w_ple_proj):
    depth = w_in.shape[0]
    bp, seq_len, _ = x_prompt.shape
    bs = x_sample.shape[0]
    n_p = bp * seq_len
    tm = min(512, seq_len)
    tq = min(512, seq_len)
    chunks_per_seq = seq_len // SSM_CHUNK
    n_chunks = n_p // SSM_CHUNK
    tmc = min(512, n_chunks)
    tm_fused = n_p // bs
    fuse_decode = (n_p % bs == 0 and tm_fused % 16 == 0 and tm_fused <= 512
                   and page_table.shape[1] % PAGES_PER_STEP == 0)

    gate_cols = 3 * CONV_DIM + SSM_DIM
    wg = w_in[:, :, 0:gate_cols].astype(BF16)
    wqkv = w_in[:, :, gate_cols:gate_cols + 3 * ATT_DIM].astype(BF16)
    wfl = jnp.pad(w_in[:, :, gate_cols + 3 * ATT_DIM:], ((0, 0), (0, 0), (0, LANES - N_HEADS))).astype(BF16)
    bfl = jnp.pad(b_forget, ((0, 0), (0, LANES - N_HEADS)))[:, None, :]
    wglu = w_ssm_glu.astype(BF16)
    wout = w_out.astype(BF16)
    wgate = w_ffn_gate.astype(BF16)
    wup = w_ffn_up.astype(BF16)
    wdown = w_ffn_down.astype(BF16)
    wpg = w_ple_gate.astype(BF16)
    wpp = w_ple_proj.astype(BF16)
    g_pre = norm_mix_pre[:, None, :]
    g_post = norm_mix_post[:, None, :]
    g_fpre = norm_ffn_pre[:, None, :]
    g_fpost = norm_ffn_post[:, None, :]
    d_row = ssm_d[:, None, :]

    w_g, w_ct, bd, b1, c1t, lam = _ssm_prep(ssm_a_re, ssm_a_im, ssm_log_dt, ssm_b_re, ssm_b_im,
                                            ssm_c_re, ssm_c_im)

    cache_kt = jnp.transpose(cache_k, (0, 1, 3, 4, 2))
    cache_vt = jnp.transpose(cache_v, (0, 1, 3, 4, 2))
    cache_lft = jnp.transpose(cache_logf, (0, 1, 3, 2))

    xp = x_prompt.reshape(n_p, D_MODEL)
    xs = x_sample.reshape(bs, D_MODEL)
    pe_prompt = p_prompt.reshape(depth, n_p, PLE_DIM)
    pe_sample = p_sample.reshape(depth, bs, PLE_DIM)
    outs_p = [[] for _ in range(6)]
    outs_s = [[] for _ in range(6)]
    for i in range(depth):
        (convo, uflat_f, uflat_b, q, k, v, kb, vb, lf, ct, crow, conv_new) = _inproj_prompt(
            xp, g_pre, wg, wqkv, wfl, bfl, conv_w, i, seq_len=seq_len, tm=tm)
        gstate = _ssm_chunk_state(uflat_b, w_g, i, tm=tmc, tn=1024)
        hin, hlast = _ssm_scan(gstate, lam, i, chunks_per_seq=chunks_per_seq)
        ssmy = _ssm_output(uflat_b, hin, uflat_f, bd, w_ct, d_row, i, tm=tmc)
        att = _fox_prompt(q, kb, vb, ct, crow, seq_len=seq_len, tq=tq)
        outs_p[0].append(k)
        outs_p[1].append(v)
        outs_p[2].append(lf.reshape(bp, seq_len, N_HEADS))
        outs_p[3].append(conv_new)
        hl = hlast.reshape(bp, 2, SSM_GROUPS, SSM_STATE)
        outs_p[4].append(hl[:, 0])
        outs_p[5].append(hl[:, 1])

        (convo_s, su_s, u_s, q_s, k_s, v_s, lf_s, sn_s) = _inproj_sample(
            xs, g_pre, wg, wqkv, wfl, bfl, conv_w, state_conv[i, :, 0], state_conv[i, :, 1], i)
        ssmy_s, hre_s, him_s = _ssm_step(
            su_s, state_ssm_re[i].reshape(bs, STATE_W), state_ssm_im[i].reshape(bs, STATE_W),
            b1, lam, c1t, d_row, i)
        post_w = (wglu, wout, g_post, g_fpre, wgate, wup, wdown, g_fpost, wpg, wpp)
        if fuse_decode:
            xp, att_s = _post_decode(xp, convo, ssmy, att, pe_prompt, post_w,
                                     page_table, q_s, v_s, sn_s, lf_s, cache_kt, cache_vt, cache_lft, i,
                                     tm=tm_fused)
        else:
            xp = _post(xp, convo, ssmy.reshape(n_p, SSM_DIM), att, pe_prompt, *post_w, i, tm=tm)
            att_s = _fox_sample(page_table, q_s, v_s, sn_s, lf_s, cache_kt, cache_vt, cache_lft, layer=i)
        xs = _post(xs, convo_s, ssmy_s, att_s, pe_sample,
                   wglu, wout, g_post, g_fpre, wgate, wup, wdown, g_fpost, wpg, wpp, i, tm=bs)
        outs_s[0].append(k_s.reshape(bs, 1, N_HEADS, HEAD_DIM))
        outs_s[1].append(v_s.reshape(bs, 1, N_HEADS, HEAD_DIM))
        outs_s[2].append(lf_s[:, 0:N_HEADS].reshape(bs, 1, N_HEADS))
        outs_s[3].append(jnp.stack([state_conv[i, :, 1], u_s], axis=1))
        outs_s[4].append(hre_s.reshape(bs, SSM_GROUPS, SSM_STATE))
        outs_s[5].append(him_s.reshape(bs, SSM_GROUPS, SSM_STATE))

    k_p = _kv_gather(outs_p[0], nseq=bp, seq_len=seq_len)
    v_p = _kv_gather(outs_p[1], nseq=bp, seq_len=seq_len)
    lf_p, conv_p, re_p, im_p = [jnp.stack(a) for a in outs_p[2:]]
    k_s, v_s, lf_s, conv_s, re_s, im_s = [jnp.stack(a) for a in outs_s]
    return (xp.reshape(bp, seq_len, D_MODEL), xs.reshape(bs, 1, D_MODEL),
            k_p, v_p, lf_p, conv_p, re_p, im_p, k_s, v_s, lf_s, conv_s, re_s, im_s)
```
